```python
import math
import jax, jax.numpy as jnp
from jax import lax
import numpy as np

D_MODEL = 1024
BATCH = 2
SEQ = 8192
DEPTH = 2

MOBA_HEADS = 8
MOBA_HEAD_DIM = 64
MOBA_WIDTH = MOBA_HEADS * MOBA_HEAD_DIM
MOBA_BLOCK = 256
MOBA_TOPK = 3
ROPE_THETA = 10000.0
RET_HEADS = 4
RET_QK_DIM = 128
RET_V_DIM = 256
RET_QK_WIDTH = RET_HEADS * RET_QK_DIM
RET_V_WIDTH = RET_HEADS * RET_V_DIM
RET_CHUNK = 256
PEER_N_KEYS = 128
PEER_N_EXPERTS = PEER_N_KEYS * PEER_N_KEYS
PEER_HEADS = 8
PEER_KEY_DIM = 256
PEER_TOPK = 16
PEER_TOKEN_CHUNK = 128
DN_ALPHA = (2.0 * DEPTH) ** 0.25
DN_BETA = (8.0 * DEPTH) ** -0.25
LN_EPS = 1e-5
NEG = -1e30

IN_SIZES = (MOBA_WIDTH, MOBA_WIDTH, MOBA_WIDTH, RET_QK_WIDTH, RET_QK_WIDTH, RET_V_WIDTH, RET_V_WIDTH, D_MODEL, D_MODEL)
IN_WIDTH = sum(IN_SIZES)
SPLIT_POINTS = tuple(int(v) for v in np.cumsum(IN_SIZES)[:-1])

kernel_name = 'hybrid_moba_retnet_peer_deepnorm'


def layer_norm(x, g, b):
    xf = x.astype(jnp.float32)
    mu = xf.mean(-1, keepdims=True)
    var = jnp.square(xf - mu).mean(-1, keepdims=True)
    return ((xf - mu) * lax.rsqrt(var + LN_EPS) * g.astype(jnp.float32) + b.astype(jnp.float32)).astype(x.dtype)


def group_norm(y):
    yf = y.astype(jnp.float32)
    mu = yf.mean(-1, keepdims=True)
    var = jnp.square(yf - mu).mean(-1, keepdims=True)
    return ((yf - mu) * lax.rsqrt(var + LN_EPS)).astype(y.dtype)


def rope_half(x, pos):
    d = x.shape[-1]
    inv = ROPE_THETA ** (-jnp.arange(0, d, 2, dtype=jnp.float32) / d)
    ang = pos[:, None] * inv[None, :]
    cos, sin = jnp.cos(ang), jnp.sin(ang)
    x1, x2 = x[..., : d // 2], x[..., d // 2:]
    return jnp.concatenate([x1 * cos - x2 * sin, x2 * cos + x1 * sin], axis=-1).astype(x.dtype)


def rotate_retnet(x, pos):
    d = x.shape[-1]
    inv = 1.0 / (ROPE_THETA ** jnp.linspace(0.0, 1.0, d // 2, dtype=jnp.float32))
    ang = pos[:, None] * inv[None, :]
    cos, sin = jnp.cos(ang), jnp.sin(ang)
    x1, x2 = x[..., 0::2], x[..., 1::2]
    return jnp.stack([x1 * cos - x2 * sin, x2 * cos + x1 * sin], axis=-1).reshape(x.shape).astype(x.dtype)


def moba_attention(q, k, v):
    B, H, S, d = q.shape
    nb = S // MOBA_BLOCK
    scale = d ** -0.5
    kb = k.reshape(B, H, nb, MOBA_BLOCK, d)
    vb = v.reshape(B, H, nb, MOBA_BLOCK, d)
    qb = jnp.moveaxis(q.reshape(B, H, nb, MOBA_BLOCK, d), 2, 0)
    k_mean = kb.astype(jnp.float32).mean(axis=3)
    n_sel = min(MOBA_TOPK, nb)
    causal = jnp.tril(jnp.ones((MOBA_BLOCK, MOBA_BLOCK), dtype=bool))
    b_ix = jnp.arange(B)[:, None, None, None]
    h_ix = jnp.arange(H)[None, :, None, None]
    blk_ids = jnp.arange(nb)
    slot_ids = jnp.arange(n_sel)

    def one_block(args):
        i, q_i = args
        gate = jnp.einsum('bhld,bhnd->bhln', q_i.astype(jnp.float32), k_mean)
        gate = jnp.where(blk_ids < i, gate, NEG)
        _, top_idx = lax.top_k(gate, n_sel)
        valid = slot_ids < i
        k_sel = kb[b_ix, h_ix, top_idx]
        v_sel = vb[b_ix, h_ix, top_idx]
        s_sel = jnp.einsum('bhld,bhlrkd->bhlrk', q_i, k_sel).astype(jnp.float32) * scale
        s_sel = jnp.where(valid[:, None], s_sel, NEG).reshape(B, H, MOBA_BLOCK, n_sel * MOBA_BLOCK)
        k_own = lax.dynamic_index_in_dim(kb, i, axis=2, keepdims=False)
        v_own = lax.dynamic_index_in_dim(vb, i, axis=2, keepdims=False)
        s_own = jnp.einsum('bhld,bhkd->bhlk', q_i, k_own).astype(jnp.float32) * scale
        s_own = jnp.where(causal, s_own, NEG)
        p = jax.nn.softmax(jnp.concatenate([s_sel, s_own], axis=-1), axis=-1).astype(v.dtype)
        p_sel = p[..., : n_sel * MOBA_BLOCK].reshape(B, H, MOBA_BLOCK, n_sel, MOBA_BLOCK)
        p_own = p[..., n_sel * MOBA_BLOCK:]
        return (jnp.einsum('bhlrk,bhlrkd->bhld', p_sel, v_sel)
                + jnp.einsum('bhlk,bhkd->bhld', p_own, v_own))

    out = lax.map(one_block, (blk_ids, qb))
    return jnp.moveaxis(out, 0, 2).reshape(B, H, S, d)


def retention(q, k, v):
    B, H, S, dk = q.shape
    dv = v.shape[-1]
    C = RET_CHUNK
    nc = S // C
    log_g = jnp.log(1.0 - 2.0 ** (-5.0 - jnp.arange(H, dtype=jnp.float32)))
    k = k * (dk ** -0.5)
    qc = q.reshape(B, H, nc, C, dk)
    kc = k.reshape(B, H, nc, C, dk)
    vc = v.reshape(B, H, nc, C, dv)
    pos = jnp.arange(C, dtype=jnp.float32)
    diff = pos[:, None] - pos[None, :]
    decay = jnp.where(diff >= 0, jnp.exp(log_g[:, None, None] * jnp.maximum(diff, 0.0)), 0.0)
    scores = jnp.einsum('bhncd,bhnmd->bhncm', qc, kc) * decay[None, :, None]
    y_inner = jnp.einsum('bhncm,bhnme->bhnce', scores, vc)
    k_decay = jnp.exp(log_g[:, None] * (C - 1.0 - pos))
    q_decay = jnp.exp(log_g[:, None] * (pos + 1.0))
    chunk_decay = jnp.exp(log_g * C)
    kv = jnp.einsum('bhncd,bhnce->nbhde', kc * k_decay[None, :, None, :, None], vc)

    def step(state, kv_n):
        return state * chunk_decay[None, :, None, None] + kv_n, state

    _, prev = lax.scan(step, jnp.zeros((B, H, dk, dv), kv.dtype), kv)
    y_cross = jnp.einsum('bhncd,nbhde->bhnce', qc, prev) * q_decay[None, :, None, :, None]
    return (y_inner + y_cross).reshape(B, H, S, dv)


def token_mixer(x, w_in, w_moba_out, w_ret_out, w_out):
    B, S, _ = x.shape
    pad = (-S) % MOBA_BLOCK
    Sp = S + pad
    proj = jnp.einsum('bsd,de->bse', x, w_in)
    q_a, k_a, v_a, q_r, k_r, v_r, g_r, gate_a, gate_r = jnp.split(proj, SPLIT_POINTS, axis=-1)

    def to_heads(t, n_heads):
        t = jnp.pad(t, ((0, 0), (0, pad), (0, 0)))
        return t.reshape(B, Sp, n_heads, -1).transpose(0, 2, 1, 3)

    pos = jnp.arange(Sp, dtype=jnp.float32)
    qa = rope_half(to_heads(q_a, MOBA_HEADS), pos)
    ka = rope_half(to_heads(k_a, MOBA_HEADS), pos)
    va = to_heads(v_a, MOBA_HEADS)
    y_a = moba_attention(qa, ka, va)[:, :, :S]
    y_a = y_a.transpose(0, 2, 1, 3).reshape(B, S, MOBA_WIDTH)

    qr = rotate_retnet(to_heads(q_r, RET_HEADS), pos)
    kr = rotate_retnet(to_heads(k_r, RET_HEADS), pos)
    vr = to_heads(v_r, RET_HEADS)
    y_r = group_norm(retention(qr, kr, vr)[:, :, :S])
    y_r = y_r.transpose(0, 2, 1, 3).reshape(B, S, RET_V_WIDTH)
    y_r = jax.nn.silu(g_r) * y_r

    branch_a = jnp.einsum('bse,ed->bsd', y_a, w_moba_out)
    branch_r = jnp.einsum('bse,ed->bsd', y_r, w_ret_out)
    merged = jax.nn.sigmoid(gate_a) * branch_a + jax.nn.sigmoid(gate_r) * branch_r
    return jnp.einsum('bsd,de->bse', merged, w_out)


def peer_ffn(x, w_query, sub_keys, expert_u, expert_v):
    B, S, D = x.shape
    L = PEER_TOKEN_CHUNK
    xt = x.reshape((B * S) // L, L, D)

    def one_chunk(xc):
        q = jnp.einsum('ld,de->le', xc, w_query).reshape(L, PEER_HEADS, 2, PEER_KEY_DIM // 2)
        s = jnp.einsum('lhpd,pkd->lhpk', q, sub_keys).astype(jnp.float32)
        s_top, i_top = lax.top_k(s, PEER_TOPK)
        cand_s = (s_top[:, :, 0, :, None] + s_top[:, :, 1, None, :]).reshape(L, PEER_HEADS, PEER_TOPK * PEER_TOPK)
        cand_i = (i_top[:, :, 0, :, None] * PEER_N_KEYS + i_top[:, :, 1, None, :]).reshape(L, PEER_HEADS, PEER_TOPK * PEER_TOPK)
        fin_s, fin_pos = lax.top_k(cand_s, PEER_TOPK)
        fin_i = jnp.take_along_axis(cand_i, fin_pos, axis=-1)
        g = jax.nn.softmax(fin_s, axis=-1).astype(xc.dtype)
        u = expert_u[fin_i]
        h = jax.nn.gelu(jnp.einsum('ld,lhkd->lhk', xc, u))
        v = expert_v[fin_i]
        return jnp.einsum('lhk,lhkd->ld', g * h, v)

    return lax.map(one_chunk, xt).reshape(B, S, D)


def setup_inputs(seed: int = 0) -> dict:
    key = jax.random.key(seed)
    ks = jax.random.split(key, 14)
    f32 = jnp.float32
    offsets = np.concatenate([[0], np.cumsum(IN_SIZES)])
    col_scale = np.ones((IN_WIDTH,), np.float32)
    col_scale[offsets[2]:offsets[3]] = DN_BETA
    col_scale[offsets[5]:offsets[6]] = DN_BETA
    x = jax.random.normal(ks[0], (BATCH, SEQ, D_MODEL), f32)
    w_in = jax.random.normal(ks[1], (DEPTH, D_MODEL, IN_WIDTH), f32) * (D_MODEL ** -0.5) * jnp.asarray(col_scale)
    w_moba_out = jax.random.normal(ks[2], (DEPTH, MOBA_WIDTH, D_MODEL), f32) * (MOBA_WIDTH ** -0.5) * DN_BETA
    w_ret_out = jax.random.normal(ks[3], (DEPTH, RET_V_WIDTH, D_MODEL), f32) * (RET_V_WIDTH ** -0.5) * DN_BETA
    w_out = jax.random.normal(ks[4], (DEPTH, D_MODEL, D_MODEL), f32) * (D_MODEL ** -0.5) * DN_BETA
    ln1_g = 1.0 + 0.02 * jax.random.normal(ks[5], (DEPTH, D_MODEL), f32)
    ln1_b = 0.02 * jax.random.normal(ks[6], (DEPTH, D_MODEL), f32)
    peer_w_query = jax.random.normal(ks[7], (DEPTH, D_MODEL, PEER_HEADS * PEER_KEY_DIM), f32) * (D_MODEL ** -0.5)
    peer_sub_keys = jax.random.normal(ks[8], (DEPTH, 2, PEER_N_KEYS, PEER_KEY_DIM // 2), f32) * ((PEER_KEY_DIM // 2) ** -0.5)
    peer_u = jax.random.normal(ks[9], (DEPTH, PEER_N_EXPERTS, D_MODEL), f32) * (D_MODEL ** -0.5)
    peer_v = jax.random.normal(ks[10], (DEPTH, PEER_N_EXPERTS, D_MODEL), f32) * DN_BETA
    ln2_g = 1.0 + 0.02 * jax.random.normal(ks[11], (DEPTH, D_MODEL), f32)
    ln2_b = 0.02 * jax.random.normal(ks[12], (DEPTH, D_MODEL), f32)
    return {'x': x, 'w_in': w_in, 'w_moba_out': w_moba_out, 'w_ret_out': w_ret_out, 'w_out': w_out,
            'ln1_g': ln1_g, 'ln1_b': ln1_b, 'peer_w_query': peer_w_query, 'peer_sub_keys': peer_sub_keys,
            'peer_u': peer_u, 'peer_v': peer_v, 'ln2_g': ln2_g, 'ln2_b': ln2_b}


def reference(x, w_in, w_moba_out, w_ret_out, w_out, ln1_g, ln1_b, peer_w_query, peer_sub_keys,
              peer_u, peer_v, ln2_g, ln2_b):
    for l in range(DEPTH):
        mix = token_mixer(x, w_in[l], w_moba_out[l], w_ret_out[l], w_out[l])
        x = layer_norm(DN_ALPHA * x + mix, ln1_g[l], ln1_b[l])
        ffn = peer_ffn(x, peer_w_query[l], peer_sub_keys[l], peer_u[l], peer_v[l])
        x = layer_norm(DN_ALPHA * x + ffn, ln2_g[l], ln2_b[l])
    return x
```

```python
import functools
import math

import numpy as np
import jax
import jax.numpy as jnp
from jax import lax
from jax.experimental import pallas as pl
from jax.experimental.pallas import tpu as pltpu

F32 = jnp.float32
BF16 = jnp.bfloat16

D_MODEL = 1024
DEPTH = 2
MOBA_HEADS = 8
MOBA_HEAD_DIM = 64
MOBA_WIDTH = MOBA_HEADS * MOBA_HEAD_DIM
MOBA_BLOCK = 256
MOBA_TOPK = 3
ROPE_THETA = 10000.0
RET_HEADS = 4
RET_QK_DIM = 128
RET_V_DIM = 256
RET_QK_WIDTH = RET_HEADS * RET_QK_DIM
RET_V_WIDTH = RET_HEADS * RET_V_DIM
RET_CHUNK = 256
PEER_N_KEYS = 128
PEER_N_EXPERTS = PEER_N_KEYS * PEER_N_KEYS
PEER_HEADS = 8
PEER_KEY_DIM = 256
PEER_TOPK = 16
DN_ALPHA = (2.0 * DEPTH) ** 0.25
LN_EPS = 1e-5
NEG = -1e30

IN_SIZES = (MOBA_WIDTH, MOBA_WIDTH, MOBA_WIDTH, RET_QK_WIDTH, RET_QK_WIDTH,
            RET_V_WIDTH, RET_V_WIDTH, D_MODEL, D_MODEL)
IN_OFFSETS = tuple(int(v) for v in np.concatenate([[0], np.cumsum(IN_SIZES)]))

LANES = 128
SUBLANES = 8
VMEM_LIMIT = 56 * 1024 * 1024

PROJ_TM = 1024
PROJ_TN = 512
ROUTE_TM = 256
PEER_TM = 512
PEER_ROWS = 8

_NT = (((1,), (1,)), ((), ()))


def _params(sem):
    return pltpu.CompilerParams(dimension_semantics=sem, vmem_limit_bytes=VMEM_LIMIT)


def _rotate_groups(acc, cos, sin, o_ref, partner_fn):
    for g in range(PROJ_TN // LANES):
        xg = acc[:, g * LANES:(g + 1) * LANES]
        o_ref[:, g * LANES:(g + 1) * LANES] = (xg * cos + partner_fn(xg) * sin).astype(o_ref.dtype)


def _inproj_kernel(x_ref, w_ref, cm_ref, sm_ref, cr_ref, sr_ref, o_ref):
    j = pl.program_id(1)
    acc = jnp.dot(x_ref[...], w_ref[...], preferred_element_type=F32)
    tm = acc.shape[0]

    @pl.when(j < 2)
    def _():
        lane = lax.broadcasted_iota(jnp.int32, (tm, LANES), 1)
        first = (lane % MOBA_HEAD_DIM) < (MOBA_HEAD_DIM // 2)

        def partner(xg):
            return jnp.where(first, pltpu.roll(xg, LANES - MOBA_HEAD_DIM // 2, axis=1),
                             pltpu.roll(xg, MOBA_HEAD_DIM // 2, axis=1))

        _rotate_groups(acc, cm_ref[...], sm_ref[...], o_ref, partner)

    @pl.when((j >= 2) & (j < 4))
    def _():
        _rotate_groups(acc, cr_ref[...], sr_ref[...], o_ref,
                       lambda xg: pltpu.roll(xg, RET_QK_DIM // 2, axis=1))

    @pl.when(j >= 4)
    def _():
        o_ref[...] = acc.astype(o_ref.dtype)


def _inproj(xb, w_main, tabs, S):
    T = xb.shape[0]
    tm = min(PROJ_TM, S)
    n_col = w_main.shape[1] // PROJ_TN
    pos_blocks = S // tm
    tab_spec = pl.BlockSpec((tm, LANES), lambda i, j: (i % pos_blocks, 0))
    return pl.pallas_call(
        _inproj_kernel,
        grid=(T // tm, n_col),
        in_specs=[pl.BlockSpec((tm, D_MODEL), lambda i, j: (i, 0)),
                  pl.BlockSpec((D_MODEL, PROJ_TN), lambda i, j: (0, j)),
                  tab_spec, tab_spec, tab_spec, tab_spec],
        out_specs=pl.BlockSpec((tm, PROJ_TN), lambda i, j: (i, j)),
        out_shape=jax.ShapeDtypeStruct((T, w_main.shape[1]), BF16),
        compiler_params=_params(("parallel", "arbitrary")),
        name="inproj",
    )(xb, w_main, *tabs)


def _vt_kernel(x_ref, w_ref, o_ref):
    res = lax.dot_general(w_ref[...], x_ref[...], _NT, preferred_element_type=F32)
    for c in range(o_ref.shape[0]):
        o_ref[c] = res[:, c * MOBA_BLOCK:(c + 1) * MOBA_BLOCK].astype(o_ref.dtype)


def _moba_values_t(xb, w_vt, S):
    T = xb.shape[0]
    tm = min(PROJ_TM, S)
    per = tm // MOBA_BLOCK
    return pl.pallas_call(
        _vt_kernel,
        grid=(T // tm,),
        in_specs=[pl.BlockSpec((tm, D_MODEL), lambda i: (i, 0)),
                  pl.BlockSpec((MOBA_WIDTH, D_MODEL), lambda i: (0, 0))],
        out_specs=pl.BlockSpec((per, MOBA_WIDTH, MOBA_BLOCK), lambda i: (i, 0, 0)),
        out_shape=jax.ShapeDtypeStruct((T // MOBA_BLOCK, MOBA_WIDTH, MOBA_BLOCK), BF16),
        compiler_params=_params(("parallel",)),
        name="moba_vt",
    )(xb, w_vt)


def _moba_kernel(q_ref, k_ref, vt_ref, o_ref, kmean_ref, bias_ref, qs_ref, acc_ref, m_ref, l_ref, *, nb):
    i = pl.program_id(1)
    L = MOBA_BLOCK
    hd = MOBA_HEAD_DIM

    @pl.when(i == 0)
    def _():
        def body(j, c):
            kb = k_ref[pl.ds(pl.multiple_of(j * L, L), L), :].astype(F32)
            kmean_ref[pl.ds(j, 1), :] = jnp.sum(kb, axis=0, keepdims=True) * (1.0 / L)
            return c
        lax.fori_loop(0, nb, body, 0)

    blk = lax.broadcasted_iota(jnp.int32, (nb, L), 0)
    kpos = lax.broadcasted_iota(jnp.int32, (L, L), 0)
    qpos = lax.broadcasted_iota(jnp.int32, (L, L), 1)
    lane = lax.broadcasted_iota(jnp.int32, (L, LANES), 1)
    row0 = pl.multiple_of(i * L, L)

    for h in range(MOBA_HEADS):
        g, hh = divmod(h, LANES // hd)
        cols = slice(g * LANES, (g + 1) * LANES)
        q_pair = q_ref[:, cols]
        in_head = (lane // hd) == hh
        qm = jnp.where(in_head, q_pair, jnp.zeros_like(q_pair))
        gate = lax.dot_general(kmean_ref[:, cols].astype(BF16), qm, _NT,
                               preferred_element_type=F32)
        gm = jnp.where(blk < i, gate, -jnp.inf)
        rank = jnp.zeros((nb, L), F32)
        for jp in range(nb):
            row = gm[jp:jp + 1, :]
            rank = rank + jnp.where(blk > jp, jnp.where(row >= gm, 1.0, 0.0), jnp.where(row > gm, 1.0, 0.0))
        bias_ref[h] = jnp.where(blk < i, jnp.where(rank < float(MOBA_TOPK), 0.0, NEG), NEG)

        qs = (qm.astype(F32) * (hd ** -0.5)).astype(BF16)
        qs_ref[h] = qs
        s = lax.dot_general(k_ref[pl.ds(row0, L), cols], qs, _NT, preferred_element_type=F32)
        s = jnp.where(kpos <= qpos, s, NEG)
        m = jnp.max(s, axis=0, keepdims=True)
        p = jnp.exp(s - m)
        m_ref[h:h + 1, :] = m
        l_ref[h:h + 1, :] = jnp.sum(p, axis=0, keepdims=True)
        acc_ref[h * hd:(h + 1) * hd, :] = jnp.dot(vt_ref[i, h * hd:(h + 1) * hd, :], p.astype(BF16),
                                                  preferred_element_type=F32)

    def past(j, c):
        rj = pl.multiple_of(j * L, L)
        for h in range(MOBA_HEADS):
            g = h // (LANES // hd)
            cols = slice(g * LANES, (g + 1) * LANES)
            s = lax.dot_general(k_ref[pl.ds(rj, L), cols], qs_ref[h], _NT, preferred_element_type=F32)
            s = s + bias_ref[h, pl.ds(j, 1), :]
            m_old = m_ref[h:h + 1, :]
            m_new = jnp.maximum(m_old, jnp.max(s, axis=0, keepdims=True))
            alpha = jnp.exp(m_old - m_new)
            p = jnp.exp(s - m_new)
            m_ref[h:h + 1, :] = m_new
            l_ref[h:h + 1, :] = alpha * l_ref[h:h + 1, :] + jnp.sum(p, axis=0, keepdims=True)
            pv = jnp.dot(vt_ref[j, h * hd:(h + 1) * hd, :], p.astype(BF16), preferred_element_type=F32)
            acc_ref[h * hd:(h + 1) * hd, :] = alpha * acc_ref[h * hd:(h + 1) * hd, :] + pv
        return c

    lax.fori_loop(0, i, past, 0)

    for h in range(MOBA_HEADS):
        o_ref[0, h * hd:(h + 1) * hd, :] = acc_ref[h * hd:(h + 1) * hd, :] / l_ref[h:h + 1, :]


def _moba(proj, vt, B, S):
    nb = S // MOBA_BLOCK
    L = MOBA_BLOCK
    return pl.pallas_call(
        functools.partial(_moba_kernel, nb=nb),
        grid=(B, nb),
        in_specs=[pl.BlockSpec((L, MOBA_WIDTH), lambda b, i: (b * nb + i, 0)),
                  pl.BlockSpec((S, MOBA_WIDTH), lambda b, i: (b, 1)),
                  pl.BlockSpec((nb, MOBA_WIDTH, L), lambda b, i: (b, 0, 0))],
        out_specs=pl.BlockSpec((1, MOBA_WIDTH, L), lambda b, i: (b, 0, i)),
        out_shape=jax.ShapeDtypeStruct((B, MOBA_WIDTH, S), F32),
        scratch_shapes=[pltpu.VMEM((nb, MOBA_WIDTH), F32),
                        pltpu.VMEM((MOBA_HEADS, nb, L), F32),
                        pltpu.VMEM((MOBA_HEADS, L, LANES), BF16),
                        pltpu.VMEM((MOBA_WIDTH, L), F32),
                        pltpu.VMEM((MOBA_HEADS, L), F32),
                        pltpu.VMEM((MOBA_HEADS, L), F32)],
        compiler_params=_params(("parallel", "arbitrary")),
        name="moba",
    )(proj, proj, vt)


def _ret_log_g():
    return jnp.log(1.0 - 2.0 ** (-5.0 - jnp.arange(RET_HEADS, dtype=F32)))


def _retention_kernel(q_ref, k_ref, v_ref, g_ref, dec_ref, qd_ref, kd_ref, cd_ref, o_ref, state_ref):
    n = pl.program_id(1)

    @pl.when(n == 0)
    def _():
        state_ref[...] = jnp.zeros_like(state_ref)

    for h in range(RET_HEADS):
        qk = slice(h * RET_QK_DIM, (h + 1) * RET_QK_DIM)
        vv = slice(h * RET_V_DIM, (h + 1) * RET_V_DIM)
        q = q_ref[:, qk]
        k = k_ref[:, qk]
        v = v_ref[:, vv]
        scores = lax.dot_general(q, k, _NT, preferred_element_type=F32) * dec_ref[h]
        y = jnp.dot(scores.astype(BF16), v, preferred_element_type=F32)
        state = state_ref[h]
        y = y + jnp.dot(q, state.astype(BF16), preferred_element_type=F32) * qd_ref[h]
        kt = (k.astype(F32) * kd_ref[h]).T.astype(BF16)
        kv = jnp.dot(kt, v, preferred_element_type=F32)
        state_ref[h] = state * cd_ref[h:h + 1, :] + kv
        mu = jnp.mean(y, axis=-1, keepdims=True)
        yc = y - mu
        var = jnp.mean(yc * yc, axis=-1, keepdims=True)
        yn = yc * lax.rsqrt(var + LN_EPS)
        gate = g_ref[:, vv].astype(F32)
        o_ref[:, vv] = (gate * jax.nn.sigmoid(gate) * yn).astype(o_ref.dtype)


def _retention(proj, B, S):
    C = RET_CHUNK
    nc = S // C
    T = B * S
    log_g = _ret_log_g()
    pos = jnp.arange(C, dtype=F32)
    diff = pos[:, None] - pos[None, :]
    scale = RET_QK_DIM ** -0.5
    decay = jnp.where(diff >= 0, jnp.exp(log_g[:, None, None] * jnp.maximum(diff, 0.0)), 0.0) * scale
    q_decay = jnp.broadcast_to(jnp.exp(log_g[:, None] * (pos + 1.0))[:, :, None], (RET_HEADS, C, RET_V_DIM))
    k_decay = jnp.broadcast_to((jnp.exp(log_g[:, None] * (C - 1.0 - pos)) * scale)[:, :, None],
                               (RET_HEADS, C, RET_QK_DIM))
    chunk_decay = jnp.broadcast_to(jnp.exp(log_g * C)[:, None], (RET_HEADS, RET_V_DIM))
    const = lambda shape: pl.BlockSpec(shape, lambda b, n: (0,) * len(shape))
    return pl.pallas_call(
        _retention_kernel,
        grid=(B, nc),
        in_specs=[pl.BlockSpec((C, RET_QK_WIDTH), lambda b, n: (b * nc + n, 2)),
                  pl.BlockSpec((C, RET_QK_WIDTH), lambda b, n: (b * nc + n, 3)),
                  pl.BlockSpec((C, RET_V_WIDTH), lambda b, n: (b * nc + n, 2)),
                  pl.BlockSpec((C, RET_V_WIDTH), lambda b, n: (b * nc + n, 3)),
                  const((RET_HEADS, C, C)), const((RET_HEADS, C, RET_V_DIM)),
                  const((RET_HEADS, C, RET_QK_DIM)), const((RET_HEADS, RET_V_DIM))],
        out_specs=pl.BlockSpec((C, RET_V_WIDTH), lambda b, n: (b * nc + n, 0)),
        out_shape=jax.ShapeDtypeStruct((T, RET_V_WIDTH), BF16),
        scratch_shapes=[pltpu.VMEM((RET_HEADS, RET_QK_DIM, RET_V_DIM), F32)],
        compiler_params=_params(("parallel", "arbitrary")),
        name="retention",
    )(proj, proj, proj, proj, decay, q_decay, k_decay, chunk_decay)


def _layer_norm(y, g, b):
    mu = jnp.mean(y, axis=-1, keepdims=True)
    yc = y - mu
    var = jnp.mean(yc * yc, axis=-1, keepdims=True)
    return yc * lax.rsqrt(var + LN_EPS) * g + b


def _merge_kernel(yat_ref, yr_ref, ga_ref, gr_ref, x_ref, wa_ref, wr_ref, wo_ref, g_ref, b_ref, o_ref, ob_ref):
    ya = yat_ref[0].T.astype(BF16)
    branch_a = jnp.dot(ya, wa_ref[...], preferred_element_type=F32)
    branch_r = jnp.dot(yr_ref[...], wr_ref[...], preferred_element_type=F32)
    merged = (jax.nn.sigmoid(ga_ref[...].astype(F32)) * branch_a
              + jax.nn.sigmoid(gr_ref[...].astype(F32)) * branch_r)
    mix = jnp.dot(merged.astype(BF16), wo_ref[...], preferred_element_type=F32)
    y = _layer_norm(DN_ALPHA * x_ref[...] + mix, g_ref[...], b_ref[...])
    o_ref[...] = y
    ob_ref[...] = y.astype(BF16)


def _merge(yat, yr, proj, x, wa, wr, wo, g, b, B, S):
    L = MOBA_BLOCK
    nb = S // L
    T = B * S
    tok = lambda c: pl.BlockSpec((L, D_MODEL), lambda bb, i: (bb * nb + i, c))
    const = lambda shape: pl.BlockSpec(shape, lambda bb, i: (0,) * len(shape))
    return pl.pallas_call(
        _merge_kernel,
        grid=(B, nb),
        in_specs=[pl.BlockSpec((1, MOBA_WIDTH, L), lambda bb, i: (bb, 0, i)),
                  tok(0), tok(4), tok(5), tok(0),
                  const((MOBA_WIDTH, D_MODEL)), const((RET_V_WIDTH, D_MODEL)), const((D_MODEL, D_MODEL)),
                  const((1, D_MODEL)), const((1, D_MODEL))],
        out_specs=[tok(0), tok(0)],
        out_shape=[jax.ShapeDtypeStruct((T, D_MODEL), F32), jax.ShapeDtypeStruct((T, D_MODEL), BF16)],
        compiler_params=_params(("parallel", "parallel")),
        name="merge_ln1",
    )(yat, yr, proj, proj, x, wa, wr, wo, g, b)


def _cmpx(xs, a, b):
    hi = jnp.maximum(xs[a], xs[b])
    lo = jnp.minimum(xs[a], xs[b])
    xs[a], xs[b] = hi, lo


def _bitonic_merge_desc(xs):
    n = len(xs)
    j = n // 2
    while j >= 1:
        for a in range(n):
            b = a ^ j
            if b > a:
                _cmpx(xs, a, b)
        j //= 2
    return xs


def _sort_desc(xs):
    xs = list(xs)
    n = len(xs)
    k = 2
    while k <= n:
        j = k // 2
        while j >= 1:
            for a in range(n):
                b = a ^ j
                if b > a:
                    if (a & k) == 0:
                        _cmpx(xs, a, b)
                    else:
                        _cmpx(xs, b, a)
            j //= 2
        k *= 2
    return xs


def _merge_top(xs, ys):
    n = len(xs)
    return _bitonic_merge_desc([jnp.maximum(xs[v], ys[n - 1 - v]) for v in range(n)])


def _across_sublanes(xs):
    for shift in (4, 2, 1):
        xs = _merge_top(xs, [pltpu.roll(x, shift, axis=0) for x in xs])
    return xs


def _top16_desc(s):
    rows = [s[SUBLANES * v:SUBLANES * (v + 1), :] for v in range(s.shape[0] // SUBLANES)]
    return _across_sublanes(_sort_desc(rows))


def _min_over_sublanes(x):
    for shift in (4, 2, 1):
        x = jnp.minimum(x, pltpu.roll(x, shift, axis=0))
    return x


def _route_kernel(x_ref, wq_ref, sk_ref, s1_ref, e1_ref, th_ref, c0_ref):
    q = jnp.dot(x_ref[...], wq_ref[...], preferred_element_type=F32)
    tm = q.shape[0]
    K = PEER_TOPK
    half = PEER_KEY_DIM // 2
    sub = lax.broadcasted_iota(jnp.int32, (SUBLANES, tm), 0)
    inf = jnp.inf
    for h in range(PEER_HEADS):
        q0 = q[:, (2 * h) * half:(2 * h + 1) * half].astype(BF16)
        q1 = q[:, (2 * h + 1) * half:(2 * h + 2) * half].astype(BF16)
        s0 = lax.dot_general(sk_ref[0], q0, _NT, preferred_element_type=F32)
        s1 = lax.dot_general(sk_ref[1], q1, _NT, preferred_element_type=F32)
        a = _top16_desc(s0)
        b = _top16_desc(s1)
        a_lo, a_hi = a[0], a[SUBLANES]
        for r in range(1, SUBLANES):
            a_lo = jnp.where(sub == r, a[r], a_lo)
            a_hi = jnp.where(sub == r, a[SUBLANES + r], a_hi)
        x_lo = [a_lo + b[c] for c in range(K)]
        x_hi = [a_hi + b[c] for c in range(K)]
        z = _across_sublanes(_merge_top(x_lo, x_hi))
        tau = z[K - 1]
        zsum = jnp.exp(z[0] - z[0])
        for v in range(1, K):
            zsum = zsum + jnp.exp(z[v] - z[0])
        inv_z = (1.0 / zsum)[0:1, :]
        theta = jnp.full(s0.shape, inf, F32)
        for c in range(K):
            alpha = jnp.minimum(jnp.where(x_lo[c] >= tau, a_lo, inf), jnp.where(x_hi[c] >= tau, a_hi, inf))
            alpha = _min_over_sublanes(alpha)[0:1, :]
            theta = jnp.where(s0 >= alpha, b[c][0:1, :], theta)
        s1_ref[h] = s1
        e1_ref[h] = jnp.exp(s1 - b[0][0:1, :])
        th_ref[h] = theta
        c0_ref[h] = jnp.exp(s0 - a[0][0:1, :]) * inv_z


def _route(xb, wq, sk):
    T = xb.shape[0]
    tm = ROUTE_TM
    shape = jax.ShapeDtypeStruct((PEER_HEADS, PEER_N_KEYS, T), F32)
    spec = pl.BlockSpec((PEER_HEADS, PEER_N_KEYS, tm), lambda t: (0, 0, t))
    return pl.pallas_call(
        _route_kernel,
        grid=(T // tm,),
        in_specs=[pl.BlockSpec((tm, D_MODEL), lambda t: (t, 0)),
                  pl.BlockSpec(wq.shape, lambda t: (0, 0)),
                  pl.BlockSpec(sk.shape, lambda t: (0, 0, 0))],
        out_specs=[spec, spec, spec, spec],
        out_shape=[shape, shape, shape, shape],
        compiler_params=_params(("parallel",)),
        name="peer_route",
    )(xb, wq, sk)


def _gelu_tanh(x):
    return 0.5 * x * (1.0 + jnp.tanh(math.sqrt(2.0 / math.pi) * (x + 0.044715 * (x * x * x))))


def _peer_kernel(xb_ref, u_ref, vt_ref, s1_ref, e1_ref, th_ref, c0_ref, x_ref, g_ref, b_ref,
                 o_ref, ob_ref, act_ref, acc_ref, *, rows):
    e = pl.program_id(1)
    nk = PEER_N_KEYS

    @pl.when(e == 0)
    def _():
        acc_ref[...] = jnp.zeros_like(acc_ref)

    def row(r, c):
        i = e * rows + r
        r0 = pl.multiple_of(r * nk, nk)
        ht = lax.dot_general(u_ref[pl.ds(r0, nk), :], xb_ref[...], _NT, preferred_element_type=F32)
        w = jnp.zeros(ht.shape, F32)
        for h in range(PEER_HEADS):
            sel = jnp.where(s1_ref[h] >= th_ref[h, pl.ds(i, 1), :], e1_ref[h], 0.0)
            w = w + sel * c0_ref[h, pl.ds(i, 1), :]
        act_ref[pl.ds(r0, nk), :] = (w * _gelu_tanh(ht)).astype(BF16)
        return c

    lax.fori_loop(0, rows, row, 0)
    acc_ref[...] += jnp.dot(vt_ref[...], act_ref[...], preferred_element_type=F32)

    @pl.when(e == pl.num_programs(1) - 1)
    def _():
        y = _layer_norm(DN_ALPHA * x_ref[...] + acc_ref[...].T, g_ref[...], b_ref[...])
        o_ref[...] = y
        ob_ref[...] = y.astype(BF16)


def _peer(xb, x, u, vt, routing, g, b):
    T = xb.shape[0]
    tm = min(PEER_TM, T)
    rows = PEER_ROWS
    nk = PEER_N_KEYS
    rspec = pl.BlockSpec((PEER_HEADS, nk, tm), lambda t, e: (0, 0, t))
    tok = pl.BlockSpec((tm, D_MODEL), lambda t, e: (t, 0))
    const = pl.BlockSpec((1, D_MODEL), lambda t, e: (0, 0))
    return pl.pallas_call(
        functools.partial(_peer_kernel, rows=rows),
        grid=(T // tm, nk // rows),
        in_specs=[tok,
                  pl.BlockSpec((rows * nk, D_MODEL), lambda t, e: (e, 0)),
                  pl.BlockSpec((D_MODEL, rows * nk), lambda t, e: (0, e)),
                  rspec, rspec, rspec, rspec, tok, const, const],
        out_specs=[tok, tok],
        out_shape=[jax.ShapeDtypeStruct((T, D_MODEL), F32), jax.ShapeDtypeStruct((T, D_MODEL), BF16)],
        scratch_shapes=[pltpu.VMEM((rows * nk, tm), BF16),
                        pltpu.VMEM((D_MODEL, tm), F32)],
        compiler_params=_params(("parallel", "arbitrary")),
        name="peer_dense",
    )(xb, u, vt, *routing, x, g, b)


def _rope_tables(S):
    pos = jnp.arange(S, dtype=F32)
    d = MOBA_HEAD_DIM
    inv = ROPE_THETA ** (-jnp.arange(0, d, 2, dtype=F32) / d)
    ang = pos[:, None] * inv[None, :]
    cos, sin = jnp.cos(ang), jnp.sin(ang)
    reps = LANES // d
    cos_m = jnp.tile(jnp.concatenate([cos, cos], axis=1), (1, reps))
    sin_m = jnp.tile(jnp.concatenate([-sin, sin], axis=1), (1, reps))
    d = RET_QK_DIM
    inv = 1.0 / (ROPE_THETA ** jnp.linspace(0.0, 1.0, d // 2, dtype=F32))
    ang = pos[:, None] * inv[None, :]
    cos, sin = jnp.cos(ang), jnp.sin(ang)
    cos_r = jnp.concatenate([cos, cos], axis=1)
    sin_r = jnp.concatenate([-sin, sin], axis=1)
    return cos_m, sin_m, cos_r, sin_r


def _ret_column_perm():
    within = np.concatenate([np.arange(0, RET_QK_DIM, 2), np.arange(1, RET_QK_DIM, 2)])
    return np.concatenate([h * RET_QK_DIM + within for h in range(RET_HEADS)])


def kernel(x, w_in, w_moba_out, w_ret_out, w_out, ln1_g, ln1_b, peer_w_query, peer_sub_keys,
           peer_u, peer_v, ln2_g, ln2_b):
    B, S, D = x.shape
    assert D == D_MODEL and S % MOBA_BLOCK == 0 and S % RET_CHUNK == 0
    T = B * S
    tabs = _rope_tables(S)
    perm = _ret_column_perm()
    o = IN_OFFSETS
    xf = x.reshape(T, D).astype(F32)
    xb = xf.astype(BF16)
    for l in range(DEPTH):
        w = w_in[l]
        w_main = jnp.concatenate(
            [w[:, o[0]:o[2]], w[:, o[3]:o[4]][:, perm], w[:, o[4]:o[5]][:, perm], w[:, o[5]:o[9]]],
            axis=1).astype(BF16)
        w_vt = w[:, o[2]:o[3]].T.astype(BF16)
        proj = _inproj(xb, w_main, tabs, S)
        vt = _moba_values_t(xb, w_vt, S)
        yat = _moba(proj, vt, B, S)
        yr = _retention(proj, B, S)
        xf, xb = _merge(yat, yr, proj, xf, w_moba_out[l].astype(BF16), w_ret_out[l].astype(BF16),
                        w_out[l].astype(BF16), ln1_g[l].reshape(1, D), ln1_b[l].reshape(1, D), B, S)
        routing = _route(xb, peer_w_query[l].astype(BF16), peer_sub_keys[l].astype(BF16))
        xf, xb = _peer(xb, xf, peer_u[l].astype(BF16), peer_v[l].T.astype(BF16), routing,
                       ln2_g[l].reshape(1, D), ln2_b[l].reshape(1, D))
    return xf.reshape(B, S, D).astype(x.dtype)
```

```python
import functools
import math

import numpy as np
import jax
import jax.numpy as jnp
from jax import lax
from jax.experimental import pallas as pl
from jax.experimental.pallas import tpu as pltpu

F32 = jnp.float32
BF16 = jnp.bfloat16

D_MODEL = 1024
DEPTH = 2
MOBA_HEADS = 8
MOBA_HEAD_DIM = 64
MOBA_WIDTH = MOBA_HEADS * MOBA_HEAD_DIM
MOBA_BLOCK = 256
MOBA_TOPK = 3
ROPE_THETA = 10000.0
RET_HEADS = 4
RET_QK_DIM = 128
RET_V_DIM = 256
RET_QK_WIDTH = RET_HEADS * RET_QK_DIM
RET_V_WIDTH = RET_HEADS * RET_V_DIM
RET_CHUNK = 256
PEER_N_KEYS = 128
PEER_N_EXPERTS = PEER_N_KEYS * PEER_N_KEYS
PEER_HEADS = 8
PEER_KEY_DIM = 256
PEER_TOPK = 16
DN_ALPHA = (2.0 * DEPTH) ** 0.25
LN_EPS = 1e-5
NEG = -1e30

IN_SIZES = (MOBA_WIDTH, MOBA_WIDTH, MOBA_WIDTH, RET_QK_WIDTH, RET_QK_WIDTH,
            RET_V_WIDTH, RET_V_WIDTH, D_MODEL, D_MODEL)
IN_OFFSETS = tuple(int(v) for v in np.concatenate([[0], np.cumsum(IN_SIZES)]))

LANES = 128
SUBLANES = 8
VMEM_LIMIT = 56 * 1024 * 1024

PROJ_TM = 1024
PROJ_TN = 512
ROUTE_TM = 256
PEER_TM = 512
PEER_ROWS = 8
PEER_LANE_CHUNK = 256

_NT = (((1,), (1,)), ((), ()))


def _params(sem):
    return pltpu.CompilerParams(dimension_semantics=sem, vmem_limit_bytes=VMEM_LIMIT)


def _rotate_groups(acc, cos, sin, o_ref, partner_fn):
    for g in range(PROJ_TN // LANES):
        xg = acc[:, g * LANES:(g + 1) * LANES]
        o_ref[:, g * LANES:(g + 1) * LANES] = (xg * cos + partner_fn(xg) * sin).astype(o_ref.dtype)


def _inproj_kernel(x_ref, w_ref, cm_ref, sm_ref, cr_ref, sr_ref, o_ref):
    j = pl.program_id(1)
    acc = jnp.dot(x_ref[...], w_ref[...], preferred_element_type=F32)
    tm = acc.shape[0]

    @pl.when(j < 2)
    def _():
        lane = lax.broadcasted_iota(jnp.int32, (tm, LANES), 1)
        first = (lane % MOBA_HEAD_DIM) < (MOBA_HEAD_DIM // 2)

        def partner(xg):
            return jnp.where(first, pltpu.roll(xg, LANES - MOBA_HEAD_DIM // 2, axis=1),
                             pltpu.roll(xg, MOBA_HEAD_DIM // 2, axis=1))

        _rotate_groups(acc, cm_ref[...], sm_ref[...], o_ref, partner)

    @pl.when((j >= 2) & (j < 4))
    def _():
        _rotate_groups(acc, cr_ref[...], sr_ref[...], o_ref,
                       lambda xg: pltpu.roll(xg, RET_QK_DIM // 2, axis=1))

    @pl.when(j >= 4)
    def _():
        o_ref[...] = acc.astype(o_ref.dtype)


def _inproj(xb, w_main, tabs, S):
    T = xb.shape[0]
    tm = min(PROJ_TM, S)
    n_col = w_main.shape[1] // PROJ_TN
    pos_blocks = S // tm
    tab_spec = pl.BlockSpec((tm, LANES), lambda i, j: (i % pos_blocks, 0))
    return pl.pallas_call(
        _inproj_kernel,
        grid=(T // tm, n_col),
        in_specs=[pl.BlockSpec((tm, D_MODEL), lambda i, j: (i, 0)),
                  pl.BlockSpec((D_MODEL, PROJ_TN), lambda i, j: (0, j)),
                  tab_spec, tab_spec, tab_spec, tab_spec],
        out_specs=pl.BlockSpec((tm, PROJ_TN), lambda i, j: (i, j)),
        out_shape=jax.ShapeDtypeStruct((T, w_main.shape[1]), BF16),
        compiler_params=_params(("parallel", "arbitrary")),
        name="inproj",
    )(xb, w_main, *tabs)


def _vt_kernel(x_ref, w_ref, o_ref):
    res = lax.dot_general(w_ref[...], x_ref[...], _NT, preferred_element_type=F32)
    for c in range(o_ref.shape[0]):
        o_ref[c] = res[:, c * MOBA_BLOCK:(c + 1) * MOBA_BLOCK].astype(o_ref.dtype)


def _moba_values_t(xb, w_vt, S):
    T = xb.shape[0]
    tm = min(PROJ_TM, S)
    per = tm // MOBA_BLOCK
    return pl.pallas_call(
        _vt_kernel,
        grid=(T // tm,),
        in_specs=[pl.BlockSpec((tm, D_MODEL), lambda i: (i, 0)),
                  pl.BlockSpec((MOBA_WIDTH, D_MODEL), lambda i: (0, 0))],
        out_specs=pl.BlockSpec((per, MOBA_WIDTH, MOBA_BLOCK), lambda i: (i, 0, 0)),
        out_shape=jax.ShapeDtypeStruct((T // MOBA_BLOCK, MOBA_WIDTH, MOBA_BLOCK), BF16),
        compiler_params=_params(("parallel",)),
        name="moba_vt",
    )(xb, w_vt)


def _moba_kernel(q_ref, k_ref, vt_ref, o_ref, kmean_ref, bias_ref, qs_ref, acc_ref, m_ref, l_ref, *, nb):
    i = pl.program_id(1)
    L = MOBA_BLOCK
    hd = MOBA_HEAD_DIM

    @pl.when(i == 0)
    def _():
        def body(j, c):
            kb = k_ref[pl.ds(pl.multiple_of(j * L, L), L), :].astype(F32)
            kmean_ref[pl.ds(j, 1), :] = jnp.sum(kb, axis=0, keepdims=True) * (1.0 / L)
            return c
        lax.fori_loop(0, nb, body, 0)

    per = LANES // hd
    groups = MOBA_HEADS // per
    W = MOBA_HEADS * L
    lane = lax.broadcasted_iota(jnp.int32, (L, LANES), 1)
    row0 = pl.multiple_of(i * L, L)

    gates = []
    for g in range(groups):
        cols = slice(g * LANES, (g + 1) * LANES)
        q_pair = q_ref[:, cols]
        km = kmean_ref[:, cols].astype(BF16)
        for hh in range(per):
            qm = jnp.where((lane // hd) == hh, q_pair, jnp.zeros_like(q_pair))
            gates.append(lax.dot_general(km, qm, _NT, preferred_element_type=F32))
            qs_ref[g, hh * L:(hh + 1) * L, :] = (qm.astype(F32) * (hd ** -0.5)).astype(BF16)

    blk = lax.broadcasted_iota(jnp.int32, (nb, W), 0)
    blkf = blk.astype(F32)
    gm = jnp.where(blk < i, jnp.concatenate(gates, axis=1), -jnp.inf)
    keep = jnp.zeros((nb, W), F32)
    for _ in range(MOBA_TOPK):
        best = jnp.max(gm, axis=0, keepdims=True)
        first = jnp.min(jnp.where(gm == best, blkf, float(nb)), axis=0, keepdims=True)
        hit = blkf == first
        keep = jnp.where(hit, 1.0, keep)
        gm = jnp.where(hit, -jnp.inf, gm)
    bias_ref[...] = jnp.where(blk < i, jnp.where(keep > 0.0, 0.0, NEG), NEG)

    def scores(r):
        return jnp.concatenate(
            [lax.dot_general(k_ref[pl.ds(r, L), g * LANES:(g + 1) * LANES], qs_ref[g], _NT,
                             preferred_element_type=F32) for g in range(groups)], axis=1)

    def values(j, p):
        pb = p.astype(BF16)
        return [jnp.dot(vt_ref[j, h * hd:(h + 1) * hd, :], pb[:, h * L:(h + 1) * L],
                        preferred_element_type=F32) for h in range(MOBA_HEADS)]

    kpos = lax.broadcasted_iota(jnp.int32, (L, W), 0)
    qpos = lax.broadcasted_iota(jnp.int32, (L, W), 1) % L
    s = jnp.where(kpos <= qpos, scores(row0), NEG)
    m = jnp.max(s, axis=0, keepdims=True)
    p = jnp.exp(s - m)
    m_ref[...] = jnp.broadcast_to(m, (SUBLANES, W))
    l_ref[...] = jnp.broadcast_to(jnp.sum(p, axis=0, keepdims=True), (SUBLANES, W))
    for h, pv in enumerate(values(i, p)):
        acc_ref[h * hd:(h + 1) * hd, :] = pv

    def past(j, c):
        s = scores(pl.multiple_of(j * L, L)) + bias_ref[pl.ds(j, 1), :]
        m_old = m_ref[...]
        m_new = jnp.maximum(m_old, jnp.max(s, axis=0, keepdims=True))
        alpha = jnp.exp(m_old - m_new)
        p = jnp.exp(s - m_new[0:1, :])
        m_ref[...] = m_new
        l_ref[...] = alpha * l_ref[...] + jnp.sum(p, axis=0, keepdims=True)
        for h, pv in enumerate(values(j, p)):
            rows = slice(h * hd, (h + 1) * hd)
            acc_ref[rows, :] = alpha[0:1, h * L:(h + 1) * L] * acc_ref[rows, :] + pv
        return c

    lax.fori_loop(0, i, past, 0)

    for h in range(MOBA_HEADS):
        rows = slice(h * hd, (h + 1) * hd)
        o_ref[0, rows, :] = acc_ref[rows, :] / l_ref[0:1, h * L:(h + 1) * L]


def _moba(proj, vt, B, S):
    nb = S // MOBA_BLOCK
    L = MOBA_BLOCK
    return pl.pallas_call(
        functools.partial(_moba_kernel, nb=nb),
        grid=(B, nb),
        in_specs=[pl.BlockSpec((L, MOBA_WIDTH), lambda b, i: (b * nb + i, 0)),
                  pl.BlockSpec((S, MOBA_WIDTH), lambda b, i: (b, 1)),
                  pl.BlockSpec((nb, MOBA_WIDTH, L), lambda b, i: (b, 0, 0))],
        out_specs=pl.BlockSpec((1, MOBA_WIDTH, L), lambda b, i: (b, 0, i)),
        out_shape=jax.ShapeDtypeStruct((B, MOBA_WIDTH, S), F32),
        scratch_shapes=[pltpu.VMEM((nb, MOBA_WIDTH), F32),
                        pltpu.VMEM((nb, MOBA_HEADS * L), F32),
                        pltpu.VMEM((MOBA_HEADS // (LANES // MOBA_HEAD_DIM), (LANES // MOBA_HEAD_DIM) * L, LANES),
                                   BF16),
                        pltpu.VMEM((MOBA_WIDTH, L), F32),
                        pltpu.VMEM((SUBLANES, MOBA_HEADS * L), F32),
                        pltpu.VMEM((SUBLANES, MOBA_HEADS * L), F32)],
        compiler_params=_params(("parallel", "arbitrary")),
        name="moba",
    )(proj, proj, vt)


def _ret_log_g():
    return jnp.log(1.0 - 2.0 ** (-5.0 - jnp.arange(RET_HEADS, dtype=F32)))


def _retention_kernel(q_ref, k_ref, v_ref, g_ref, dec_ref, qd_ref, kd_ref, cd_ref, o_ref, state_ref):
    n = pl.program_id(1)

    @pl.when(n == 0)
    def _():
        state_ref[...] = jnp.zeros_like(state_ref)

    for h in range(RET_HEADS):
        qk = slice(h * RET_QK_DIM, (h + 1) * RET_QK_DIM)
        vv = slice(h * RET_V_DIM, (h + 1) * RET_V_DIM)
        q = q_ref[:, qk]
        k = k_ref[:, qk]
        v = v_ref[:, vv]
        scores = lax.dot_general(q, k, _NT, preferred_element_type=F32) * dec_ref[h]
        y = jnp.dot(scores.astype(BF16), v, preferred_element_type=F32)
        state = state_ref[h]
        y = y + jnp.dot(q, state.astype(BF16), preferred_element_type=F32) * qd_ref[h]
        kt = (k.astype(F32) * kd_ref[h]).T.astype(BF16)
        kv = jnp.dot(kt, v, preferred_element_type=F32)
        state_ref[h] = state * cd_ref[h:h + 1, :] + kv
        mu = jnp.mean(y, axis=-1, keepdims=True)
        yc = y - mu
        var = jnp.mean(yc * yc, axis=-1, keepdims=True)
        yn = yc * lax.rsqrt(var + LN_EPS)
        gate = g_ref[:, vv].astype(F32)
        o_ref[:, vv] = (gate * jax.nn.sigmoid(gate) * yn).astype(o_ref.dtype)


def _retention(proj, B, S):
    C = RET_CHUNK
    nc = S // C
    T = B * S
    log_g = _ret_log_g()
    pos = jnp.arange(C, dtype=F32)
    diff = pos[:, None] - pos[None, :]
    scale = RET_QK_DIM ** -0.5
    decay = jnp.where(diff >= 0, jnp.exp(log_g[:, None, None] * jnp.maximum(diff, 0.0)), 0.0) * scale
    q_decay = jnp.broadcast_to(jnp.exp(log_g[:, None] * (pos + 1.0))[:, :, None], (RET_HEADS, C, RET_V_DIM))
    k_decay = jnp.broadcast_to((jnp.exp(log_g[:, None] * (C - 1.0 - pos)) * scale)[:, :, None],
                               (RET_HEADS, C, RET_QK_DIM))
    chunk_decay = jnp.broadcast_to(jnp.exp(log_g * C)[:, None], (RET_HEADS, RET_V_DIM))
    const = lambda shape: pl.BlockSpec(shape, lambda b, n: (0,) * len(shape))
    return pl.pallas_call(
        _retention_kernel,
        grid=(B, nc),
        in_specs=[pl.BlockSpec((C, RET_QK_WIDTH), lambda b, n: (b * nc + n, 2)),
                  pl.BlockSpec((C, RET_QK_WIDTH), lambda b, n: (b * nc + n, 3)),
                  pl.BlockSpec((C, RET_V_WIDTH), lambda b, n: (b * nc + n, 2)),
                  pl.BlockSpec((C, RET_V_WIDTH), lambda b, n: (b * nc + n, 3)),
                  const((RET_HEADS, C, C)), const((RET_HEADS, C, RET_V_DIM)),
                  const((RET_HEADS, C, RET_QK_DIM)), const((RET_HEADS, RET_V_DIM))],
        out_specs=pl.BlockSpec((C, RET_V_WIDTH), lambda b, n: (b * nc + n, 0)),
        out_shape=jax.ShapeDtypeStruct((T, RET_V_WIDTH), BF16),
        scratch_shapes=[pltpu.VMEM((RET_HEADS, RET_QK_DIM, RET_V_DIM), F32)],
        compiler_params=_params(("parallel", "arbitrary")),
        name="retention",
    )(proj, proj, proj, proj, decay, q_decay, k_decay, chunk_decay)


def _layer_norm(y, g, b):
    mu = jnp.mean(y, axis=-1, keepdims=True)
    yc = y - mu
    var = jnp.mean(yc * yc, axis=-1, keepdims=True)
    return yc * lax.rsqrt(var + LN_EPS) * g + b


def _merge_kernel(yat_ref, yr_ref, ga_ref, gr_ref, x_ref, wa_ref, wr_ref, wo_ref, g_ref, b_ref, o_ref, ob_ref):
    ya = yat_ref[0].T.astype(BF16)
    branch_a = jnp.dot(ya, wa_ref[...], preferred_element_type=F32)
    branch_r = jnp.dot(yr_ref[...], wr_ref[...], preferred_element_type=F32)
    merged = (jax.nn.sigmoid(ga_ref[...].astype(F32)) * branch_a
              + jax.nn.sigmoid(gr_ref[...].astype(F32)) * branch_r)
    mix = jnp.dot(merged.astype(BF16), wo_ref[...], preferred_element_type=F32)
    y = _layer_norm(DN_ALPHA * x_ref[...] + mix, g_ref[...], b_ref[...])
    o_ref[...] = y
    ob_ref[...] = y.astype(BF16)


def _merge(yat, yr, proj, x, wa, wr, wo, g, b, B, S):
    L = MOBA_BLOCK
    nb = S // L
    T = B * S
    tok = lambda c: pl.BlockSpec((L, D_MODEL), lambda bb, i: (bb * nb + i, c))
    const = lambda shape: pl.BlockSpec(shape, lambda bb, i: (0,) * len(shape))
    return pl.pallas_call(
        _merge_kernel,
        grid=(B, nb),
        in_specs=[pl.BlockSpec((1, MOBA_WIDTH, L), lambda bb, i: (bb, 0, i)),
                  tok(0), tok(4), tok(5), tok(0),
                  const((MOBA_WIDTH, D_MODEL)), const((RET_V_WIDTH, D_MODEL)), const((D_MODEL, D_MODEL)),
                  const((1, D_MODEL)), const((1, D_MODEL))],
        out_specs=[tok(0), tok(0)],
        out_shape=[jax.ShapeDtypeStruct((T, D_MODEL), F32), jax.ShapeDtypeStruct((T, D_MODEL), BF16)],
        compiler_params=_params(("parallel", "parallel")),
        name="merge_ln1",
    )(yat, yr, proj, proj, x, wa, wr, wo, g, b)


def _cmpx(xs, a, b):
    hi = jnp.maximum(xs[a], xs[b])
    lo = jnp.minimum(xs[a], xs[b])
    xs[a], xs[b] = hi, lo


def _bitonic_merge_desc(xs):
    n = len(xs)
    j = n // 2
    while j >= 1:
        for a in range(n):
            b = a ^ j
            if b > a:
                _cmpx(xs, a, b)
        j //= 2
    return xs


def _sort_desc(xs):
    xs = list(xs)
    n = len(xs)
    k = 2
    while k <= n:
        j = k // 2
        while j >= 1:
            for a in range(n):
                b = a ^ j
                if b > a:
                    if (a & k) == 0:
                        _cmpx(xs, a, b)
                    else:
                        _cmpx(xs, b, a)
            j //= 2
        k *= 2
    return xs


def _merge_top(xs, ys):
    n = len(xs)
    return _bitonic_merge_desc([jnp.maximum(xs[v], ys[n - 1 - v]) for v in range(n)])


def _across_sublanes(xs):
    for shift in (4, 2, 1):
        xs = _merge_top(xs, [pltpu.roll(x, shift, axis=0) for x in xs])
    return xs


def _top16_desc(s):
    rows = [s[SUBLANES * v:SUBLANES * (v + 1), :] for v in range(s.shape[0] // SUBLANES)]
    return _across_sublanes(_sort_desc(rows))


def _min_over_sublanes(x):
    for shift in (4, 2, 1):
        x = jnp.minimum(x, pltpu.roll(x, shift, axis=0))
    return x


def _route_kernel(x_ref, wq_ref, sk_ref, r1_ref, e1_ref, n0_ref, c0_ref):
    q = jnp.dot(x_ref[...], wq_ref[...], preferred_element_type=F32)
    tm = q.shape[0]
    K = PEER_TOPK
    half = PEER_KEY_DIM // 2
    sub = lax.broadcasted_iota(jnp.int32, (SUBLANES, tm), 0)
    inf = jnp.inf
    for h in range(PEER_HEADS):
        q0 = q[:, (2 * h) * half:(2 * h + 1) * half].astype(BF16)
        q1 = q[:, (2 * h + 1) * half:(2 * h + 2) * half].astype(BF16)
        s0 = lax.dot_general(sk_ref[0], q0, _NT, preferred_element_type=F32)
        s1 = lax.dot_general(sk_ref[1], q1, _NT, preferred_element_type=F32)
        a = _top16_desc(s0)
        b = _top16_desc(s1)
        a_lo, a_hi = a[0], a[SUBLANES]
        for r in range(1, SUBLANES):
            a_lo = jnp.where(sub == r, a[r], a_lo)
            a_hi = jnp.where(sub == r, a[SUBLANES + r], a_hi)
        x_lo = [a_lo + b[c] for c in range(K)]
        x_hi = [a_hi + b[c] for c in range(K)]
        z = _across_sublanes(_merge_top(x_lo, x_hi))
        tau = z[K - 1]
        zsum = jnp.exp(z[0] - z[0])
        for v in range(1, K):
            zsum = zsum + jnp.exp(z[v] - z[0])
        inv_z = (1.0 / zsum)[0:1, :]
        count = jnp.zeros(s0.shape, F32)
        for c in range(K):
            alpha = jnp.minimum(jnp.where(x_lo[c] >= tau, a_lo, inf), jnp.where(x_hi[c] >= tau, a_hi, inf))
            alpha = _min_over_sublanes(alpha)[0:1, :]
            count = jnp.where(s0 >= alpha, float(c + 1), count)
        rank = jnp.full(s1.shape, float(K), F32)
        for c in reversed(range(K)):
            rank = jnp.where(s1 >= b[c][0:1, :], float(c), rank)
        r1_ref[h] = rank.astype(BF16)
        e1_ref[h] = jnp.exp(s1 - b[0][0:1, :]).astype(BF16)
        n0_ref[h] = count
        c0_ref[h] = jnp.exp(s0 - a[0][0:1, :]) * inv_z


def _route(xb, wq, sk):
    T = xb.shape[0]
    tm = ROUTE_TM
    spec = pl.BlockSpec((PEER_HEADS, PEER_N_KEYS, tm), lambda t: (0, 0, t))
    half = jax.ShapeDtypeStruct((PEER_HEADS, PEER_N_KEYS, T), BF16)
    word = jax.ShapeDtypeStruct((PEER_HEADS, PEER_N_KEYS, T), F32)
    return pl.pallas_call(
        _route_kernel,
        grid=(T // tm,),
        in_specs=[pl.BlockSpec((tm, D_MODEL), lambda t: (t, 0)),
                  pl.BlockSpec(wq.shape, lambda t: (0, 0)),
                  pl.BlockSpec(sk.shape, lambda t: (0, 0, 0))],
        out_specs=[spec, spec, spec, spec],
        out_shape=[half, half, word, word],
        compiler_params=_params(("parallel",)),
        name="peer_route",
    )(xb, wq, sk)


def _gelu_tanh(x):
    return 0.5 * x * (1.0 + jnp.tanh(math.sqrt(2.0 / math.pi) * (x + 0.044715 * (x * x * x))))


def _peer_kernel(xb_ref, u_ref, vt_ref, r1_ref, e1_ref, n0_ref, c0_ref, x_ref, g_ref, b_ref,
                 o_ref, ob_ref, w_ref, act_ref, acc_ref, *, rows):
    e = pl.program_id(1)
    nk = PEER_N_KEYS
    tm = xb_ref.shape[0]
    pk = 2 * SUBLANES

    @pl.when(e == 0)
    def _():
        acc_ref[...] = jnp.zeros_like(acc_ref)

    tc = min(PEER_LANE_CHUNK, tm)

    for r in range(rows):
        i = e * rows + r
        for ch in range(tm // tc):
            ls = slice(ch * tc, (ch + 1) * tc)
            w = [None] * (nk // pk)
            for h in range(PEER_HEADS):
                count = jnp.broadcast_to(n0_ref[h, pl.ds(i, 1), ls], (pk, tc)).astype(BF16)
                weight = jnp.broadcast_to(c0_ref[h, pl.ds(i, 1), ls], (pk, tc)).astype(BF16)
                for g in range(nk // pk):
                    rs = slice(g * pk, (g + 1) * pk)
                    sel = jnp.where(r1_ref[h, rs, ls] < count, e1_ref[h, rs, ls], jnp.zeros((pk, tc), BF16))
                    term = sel * weight
                    w[g] = term if w[g] is None else w[g] + term
            for g in range(nk // pk):
                w_ref[r * nk + g * pk:r * nk + (g + 1) * pk, ls] = w[g]

    ht = lax.dot_general(u_ref[...], xb_ref[...], _NT, preferred_element_type=F32)
    act_ref[...] = _gelu_tanh(ht).astype(BF16) * w_ref[...]
    acc_ref[...] += jnp.dot(vt_ref[...], act_ref[...], preferred_element_type=F32)

    @pl.when(e == pl.num_programs(1) - 1)
    def _():
        y = _layer_norm(DN_ALPHA * x_ref[...] + acc_ref[...].T, g_ref[...], b_ref[...])
        o_ref[...] = y
        ob_ref[...] = y.astype(BF16)


def _peer(xb, x, u, vt, routing, g, b):
    T = xb.shape[0]
    tm = min(PEER_TM, T)
    rows = PEER_ROWS
    nk = PEER_N_KEYS
    rspec = pl.BlockSpec((PEER_HEADS, nk, tm), lambda t, e: (0, 0, t))
    tok = pl.BlockSpec((tm, D_MODEL), lambda t, e: (t, 0))
    const = pl.BlockSpec((1, D_MODEL), lambda t, e: (0, 0))
    return pl.pallas_call(
        functools.partial(_peer_kernel, rows=rows),
        grid=(T // tm, nk // rows),
        in_specs=[tok,
                  pl.BlockSpec((rows * nk, D_MODEL), lambda t, e: (e, 0)),
                  pl.BlockSpec((D_MODEL, rows * nk), lambda t, e: (0, e)),
                  rspec, rspec, rspec, rspec, tok, const, const],
        out_specs=[tok, tok],
        out_shape=[jax.ShapeDtypeStruct((T, D_MODEL), F32), jax.ShapeDtypeStruct((T, D_MODEL), BF16)],
        scratch_shapes=[pltpu.VMEM((rows * nk, tm), BF16),
                        pltpu.VMEM((rows * nk, tm), BF16),
                        pltpu.VMEM((D_MODEL, tm), F32)],
        compiler_params=_params(("parallel", "arbitrary")),
        name="peer_dense",
    )(xb, u, vt, *routing, x, g, b)


def _rope_tables(S):
    pos = jnp.arange(S, dtype=F32)
    d = MOBA_HEAD_DIM
    inv = ROPE_THETA ** (-jnp.arange(0, d, 2, dtype=F32) / d)
    ang = pos[:, None] * inv[None, :]
    cos, sin = jnp.cos(ang), jnp.sin(ang)
    reps = LANES // d
    cos_m = jnp.tile(jnp.concatenate([cos, cos], axis=1), (1, reps))
    sin_m = jnp.tile(jnp.concatenate([-sin, sin], axis=1), (1, reps))
    d = RET_QK_DIM
    inv = 1.0 / (ROPE_THETA ** jnp.linspace(0.0, 1.0, d // 2, dtype=F32))
    ang = pos[:, None] * inv[None, :]
    cos, sin = jnp.cos(ang), jnp.sin(ang)
    cos_r = jnp.concatenate([cos, cos], axis=1)
    sin_r = jnp.concatenate([-sin, sin], axis=1)
    return cos_m, sin_m, cos_r, sin_r


def _ret_column_perm():
    within = np.concatenate([np.arange(0, RET_QK_DIM, 2), np.arange(1, RET_QK_DIM, 2)])
    return np.concatenate([h * RET_QK_DIM + within for h in range(RET_HEADS)])


def kernel(x, w_in, w_moba_out, w_ret_out, w_out, ln1_g, ln1_b, peer_w_query, peer_sub_keys,
           peer_u, peer_v, ln2_g, ln2_b):
    B, S, D = x.shape
    assert D == D_MODEL and S % MOBA_BLOCK == 0 and S % RET_CHUNK == 0
    T = B * S
    tabs = _rope_tables(S)
    perm = _ret_column_perm()
    o = IN_OFFSETS
    xf = x.reshape(T, D).astype(F32)
    xb = xf.astype(BF16)
    for l in range(DEPTH):
        w = w_in[l]
        w_main = jnp.concatenate(
            [w[:, o[0]:o[2]], w[:, o[3]:o[4]][:, perm], w[:, o[4]:o[5]][:, perm], w[:, o[5]:o[9]]],
            axis=1).astype(BF16)
        w_vt = w[:, o[2]:o[3]].T.astype(BF16)
        proj = _inproj(xb, w_main, tabs, S)
        vt = _moba_values_t(xb, w_vt, S)
        yat = _moba(proj, vt, B, S)
        yr = _retention(proj, B, S)
        xf, xb = _merge(yat, yr, proj, xf, w_moba_out[l].astype(BF16), w_ret_out[l].astype(BF16),
                        w_out[l].astype(BF16), ln1_g[l].reshape(1, D), ln1_b[l].reshape(1, D), B, S)
        routing = _route(xb, peer_w_query[l].astype(BF16), peer_sub_keys[l].astype(BF16))
        xf, xb = _peer(xb, xf, peer_u[l].astype(BF16), peer_v[l].T.astype(BF16), routing,
                       ln2_g[l].reshape(1, D), ln2_b[l].reshape(1, D))
    return xf.reshape(B, S, D).astype(x.dtype)
```

```python
import functools
import math

import numpy as np
import jax
import jax.numpy as jnp
from jax import lax
from jax.experimental import pallas as pl
from jax.experimental.pallas import tpu as pltpu

F32 = jnp.float32
BF16 = jnp.bfloat16

D_MODEL = 1024
DEPTH = 2
MOBA_HEADS = 8
MOBA_HEAD_DIM = 64
MOBA_WIDTH = MOBA_HEADS * MOBA_HEAD_DIM
MOBA_BLOCK = 256
MOBA_TOPK = 3
ROPE_THETA = 10000.0
RET_HEADS = 4
RET_QK_DIM = 128
RET_V_DIM = 256
RET_QK_WIDTH = RET_HEADS * RET_QK_DIM
RET_V_WIDTH = RET_HEADS * RET_V_DIM
RET_CHUNK = 256
PEER_N_KEYS = 128
PEER_N_EXPERTS = PEER_N_KEYS * PEER_N_KEYS
PEER_HEADS = 8
PEER_KEY_DIM = 256
PEER_TOPK = 16
DN_ALPHA = (2.0 * DEPTH) ** 0.25
LN_EPS = 1e-5
NEG = -1e30

IN_SIZES = (MOBA_WIDTH, MOBA_WIDTH, MOBA_WIDTH, RET_QK_WIDTH, RET_QK_WIDTH,
            RET_V_WIDTH, RET_V_WIDTH, D_MODEL, D_MODEL)
IN_OFFSETS = tuple(int(v) for v in np.concatenate([[0], np.cumsum(IN_SIZES)]))

LANES = 128
SUBLANES = 8
VMEM_LIMIT = 56 * 1024 * 1024

PROJ_TM = 1024
PROJ_TN = 512
ROUTE_TM = 256
PEER_TM = 512
PEER_ROWS = 16
PEER_LANE_CHUNK = 256
PEER_SCHED_FLAGS = None

_NT = (((1,), (1,)), ((), ()))


def _params(sem, flags=None):
    return pltpu.CompilerParams(dimension_semantics=sem, vmem_limit_bytes=VMEM_LIMIT, flags=flags)


def _rotate_groups(acc, cos, sin, o_ref, partner_fn):
    for g in range(PROJ_TN // LANES):
        xg = acc[:, g * LANES:(g + 1) * LANES]
        o_ref[:, g * LANES:(g + 1) * LANES] = (xg * cos + partner_fn(xg) * sin).astype(o_ref.dtype)


def _inproj_kernel(x_ref, w_ref, cm_ref, sm_ref, cr_ref, sr_ref, o_ref):
    j = pl.program_id(1)
    acc = jnp.dot(x_ref[...], w_ref[...], preferred_element_type=F32)
    tm = acc.shape[0]

    @pl.when(j < 2)
    def _():
        lane = lax.broadcasted_iota(jnp.int32, (tm, LANES), 1)
        first = (lane % MOBA_HEAD_DIM) < (MOBA_HEAD_DIM // 2)

        def partner(xg):
            return jnp.where(first, pltpu.roll(xg, LANES - MOBA_HEAD_DIM // 2, axis=1),
                             pltpu.roll(xg, MOBA_HEAD_DIM // 2, axis=1))

        _rotate_groups(acc, cm_ref[...], sm_ref[...], o_ref, partner)

    @pl.when((j >= 2) & (j < 4))
    def _():
        _rotate_groups(acc, cr_ref[...], sr_ref[...], o_ref,
                       lambda xg: pltpu.roll(xg, RET_QK_DIM // 2, axis=1))

    @pl.when(j >= 4)
    def _():
        o_ref[...] = acc.astype(o_ref.dtype)


def _inproj(xb, w_main, tabs, S):
    T = xb.shape[0]
    tm = min(PROJ_TM, S)
    n_col = w_main.shape[1] // PROJ_TN
    pos_blocks = S // tm
    tab_spec = pl.BlockSpec((tm, LANES), lambda i, j: (i % pos_blocks, 0))
    return pl.pallas_call(
        _inproj_kernel,
        grid=(T // tm, n_col),
        in_specs=[pl.BlockSpec((tm, D_MODEL), lambda i, j: (i, 0)),
                  pl.BlockSpec((D_MODEL, PROJ_TN), lambda i, j: (0, j)),
                  tab_spec, tab_spec, tab_spec, tab_spec],
        out_specs=pl.BlockSpec((tm, PROJ_TN), lambda i, j: (i, j)),
        out_shape=jax.ShapeDtypeStruct((T, w_main.shape[1]), BF16),
        compiler_params=_params(("parallel", "arbitrary")),
        name="inproj",
    )(xb, w_main, *tabs)


MOBA_VROWS = MOBA_HEAD_DIM + 2 * SUBLANES


def _vt_kernel(x_ref, w_ref, o_ref):
    res = lax.dot_general(w_ref[...], x_ref[...], _NT, preferred_element_type=F32)
    hd = MOBA_HEAD_DIM
    pad = MOBA_VROWS - hd
    ones_row = jnp.where(lax.broadcasted_iota(jnp.int32, (pad, MOBA_BLOCK), 0) == 0, 1.0, 0.0).astype(o_ref.dtype)
    for c in range(o_ref.shape[0]):
        for h in range(MOBA_HEADS):
            o_ref[c, h, 0:hd, :] = res[h * hd:(h + 1) * hd, c * MOBA_BLOCK:(c + 1) * MOBA_BLOCK].astype(o_ref.dtype)
            o_ref[c, h, hd:MOBA_VROWS, :] = ones_row


def _moba_values_t(xb, w_vt, S):
    T = xb.shape[0]
    tm = min(PROJ_TM, S)
    per = tm // MOBA_BLOCK
    return pl.pallas_call(
        _vt_kernel,
        grid=(T // tm,),
        in_specs=[pl.BlockSpec((tm, D_MODEL), lambda i: (i, 0)),
                  pl.BlockSpec((MOBA_WIDTH, D_MODEL), lambda i: (0, 0))],
        out_specs=pl.BlockSpec((per, MOBA_HEADS, MOBA_VROWS, MOBA_BLOCK), lambda i: (i, 0, 0, 0)),
        out_shape=jax.ShapeDtypeStruct((T // MOBA_BLOCK, MOBA_HEADS, MOBA_VROWS, MOBA_BLOCK), BF16),
        compiler_params=_params(("parallel",)),
        name="moba_vt",
    )(xb, w_vt)


def _moba_kernel(q_ref, k_ref, vt_ref, o_ref, kmean_ref, bias_ref, qs_ref, acc_ref, m_ref, *, nb):
    i = pl.program_id(1)
    L = MOBA_BLOCK
    hd = MOBA_HEAD_DIM

    @pl.when(i == 0)
    def _():
        def body(j, c):
            kb = k_ref[pl.ds(pl.multiple_of(j * L, L), L), :].astype(F32)
            kmean_ref[pl.ds(j, 1), :] = jnp.sum(kb, axis=0, keepdims=True) * (1.0 / L)
            return c
        lax.fori_loop(0, nb, body, 0)

    per = LANES // hd
    groups = MOBA_HEADS // per
    W = MOBA_HEADS * L
    lane = lax.broadcasted_iota(jnp.int32, (L, LANES), 1)
    row0 = pl.multiple_of(i * L, L)

    gates = []
    for g in range(groups):
        cols = slice(g * LANES, (g + 1) * LANES)
        q_pair = q_ref[:, cols]
        km = kmean_ref[:, cols].astype(BF16)
        for hh in range(per):
            qm = jnp.where((lane // hd) == hh, q_pair, jnp.zeros_like(q_pair))
            gates.append(lax.dot_general(km, qm, _NT, preferred_element_type=F32))
            qs_ref[g, hh * L:(hh + 1) * L, :] = (qm.astype(F32) * (hd ** -0.5 * math.log2(math.e))).astype(BF16)

    blk = lax.broadcasted_iota(jnp.int32, (nb, W), 0)
    blkf = blk.astype(F32)
    gm = jnp.where(blk < i, jnp.concatenate(gates, axis=1), -jnp.inf)
    keep = jnp.zeros((nb, W), F32)
    for _ in range(MOBA_TOPK):
        best = jnp.max(gm, axis=0, keepdims=True)
        first = jnp.min(jnp.where(gm == best, blkf, float(nb)), axis=0, keepdims=True)
        hit = blkf == first
        keep = jnp.where(hit, 1.0, keep)
        gm = jnp.where(hit, -jnp.inf, gm)
    bias_ref[...] = jnp.where(blk < i, jnp.where(keep > 0.0, 0.0, NEG), NEG)

    def scores(r):
        return jnp.concatenate(
            [lax.dot_general(k_ref[pl.ds(r, L), g * LANES:(g + 1) * LANES], qs_ref[g], _NT,
                             preferred_element_type=F32) for g in range(groups)], axis=1)

    def values(j, p):
        pb = p.astype(BF16)
        return [jnp.dot(vt_ref[j, h], pb[:, h * L:(h + 1) * L], preferred_element_type=F32)
                for h in range(MOBA_HEADS)]

    kpos = lax.broadcasted_iota(jnp.int32, (L, W), 0)
    qpos = lax.broadcasted_iota(jnp.int32, (L, W), 1) % L
    s = jnp.where(kpos <= qpos, scores(row0), NEG)
    m = jnp.max(s, axis=0, keepdims=True)
    m_ref[...] = jnp.broadcast_to(m, (SUBLANES, W))
    for h, pv in enumerate(values(i, jnp.exp2(s - m))):
        acc_ref[h] = pv

    def past(j):
        s = scores(pl.multiple_of(j * L, L)) + bias_ref[pl.ds(j, 1), :]
        m_old = m_ref[...]
        m_new = jnp.maximum(m_old, jnp.max(s, axis=0, keepdims=True))
        alpha = jnp.exp2(m_old - m_new)
        m_ref[...] = m_new
        for h, pv in enumerate(values(j, jnp.exp2(s - m_new[0:1, :]))):
            acc_ref[h] = alpha[0:1, h * L:(h + 1) * L] * acc_ref[h] + pv

    def two_past(jj, c):
        past(2 * jj)
        past(2 * jj + 1)
        return c

    lax.fori_loop(0, i // 2, two_past, 0)

    @pl.when(i % 2 == 1)
    def _():
        past(i - 1)

    for h in range(MOBA_HEADS):
        o_ref[0, h * hd:(h + 1) * hd, :] = acc_ref[h, 0:hd, :] / acc_ref[h, hd:hd + 1, :]


def _moba(proj, vt, B, S):
    nb = S // MOBA_BLOCK
    L = MOBA_BLOCK
    return pl.pallas_call(
        functools.partial(_moba_kernel, nb=nb),
        grid=(B, nb),
        in_specs=[pl.BlockSpec((L, MOBA_WIDTH), lambda b, i: (b * nb + i, 0)),
                  pl.BlockSpec((S, MOBA_WIDTH), lambda b, i: (b, 1)),
                  pl.BlockSpec((nb, MOBA_HEADS, MOBA_VROWS, L), lambda b, i: (b, 0, 0, 0))],
        out_specs=pl.BlockSpec((1, MOBA_WIDTH, L), lambda b, i: (b, 0, i)),
        out_shape=jax.ShapeDtypeStruct((B, MOBA_WIDTH, S), F32),
        scratch_shapes=[pltpu.VMEM((nb, MOBA_WIDTH), F32),
                        pltpu.VMEM((nb, MOBA_HEADS * L), F32),
                        pltpu.VMEM((MOBA_HEADS // (LANES // MOBA_HEAD_DIM), (LANES // MOBA_HEAD_DIM) * L, LANES),
                                   BF16),
                        pltpu.VMEM((MOBA_HEADS, MOBA_VROWS, L), F32),
                        pltpu.VMEM((SUBLANES, MOBA_HEADS * L), F32)],
        compiler_params=_params(("parallel", "arbitrary")),
        name="moba",
    )(proj, proj, vt)


def _ret_log_g():
    return jnp.log(1.0 - 2.0 ** (-5.0 - jnp.arange(RET_HEADS, dtype=F32)))


def _retention_kernel(q_ref, k_ref, v_ref, g_ref, dec_ref, qd_ref, kd_ref, cd_ref, o_ref, state_ref):
    n = pl.program_id(1)

    @pl.when(n == 0)
    def _():
        state_ref[...] = jnp.zeros_like(state_ref)

    for h in range(RET_HEADS):
        qk = slice(h * RET_QK_DIM, (h + 1) * RET_QK_DIM)
        vv = slice(h * RET_V_DIM, (h + 1) * RET_V_DIM)
        q = q_ref[:, qk]
        k = k_ref[:, qk]
        v = v_ref[:, vv]
        scores = lax.dot_general(q, k, _NT, preferred_element_type=F32) * dec_ref[h]
        y = jnp.dot(scores.astype(BF16), v, preferred_element_type=F32)
        state = state_ref[h]
        y = y + jnp.dot(q, state.astype(BF16), preferred_element_type=F32) * qd_ref[h]
        kt = (k.astype(F32) * kd_ref[h]).T.astype(BF16)
        kv = jnp.dot(kt, v, preferred_element_type=F32)
        state_ref[h] = state * cd_ref[h:h + 1, :] + kv
        mu = jnp.mean(y, axis=-1, keepdims=True)
        yc = y - mu
        var = jnp.mean(yc * yc, axis=-1, keepdims=True)
        yn = yc * lax.rsqrt(var + LN_EPS)
        gate = g_ref[:, vv].astype(F32)
        o_ref[:, vv] = (gate * jax.nn.sigmoid(gate) * yn).astype(o_ref.dtype)


def _retention(proj, B, S):
    C = RET_CHUNK
    nc = S // C
    T = B * S
    log_g = _ret_log_g()
    pos = jnp.arange(C, dtype=F32)
    diff = pos[:, None] - pos[None, :]
    scale = RET_QK_DIM ** -0.5
    decay = jnp.where(diff >= 0, jnp.exp(log_g[:, None, None] * jnp.maximum(diff, 0.0)), 0.0) * scale
    q_decay = jnp.broadcast_to(jnp.exp(log_g[:, None] * (pos + 1.0))[:, :, None], (RET_HEADS, C, RET_V_DIM))
    k_decay = jnp.broadcast_to((jnp.exp(log_g[:, None] * (C - 1.0 - pos)) * scale)[:, :, None],
                               (RET_HEADS, C, RET_QK_DIM))
    chunk_decay = jnp.broadcast_to(jnp.exp(log_g * C)[:, None], (RET_HEADS, RET_V_DIM))
    const = lambda shape: pl.BlockSpec(shape, lambda b, n: (0,) * len(shape))
    return pl.pallas_call(
        _retention_kernel,
        grid=(B, nc),
        in_specs=[pl.BlockSpec((C, RET_QK_WIDTH), lambda b, n: (b * nc + n, 2)),
                  pl.BlockSpec((C, RET_QK_WIDTH), lambda b, n: (b * nc + n, 3)),
                  pl.BlockSpec((C, RET_V_WIDTH), lambda b, n: (b * nc + n, 2)),
                  pl.BlockSpec((C, RET_V_WIDTH), lambda b, n: (b * nc + n, 3)),
                  const((RET_HEADS, C, C)), const((RET_HEADS, C, RET_V_DIM)),
                  const((RET_HEADS, C, RET_QK_DIM)), const((RET_HEADS, RET_V_DIM))],
        out_specs=pl.BlockSpec((C, RET_V_WIDTH), lambda b, n: (b * nc + n, 0)),
        out_shape=jax.ShapeDtypeStruct((T, RET_V_WIDTH), BF16),
        scratch_shapes=[pltpu.VMEM((RET_HEADS, RET_QK_DIM, RET_V_DIM), F32)],
        compiler_params=_params(("parallel", "arbitrary")),
        name="retention",
    )(proj, proj, proj, proj, decay, q_decay, k_decay, chunk_decay)


def _layer_norm(y, g, b):
    mu = jnp.mean(y, axis=-1, keepdims=True)
    yc = y - mu
    var = jnp.mean(yc * yc, axis=-1, keepdims=True)
    return yc * lax.rsqrt(var + LN_EPS) * g + b


def _merge_kernel(yat_ref, yr_ref, ga_ref, gr_ref, x_ref, wa_ref, wr_ref, wo_ref, g_ref, b_ref, o_ref, ob_ref):
    ya = yat_ref[0].T.astype(BF16)
    branch_a = jnp.dot(ya, wa_ref[...], preferred_element_type=F32)
    branch_r = jnp.dot(yr_ref[...], wr_ref[...], preferred_element_type=F32)
    merged = (jax.nn.sigmoid(ga_ref[...].astype(F32)) * branch_a
              + jax.nn.sigmoid(gr_ref[...].astype(F32)) * branch_r)
    mix = jnp.dot(merged.astype(BF16), wo_ref[...], preferred_element_type=F32)
    y = _layer_norm(DN_ALPHA * x_ref[...] + mix, g_ref[...], b_ref[...])
    o_ref[...] = y
    ob_ref[...] = y.astype(BF16)


def _merge(yat, yr, proj, x, wa, wr, wo, g, b, B, S):
    L = MOBA_BLOCK
    nb = S // L
    T = B * S
    tok = lambda c: pl.BlockSpec((L, D_MODEL), lambda bb, i: (bb * nb + i, c))
    const = lambda shape: pl.BlockSpec(shape, lambda bb, i: (0,) * len(shape))
    return pl.pallas_call(
        _merge_kernel,
        grid=(B, nb),
        in_specs=[pl.BlockSpec((1, MOBA_WIDTH, L), lambda bb, i: (bb, 0, i)),
                  tok(0), tok(4), tok(5), tok(0),
                  const((MOBA_WIDTH, D_MODEL)), const((RET_V_WIDTH, D_MODEL)), const((D_MODEL, D_MODEL)),
                  const((1, D_MODEL)), const((1, D_MODEL))],
        out_specs=[tok(0), tok(0)],
        out_shape=[jax.ShapeDtypeStruct((T, D_MODEL), F32), jax.ShapeDtypeStruct((T, D_MODEL), BF16)],
        compiler_params=_params(("parallel", "parallel")),
        name="merge_ln1",
    )(yat, yr, proj, proj, x, wa, wr, wo, g, b)


def _cmpx(xs, a, b):
    hi = jnp.maximum(xs[a], xs[b])
    lo = jnp.minimum(xs[a], xs[b])
    xs[a], xs[b] = hi, lo


def _bitonic_merge_desc(xs):
    n = len(xs)
    j = n // 2
    while j >= 1:
        for a in range(n):
            b = a ^ j
            if b > a:
                _cmpx(xs, a, b)
        j //= 2
    return xs


def _sort_desc(xs):
    xs = list(xs)
    n = len(xs)
    k = 2
    while k <= n:
        j = k // 2
        while j >= 1:
            for a in range(n):
                b = a ^ j
                if b > a:
                    if (a & k) == 0:
                        _cmpx(xs, a, b)
                    else:
                        _cmpx(xs, b, a)
            j //= 2
        k *= 2
    return xs


def _merge_top(xs, ys):
    n = len(xs)
    return _bitonic_merge_desc([jnp.maximum(xs[v], ys[n - 1 - v]) for v in range(n)])


def _across_sublanes(xs):
    for shift in (4, 2, 1):
        xs = _merge_top(xs, [pltpu.roll(x, shift, axis=0) for x in xs])
    return xs


def _top16_desc(s):
    rows = [s[SUBLANES * v:SUBLANES * (v + 1), :] for v in range(s.shape[0] // SUBLANES)]
    return _across_sublanes(_sort_desc(rows))


def _min_over_sublanes(x):
    for shift in (4, 2, 1):
        x = jnp.minimum(x, pltpu.roll(x, shift, axis=0))
    return x


def _route_kernel(x_ref, wq_ref, sk_ref, r1_ref, e1_ref, n0_ref, c0_ref):
    q = jnp.dot(x_ref[...], wq_ref[...], preferred_element_type=F32)
    tm = q.shape[0]
    K = PEER_TOPK
    half = PEER_KEY_DIM // 2
    sub = lax.broadcasted_iota(jnp.int32, (SUBLANES, tm), 0)
    inf = jnp.inf
    for h in range(PEER_HEADS):
        q0 = q[:, (2 * h) * half:(2 * h + 1) * half].astype(BF16)
        q1 = q[:, (2 * h + 1) * half:(2 * h + 2) * half].astype(BF16)
        s0 = lax.dot_general(sk_ref[0], q0, _NT, preferred_element_type=F32)
        s1 = lax.dot_general(sk_ref[1], q1, _NT, preferred_element_type=F32)
        a = _top16_desc(s0)
        b = _top16_desc(s1)
        a_lo, a_hi = a[0], a[SUBLANES]
        for r in range(1, SUBLANES):
            a_lo = jnp.where(sub == r, a[r], a_lo)
            a_hi = jnp.where(sub == r, a[SUBLANES + r], a_hi)
        x_lo = [a_lo + b[c] for c in range(K)]
        x_hi = [a_hi + b[c] for c in range(K)]
        z = _across_sublanes(_merge_top(x_lo, x_hi))
        tau = z[K - 1]
        zsum = jnp.exp(z[0] - z[0])
        for v in range(1, K):
            zsum = zsum + jnp.exp(z[v] - z[0])
        inv_z = (1.0 / zsum)[0:1, :]
        count = jnp.zeros(s0.shape, F32)
        for c in range(K):
            alpha = jnp.minimum(jnp.where(x_lo[c] >= tau, a_lo, inf), jnp.where(x_hi[c] >= tau, a_hi, inf))
            alpha = _min_over_sublanes(alpha)[0:1, :]
            count = jnp.where(s0 >= alpha, float(c + 1), count)
        rank = jnp.full(s1.shape, float(K), F32)
        for c in reversed(range(K)):
            rank = jnp.where(s1 >= b[c][0:1, :], float(c), rank)
        r1_ref[h] = rank.astype(BF16)
        e1_ref[h] = jnp.exp(s1 - b[0][0:1, :]).astype(BF16)
        n0_ref[h] = count
        c0_ref[h] = jnp.exp(s0 - a[0][0:1, :]) * inv_z


def _route(xb, wq, sk):
    T = xb.shape[0]
    tm = ROUTE_TM
    spec = pl.BlockSpec((PEER_HEADS, PEER_N_KEYS, tm), lambda t: (0, 0, t))
    half = jax.ShapeDtypeStruct((PEER_HEADS, PEER_N_KEYS, T), BF16)
    word = jax.ShapeDtypeStruct((PEER_HEADS, PEER_N_KEYS, T), F32)
    return pl.pallas_call(
        _route_kernel,
        grid=(T // tm,),
        in_specs=[pl.BlockSpec((tm, D_MODEL), lambda t: (t, 0)),
                  pl.BlockSpec(wq.shape, lambda t: (0, 0)),
                  pl.BlockSpec(sk.shape, lambda t: (0, 0, 0))],
        out_specs=[spec, spec, spec, spec],
        out_shape=[half, half, word, word],
        compiler_params=_params(("parallel",)),
        name="peer_route",
    )(xb, wq, sk)


def _gelu_tanh(x):
    k = -2.0 * math.sqrt(2.0 / math.pi) * math.log2(math.e)
    z = x * (k + (k * 0.044715) * (x * x))
    return x / (1.0 + jnp.exp2(z))


def _peer_kernel(xb_ref, u_ref, vt_ref, r1_ref, e1_ref, n0_ref, c0_ref, x_ref, g_ref, b_ref,
                 o_ref, ob_ref, w_ref, act_ref, acc_ref, *, rows):
    e = pl.program_id(1)
    nk = PEER_N_KEYS
    tm = xb_ref.shape[0]
    pk = 2 * SUBLANES

    @pl.when(e == 0)
    def _():
        acc_ref[...] = jnp.zeros_like(acc_ref)

    tc = min(PEER_LANE_CHUNK, tm)

    for r in range(rows):
        i = e * rows + r
        for ch in range(tm // tc):
            ls = slice(ch * tc, (ch + 1) * tc)
            w = [None] * (nk // pk)
            for h in range(PEER_HEADS):
                count = jnp.broadcast_to(n0_ref[h, pl.ds(i, 1), ls], (pk, tc)).astype(BF16)
                weight = jnp.broadcast_to(c0_ref[h, pl.ds(i, 1), ls], (pk, tc)).astype(BF16)
                for g in range(nk // pk):
                    rs = slice(g * pk, (g + 1) * pk)
                    sel = jnp.where(r1_ref[h, rs, ls] < count, e1_ref[h, rs, ls], jnp.zeros((pk, tc), BF16))
                    term = sel * weight
                    w[g] = term if w[g] is None else w[g] + term
            for g in range(nk // pk):
                w_ref[r * nk + g * pk:r * nk + (g + 1) * pk, ls] = w[g]

    ht = lax.dot_general(u_ref[...], xb_ref[...], _NT, preferred_element_type=F32)
    act_ref[...] = _gelu_tanh(ht.astype(BF16)) * w_ref[...]
    acc_ref[...] += jnp.dot(vt_ref[...], act_ref[...], preferred_element_type=F32)

    @pl.when(e == pl.num_programs(1) - 1)
    def _():
        y = _layer_norm(DN_ALPHA * x_ref[...] + acc_ref[...].T, g_ref[...], b_ref[...])
        o_ref[...] = y
        ob_ref[...] = y.astype(BF16)


def _peer(xb, x, u, vt, routing, g, b):
    T = xb.shape[0]
    tm = min(PEER_TM, T)
    rows = PEER_ROWS
    nk = PEER_N_KEYS
    rspec = pl.BlockSpec((PEER_HEADS, nk, tm), lambda t, e: (0, 0, t))
    tok = pl.BlockSpec((tm, D_MODEL), lambda t, e: (t, 0))
    const = pl.BlockSpec((1, D_MODEL), lambda t, e: (0, 0))
    return pl.pallas_call(
        functools.partial(_peer_kernel, rows=rows),
        grid=(T // tm, nk // rows),
        in_specs=[tok,
                  pl.BlockSpec((rows * nk, D_MODEL), lambda t, e: (e, 0)),
                  pl.BlockSpec((D_MODEL, rows * nk), lambda t, e: (0, e)),
                  rspec, rspec, rspec, rspec, tok, const, const],
        out_specs=[tok, tok],
        out_shape=[jax.ShapeDtypeStruct((T, D_MODEL), F32), jax.ShapeDtypeStruct((T, D_MODEL), BF16)],
        scratch_shapes=[pltpu.VMEM((rows * nk, tm), BF16),
                        pltpu.VMEM((rows * nk, tm), BF16),
                        pltpu.VMEM((D_MODEL, tm), F32)],
        compiler_params=_params(("parallel", "arbitrary"), PEER_SCHED_FLAGS),
        name="peer_dense",
    )(xb, u, vt, *routing, x, g, b)


def _rope_tables(S):
    pos = jnp.arange(S, dtype=F32)
    d = MOBA_HEAD_DIM
    inv = ROPE_THETA ** (-jnp.arange(0, d, 2, dtype=F32) / d)
    ang = pos[:, None] * inv[None, :]
    cos, sin = jnp.cos(ang), jnp.sin(ang)
    reps = LANES // d
    cos_m = jnp.tile(jnp.concatenate([cos, cos], axis=1), (1, reps))
    sin_m = jnp.tile(jnp.concatenate([-sin, sin], axis=1), (1, reps))
    d = RET_QK_DIM
    inv = 1.0 / (ROPE_THETA ** jnp.linspace(0.0, 1.0, d // 2, dtype=F32))
    ang = pos[:, None] * inv[None, :]
    cos, sin = jnp.cos(ang), jnp.sin(ang)
    cos_r = jnp.concatenate([cos, cos], axis=1)
    sin_r = jnp.concatenate([-sin, sin], axis=1)
    return cos_m, sin_m, cos_r, sin_r


def _ret_column_perm():
    within = np.concatenate([np.arange(0, RET_QK_DIM, 2), np.arange(1, RET_QK_DIM, 2)])
    return np.concatenate([h * RET_QK_DIM + within for h in range(RET_HEADS)])


def kernel(x, w_in, w_moba_out, w_ret_out, w_out, ln1_g, ln1_b, peer_w_query, peer_sub_keys,
           peer_u, peer_v, ln2_g, ln2_b):
    B, S, D = x.shape
    assert D == D_MODEL and S % MOBA_BLOCK == 0 and S % RET_CHUNK == 0
    T = B * S
    tabs = _rope_tables(S)
    perm = _ret_column_perm()
    o = IN_OFFSETS
    xf = x.reshape(T, D).astype(F32)
    xb = xf.astype(BF16)
    for l in range(DEPTH):
        w = w_in[l]
        w_main = jnp.concatenate(
            [w[:, o[0]:o[2]], w[:, o[3]:o[4]][:, perm], w[:, o[4]:o[5]][:, perm], w[:, o[5]:o[9]]],
            axis=1).astype(BF16)
        w_vt = w[:, o[2]:o[3]].T.astype(BF16)
        proj = _inproj(xb, w_main, tabs, S)
        vt = _moba_values_t(xb, w_vt, S)
        yat = _moba(proj, vt, B, S)
        yr = _retention(proj, B, S)
        xf, xb = _merge(yat, yr, proj, xf, w_moba_out[l].astype(BF16), w_ret_out[l].astype(BF16),
                        w_out[l].astype(BF16), ln1_g[l].reshape(1, D), ln1_b[l].reshape(1, D), B, S)
        routing = _route(xb, peer_w_query[l].astype(BF16), peer_sub_keys[l].astype(BF16))
        xf, xb = _peer(xb, xf, peer_u[l].astype(BF16), peer_v[l].T.astype(BF16), routing,
                       ln2_g[l].reshape(1, D), ln2_b[l].reshape(1, D))
    return xf.reshape(B, S, D).astype(x.dtype)
```

```python
import functools
import math

import numpy as np
import jax
import jax.numpy as jnp
from jax import lax
from jax.experimental import pallas as pl
from jax.experimental.pallas import tpu as pltpu

F32 = jnp.float32
BF16 = jnp.bfloat16

D_MODEL = 1024
DEPTH = 2
MOBA_HEADS = 8
MOBA_HEAD_DIM = 64
MOBA_WIDTH = MOBA_HEADS * MOBA_HEAD_DIM
MOBA_BLOCK = 256
MOBA_TOPK = 3
ROPE_THETA = 10000.0
RET_HEADS = 4
RET_QK_DIM = 128
RET_V_DIM = 256
RET_QK_WIDTH = RET_HEADS * RET_QK_DIM
RET_V_WIDTH = RET_HEADS * RET_V_DIM
RET_CHUNK = 256
PEER_N_KEYS = 128
PEER_N_EXPERTS = PEER_N_KEYS * PEER_N_KEYS
PEER_HEADS = 8
PEER_KEY_DIM = 256
PEER_TOPK = 16
DN_ALPHA = (2.0 * DEPTH) ** 0.25
LN_EPS = 1e-5
NEG = -1e30

IN_SIZES = (MOBA_WIDTH, MOBA_WIDTH, MOBA_WIDTH, RET_QK_WIDTH, RET_QK_WIDTH,
            RET_V_WIDTH, RET_V_WIDTH, D_MODEL, D_MODEL)
IN_OFFSETS = tuple(int(v) for v in np.concatenate([[0], np.cumsum(IN_SIZES)]))

LANES = 128
SUBLANES = 8
VMEM_LIMIT = 56 * 1024 * 1024

PROJ_TM = 1024
PROJ_TN = 512
ROUTE_TM = 256
PEER_TM = 512
PEER_ROWS = 16
PEER_LANE_CHUNK = 256
PEER_CHUNK_ROWS = 2
PEER_SCHED_FLAGS = None

_NT = (((1,), (1,)), ((), ()))


def _params(sem, flags=None):
    return pltpu.CompilerParams(dimension_semantics=sem, vmem_limit_bytes=VMEM_LIMIT, flags=flags)


def _rotate_groups(acc, cos, sin, o_ref, partner_fn):
    for g in range(PROJ_TN // LANES):
        xg = acc[:, g * LANES:(g + 1) * LANES]
        o_ref[:, g * LANES:(g + 1) * LANES] = (xg * cos + partner_fn(xg) * sin).astype(o_ref.dtype)


def _inproj_kernel(x_ref, w_ref, cm_ref, sm_ref, cr_ref, sr_ref, o_ref):
    j = pl.program_id(1)
    acc = jnp.dot(x_ref[...], w_ref[...], preferred_element_type=F32)
    tm = acc.shape[0]

    @pl.when(j < 2)
    def _():
        lane = lax.broadcasted_iota(jnp.int32, (tm, LANES), 1)
        first = (lane % MOBA_HEAD_DIM) < (MOBA_HEAD_DIM // 2)

        def partner(xg):
            return jnp.where(first, pltpu.roll(xg, LANES - MOBA_HEAD_DIM // 2, axis=1),
                             pltpu.roll(xg, MOBA_HEAD_DIM // 2, axis=1))

        _rotate_groups(acc, cm_ref[...], sm_ref[...], o_ref, partner)

    @pl.when((j >= 2) & (j < 4))
    def _():
        _rotate_groups(acc, cr_ref[...], sr_ref[...], o_ref,
                       lambda xg: pltpu.roll(xg, RET_QK_DIM // 2, axis=1))

    @pl.when(j >= 4)
    def _():
        o_ref[...] = acc.astype(o_ref.dtype)


def _inproj(xb, w_main, tabs, S):
    T = xb.shape[0]
    tm = min(PROJ_TM, S)
    n_col = w_main.shape[1] // PROJ_TN
    pos_blocks = S // tm
    tab_spec = pl.BlockSpec((tm, LANES), lambda i, j: (i % pos_blocks, 0))
    return pl.pallas_call(
        _inproj_kernel,
        grid=(T // tm, n_col),
        in_specs=[pl.BlockSpec((tm, D_MODEL), lambda i, j: (i, 0)),
                  pl.BlockSpec((D_MODEL, PROJ_TN), lambda i, j: (0, j)),
                  tab_spec, tab_spec, tab_spec, tab_spec],
        out_specs=pl.BlockSpec((tm, PROJ_TN), lambda i, j: (i, j)),
        out_shape=jax.ShapeDtypeStruct((T, w_main.shape[1]), BF16),
        compiler_params=_params(("parallel", "arbitrary")),
        name="inproj",
    )(xb, w_main, *tabs)


MOBA_VROWS = MOBA_HEAD_DIM + 2 * SUBLANES


def _vt_kernel(x_ref, w_ref, o_ref):
    res = lax.dot_general(w_ref[...], x_ref[...], _NT, preferred_element_type=F32)
    hd = MOBA_HEAD_DIM
    pad = MOBA_VROWS - hd
    ones_row = jnp.where(lax.broadcasted_iota(jnp.int32, (pad, MOBA_BLOCK), 0) == 0, 1.0, 0.0).astype(o_ref.dtype)
    for c in range(o_ref.shape[0]):
        for h in range(MOBA_HEADS):
            o_ref[c, h, 0:hd, :] = res[h * hd:(h + 1) * hd, c * MOBA_BLOCK:(c + 1) * MOBA_BLOCK].astype(o_ref.dtype)
            o_ref[c, h, hd:MOBA_VROWS, :] = ones_row


def _moba_values_t(xb, w_vt, S):
    T = xb.shape[0]
    tm = min(PROJ_TM, S)
    per = tm // MOBA_BLOCK
    return pl.pallas_call(
        _vt_kernel,
        grid=(T // tm,),
        in_specs=[pl.BlockSpec((tm, D_MODEL), lambda i: (i, 0)),
                  pl.BlockSpec((MOBA_WIDTH, D_MODEL), lambda i: (0, 0))],
        out_specs=pl.BlockSpec((per, MOBA_HEADS, MOBA_VROWS, MOBA_BLOCK), lambda i: (i, 0, 0, 0)),
        out_shape=jax.ShapeDtypeStruct((T // MOBA_BLOCK, MOBA_HEADS, MOBA_VROWS, MOBA_BLOCK), BF16),
        compiler_params=_params(("parallel",)),
        name="moba_vt",
    )(xb, w_vt)


def _moba_kernel(q_ref, k_ref, vt_ref, o_ref, kmean_ref, bias_ref, qs_ref, acc_ref, m_ref, *, nb):
    i = pl.program_id(1)
    L = MOBA_BLOCK
    hd = MOBA_HEAD_DIM

    @pl.when(i == 0)
    def _():
        def body(j, c):
            kb = k_ref[pl.ds(pl.multiple_of(j * L, L), L), :].astype(F32)
            kmean_ref[pl.ds(j, 1), :] = jnp.sum(kb, axis=0, keepdims=True) * (1.0 / L)
            return c
        lax.fori_loop(0, nb, body, 0)

    per = LANES // hd
    groups = MOBA_HEADS // per
    W = MOBA_HEADS * L
    lane = lax.broadcasted_iota(jnp.int32, (L, LANES), 1)
    row0 = pl.multiple_of(i * L, L)

    gates = []
    for g in range(groups):
        cols = slice(g * LANES, (g + 1) * LANES)
        q_pair = q_ref[:, cols]
        km = kmean_ref[:, cols].astype(BF16)
        for hh in range(per):
            qm = jnp.where((lane // hd) == hh, q_pair, jnp.zeros_like(q_pair))
            gates.append(lax.dot_general(km, qm, _NT, preferred_element_type=F32))
            qs_ref[g, hh * L:(hh + 1) * L, :] = (qm.astype(F32) * (hd ** -0.5 * math.log2(math.e))).astype(BF16)

    blk = lax.broadcasted_iota(jnp.int32, (nb, W), 0)
    blkf = blk.astype(F32)
    gm = jnp.where(blk < i, jnp.concatenate(gates, axis=1), -jnp.inf)
    keep = jnp.zeros((nb, W), F32)
    for _ in range(MOBA_TOPK):
        best = jnp.max(gm, axis=0, keepdims=True)
        first = jnp.min(jnp.where(gm == best, blkf, float(nb)), axis=0, keepdims=True)
        hit = blkf == first
        keep = jnp.where(hit, 1.0, keep)
        gm = jnp.where(hit, -jnp.inf, gm)
    bias_ref[...] = jnp.where(blk < i, jnp.where(keep > 0.0, 0.0, NEG), NEG)

    def scores(r):
        return jnp.concatenate(
            [lax.dot_general(k_ref[pl.ds(r, L), g * LANES:(g + 1) * LANES], qs_ref[g], _NT,
                             preferred_element_type=F32) for g in range(groups)], axis=1)

    def values(j, p):
        pb = p.astype(BF16)
        return [jnp.dot(vt_ref[j, h], pb[:, h * L:(h + 1) * L], preferred_element_type=F32)
                for h in range(MOBA_HEADS)]

    kpos = lax.broadcasted_iota(jnp.int32, (L, W), 0)
    qpos = lax.broadcasted_iota(jnp.int32, (L, W), 1) % L
    s = jnp.where(kpos <= qpos, scores(row0), NEG)
    m = jnp.max(s, axis=0, keepdims=True)
    m_ref[...] = jnp.broadcast_to(m, (SUBLANES, W))
    for h, pv in enumerate(values(i, jnp.exp2(s - m))):
        acc_ref[h] = pv

    def past(j):
        s = scores(pl.multiple_of(j * L, L)) + bias_ref[pl.ds(j, 1), :]
        m_old = m_ref[...]
        m_new = jnp.maximum(m_old, jnp.max(s, axis=0, keepdims=True))
        alpha = jnp.exp2(m_old - m_new)
        m_ref[...] = m_new
        for h, pv in enumerate(values(j, jnp.exp2(s - m_new[0:1, :]))):
            acc_ref[h] = alpha[0:1, h * L:(h + 1) * L] * acc_ref[h] + pv

    def two_past(jj, c):
        past(2 * jj)
        past(2 * jj + 1)
        return c

    lax.fori_loop(0, i // 2, two_past, 0)

    @pl.when(i % 2 == 1)
    def _():
        past(i - 1)

    for h in range(MOBA_HEADS):
        o_ref[0, h * hd:(h + 1) * hd, :] = acc_ref[h, 0:hd, :] / acc_ref[h, hd:hd + 1, :]


def _moba(proj, vt, B, S):
    nb = S // MOBA_BLOCK
    L = MOBA_BLOCK
    return pl.pallas_call(
        functools.partial(_moba_kernel, nb=nb),
        grid=(B, nb),
        in_specs=[pl.BlockSpec((L, MOBA_WIDTH), lambda b, i: (b * nb + i, 0)),
                  pl.BlockSpec((S, MOBA_WIDTH), lambda b, i: (b, 1)),
                  pl.BlockSpec((nb, MOBA_HEADS, MOBA_VROWS, L), lambda b, i: (b, 0, 0, 0))],
        out_specs=pl.BlockSpec((1, MOBA_WIDTH, L), lambda b, i: (b, 0, i)),
        out_shape=jax.ShapeDtypeStruct((B, MOBA_WIDTH, S), F32),
        scratch_shapes=[pltpu.VMEM((nb, MOBA_WIDTH), F32),
                        pltpu.VMEM((nb, MOBA_HEADS * L), F32),
                        pltpu.VMEM((MOBA_HEADS // (LANES // MOBA_HEAD_DIM), (LANES // MOBA_HEAD_DIM) * L, LANES),
                                   BF16),
                        pltpu.VMEM((MOBA_HEADS, MOBA_VROWS, L), F32),
                        pltpu.VMEM((SUBLANES, MOBA_HEADS * L), F32)],
        compiler_params=_params(("parallel", "arbitrary")),
        name="moba",
    )(proj, proj, vt)


def _ret_log_g():
    return jnp.log(1.0 - 2.0 ** (-5.0 - jnp.arange(RET_HEADS, dtype=F32)))


def _retention_kernel(q_ref, k_ref, v_ref, g_ref, dec_ref, qd_ref, kd_ref, cd_ref, o_ref, state_ref):
    n = pl.program_id(1)

    @pl.when(n == 0)
    def _():
        state_ref[...] = jnp.zeros_like(state_ref)

    for h in range(RET_HEADS):
        qk = slice(h * RET_QK_DIM, (h + 1) * RET_QK_DIM)
        vv = slice(h * RET_V_DIM, (h + 1) * RET_V_DIM)
        q = q_ref[:, qk]
        k = k_ref[:, qk]
        v = v_ref[:, vv]
        scores = lax.dot_general(q, k, _NT, preferred_element_type=F32) * dec_ref[h]
        y = jnp.dot(scores.astype(BF16), v, preferred_element_type=F32)
        state = state_ref[h]
        y = y + jnp.dot(q, state.astype(BF16), preferred_element_type=F32) * qd_ref[h]
        kt = (k.astype(F32) * kd_ref[h]).T.astype(BF16)
        kv = jnp.dot(kt, v, preferred_element_type=F32)
        state_ref[h] = state * cd_ref[h:h + 1, :] + kv
        mu = jnp.mean(y, axis=-1, keepdims=True)
        yc = y - mu
        var = jnp.mean(yc * yc, axis=-1, keepdims=True)
        yn = yc * lax.rsqrt(var + LN_EPS)
        gate = g_ref[:, vv].astype(F32)
        o_ref[:, vv] = (gate * jax.nn.sigmoid(gate) * yn).astype(o_ref.dtype)


def _retention(proj, B, S):
    C = RET_CHUNK
    nc = S // C
    T = B * S
    log_g = _ret_log_g()
    pos = jnp.arange(C, dtype=F32)
    diff = pos[:, None] - pos[None, :]
    scale = RET_QK_DIM ** -0.5
    decay = jnp.where(diff >= 0, jnp.exp(log_g[:, None, None] * jnp.maximum(diff, 0.0)), 0.0) * scale
    q_decay = jnp.broadcast_to(jnp.exp(log_g[:, None] * (pos + 1.0))[:, :, None], (RET_HEADS, C, RET_V_DIM))
    k_decay = jnp.broadcast_to((jnp.exp(log_g[:, None] * (C - 1.0 - pos)) * scale)[:, :, None],
                               (RET_HEADS, C, RET_QK_DIM))
    chunk_decay = jnp.broadcast_to(jnp.exp(log_g * C)[:, None], (RET_HEADS, RET_V_DIM))
    const = lambda shape: pl.BlockSpec(shape, lambda b, n: (0,) * len(shape))
    return pl.pallas_call(
        _retention_kernel,
        grid=(B, nc),
        in_specs=[pl.BlockSpec((C, RET_QK_WIDTH), lambda b, n: (b * nc + n, 2)),
                  pl.BlockSpec((C, RET_QK_WIDTH), lambda b, n: (b * nc + n, 3)),
                  pl.BlockSpec((C, RET_V_WIDTH), lambda b, n: (b * nc + n, 2)),
                  pl.BlockSpec((C, RET_V_WIDTH), lambda b, n: (b * nc + n, 3)),
                  const((RET_HEADS, C, C)), const((RET_HEADS, C, RET_V_DIM)),
                  const((RET_HEADS, C, RET_QK_DIM)), const((RET_HEADS, RET_V_DIM))],
        out_specs=pl.BlockSpec((C, RET_V_WIDTH), lambda b, n: (b * nc + n, 0)),
        out_shape=jax.ShapeDtypeStruct((T, RET_V_WIDTH), BF16),
        scratch_shapes=[pltpu.VMEM((RET_HEADS, RET_QK_DIM, RET_V_DIM), F32)],
        compiler_params=_params(("parallel", "arbitrary")),
        name="retention",
    )(proj, proj, proj, proj, decay, q_decay, k_decay, chunk_decay)


def _layer_norm(y, g, b):
    mu = jnp.mean(y, axis=-1, keepdims=True)
    yc = y - mu
    var = jnp.mean(yc * yc, axis=-1, keepdims=True)
    return yc * lax.rsqrt(var + LN_EPS) * g + b


def _merge_kernel(yat_ref, yr_ref, ga_ref, gr_ref, x_ref, wa_ref, wr_ref, wo_ref, g_ref, b_ref, o_ref, ob_ref):
    ya = yat_ref[0].T.astype(BF16)
    branch_a = jnp.dot(ya, wa_ref[...], preferred_element_type=F32)
    branch_r = jnp.dot(yr_ref[...], wr_ref[...], preferred_element_type=F32)
    merged = (jax.nn.sigmoid(ga_ref[...].astype(F32)) * branch_a
              + jax.nn.sigmoid(gr_ref[...].astype(F32)) * branch_r)
    mix = jnp.dot(merged.astype(BF16), wo_ref[...], preferred_element_type=F32)
    y = _layer_norm(DN_ALPHA * x_ref[...] + mix, g_ref[...], b_ref[...])
    o_ref[...] = y
    ob_ref[...] = y.astype(BF16)


def _merge(yat, yr, proj, x, wa, wr, wo, g, b, B, S):
    L = MOBA_BLOCK
    nb = S // L
    T = B * S
    tok = lambda c: pl.BlockSpec((L, D_MODEL), lambda bb, i: (bb * nb + i, c))
    const = lambda shape: pl.BlockSpec(shape, lambda bb, i: (0,) * len(shape))
    return pl.pallas_call(
        _merge_kernel,
        grid=(B, nb),
        in_specs=[pl.BlockSpec((1, MOBA_WIDTH, L), lambda bb, i: (bb, 0, i)),
                  tok(0), tok(4), tok(5), tok(0),
                  const((MOBA_WIDTH, D_MODEL)), const((RET_V_WIDTH, D_MODEL)), const((D_MODEL, D_MODEL)),
                  const((1, D_MODEL)), const((1, D_MODEL))],
        out_specs=[tok(0), tok(0)],
        out_shape=[jax.ShapeDtypeStruct((T, D_MODEL), F32), jax.ShapeDtypeStruct((T, D_MODEL), BF16)],
        compiler_params=_params(("parallel", "parallel")),
        name="merge_ln1",
    )(yat, yr, proj, proj, x, wa, wr, wo, g, b)


def _cmpx(xs, a, b):
    hi = jnp.maximum(xs[a], xs[b])
    lo = jnp.minimum(xs[a], xs[b])
    xs[a], xs[b] = hi, lo


def _bitonic_merge_desc(xs):
    n = len(xs)
    j = n // 2
    while j >= 1:
        for a in range(n):
            b = a ^ j
            if b > a:
                _cmpx(xs, a, b)
        j //= 2
    return xs


def _sort_desc(xs):
    xs = list(xs)
    n = len(xs)
    k = 2
    while k <= n:
        j = k // 2
        while j >= 1:
            for a in range(n):
                b = a ^ j
                if b > a:
                    if (a & k) == 0:
                        _cmpx(xs, a, b)
                    else:
                        _cmpx(xs, b, a)
            j //= 2
        k *= 2
    return xs


def _merge_top(xs, ys):
    n = len(xs)
    return _bitonic_merge_desc([jnp.maximum(xs[v], ys[n - 1 - v]) for v in range(n)])


def _across_sublanes(xs):
    for shift in (4, 2, 1):
        xs = _merge_top(xs, [pltpu.roll(x, shift, axis=0) for x in xs])
    return xs


def _top16_desc(s):
    rows = [s[SUBLANES * v:SUBLANES * (v + 1), :] for v in range(s.shape[0] // SUBLANES)]
    return _across_sublanes(_sort_desc(rows))


def _min_over_sublanes(x):
    for shift in (4, 2, 1):
        x = jnp.minimum(x, pltpu.roll(x, shift, axis=0))
    return x


def _route_kernel(x_ref, wq_ref, sk_ref, r1_ref, e1_ref, n0_ref, c0_ref):
    q = jnp.dot(x_ref[...], wq_ref[...], preferred_element_type=F32)
    tm = q.shape[0]
    K = PEER_TOPK
    half = PEER_KEY_DIM // 2
    sub = lax.broadcasted_iota(jnp.int32, (SUBLANES, tm), 0)
    inf = jnp.inf
    for h in range(PEER_HEADS):
        q0 = q[:, (2 * h) * half:(2 * h + 1) * half].astype(BF16)
        q1 = q[:, (2 * h + 1) * half:(2 * h + 2) * half].astype(BF16)
        s0 = lax.dot_general(sk_ref[0], q0, _NT, preferred_element_type=F32)
        s1 = lax.dot_general(sk_ref[1], q1, _NT, preferred_element_type=F32)
        a = _top16_desc(s0)
        b = _top16_desc(s1)
        a_lo, a_hi = a[0], a[SUBLANES]
        for r in range(1, SUBLANES):
            a_lo = jnp.where(sub == r, a[r], a_lo)
            a_hi = jnp.where(sub == r, a[SUBLANES + r], a_hi)
        x_lo = [a_lo + b[c] for c in range(K)]
        x_hi = [a_hi + b[c] for c in range(K)]
        z = _across_sublanes(_merge_top(x_lo, x_hi))
        tau = z[K - 1]
        zsum = jnp.exp(z[0] - z[0])
        for v in range(1, K):
            zsum = zsum + jnp.exp(z[v] - z[0])
        inv_z = (1.0 / zsum)[0:1, :]
        count = jnp.zeros(s0.shape, F32)
        for c in range(K):
            alpha = jnp.minimum(jnp.where(x_lo[c] >= tau, a_lo, inf), jnp.where(x_hi[c] >= tau, a_hi, inf))
            alpha = _min_over_sublanes(alpha)[0:1, :]
            count = jnp.where(s0 >= alpha, float(c + 1), count)
        rank = jnp.full(s1.shape, float(K), F32)
        for c in reversed(range(K)):
            rank = jnp.where(s1 >= b[c][0:1, :], float(c), rank)
        r1_ref[h] = rank.astype(BF16)
        e1_ref[h] = jnp.exp(s1 - b[0][0:1, :]).astype(BF16)
        n0_ref[h] = count
        c0_ref[h] = jnp.exp(s0 - a[0][0:1, :]) * inv_z


def _route(xb, wq, sk):
    T = xb.shape[0]
    tm = ROUTE_TM
    spec = pl.BlockSpec((PEER_HEADS, PEER_N_KEYS, tm), lambda t: (0, 0, t))
    half = jax.ShapeDtypeStruct((PEER_HEADS, PEER_N_KEYS, T), BF16)
    word = jax.ShapeDtypeStruct((PEER_HEADS, PEER_N_KEYS, T), F32)
    return pl.pallas_call(
        _route_kernel,
        grid=(T // tm,),
        in_specs=[pl.BlockSpec((tm, D_MODEL), lambda t: (t, 0)),
                  pl.BlockSpec(wq.shape, lambda t: (0, 0)),
                  pl.BlockSpec(sk.shape, lambda t: (0, 0, 0))],
        out_specs=[spec, spec, spec, spec],
        out_shape=[half, half, word, word],
        compiler_params=_params(("parallel",)),
        name="peer_route",
    )(xb, wq, sk)


def _gelu_tanh(x):
    k = -2.0 * math.sqrt(2.0 / math.pi) * math.log2(math.e)
    z = x * (k + (k * 0.044715) * (x * x))
    return x / (1.0 + jnp.exp2(z))


def _peer_kernel(xb_ref, u_ref, vt_ref, r1_ref, e1_ref, n0_ref, c0_ref, x_ref, g_ref, b_ref,
                 o_ref, ob_ref, w_ref, act_ref, acc_ref, *, rows):
    e = pl.program_id(1)
    nk = PEER_N_KEYS
    tm = xb_ref.shape[0]
    pk = 2 * SUBLANES

    @pl.when(e == 0)
    def _():
        acc_ref[...] = jnp.zeros_like(acc_ref)

    tc = min(PEER_LANE_CHUNK, tm)

    def routing_weights(r):
        i = e * rows + r
        first = None
        for ch in range(tm // tc):
            ls = slice(ch * tc, (ch + 1) * tc)
            w = [None] * (nk // pk)
            for h in range(PEER_HEADS):
                count = jnp.broadcast_to(n0_ref[h, pl.ds(i, 1), ls], (pk, tc)).astype(BF16)
                weight = jnp.broadcast_to(c0_ref[h, pl.ds(i, 1), ls], (pk, tc)).astype(BF16)
                for g in range(nk // pk):
                    rs = slice(g * pk, (g + 1) * pk)
                    sel = jnp.where(r1_ref[h, rs, ls] < count, e1_ref[h, rs, ls], jnp.zeros((pk, tc), BF16))
                    term = sel * weight
                    w[g] = term if w[g] is None else w[g] + term
            for g in range(nk // pk):
                w_ref[r * nk + g * pk:r * nk + (g + 1) * pk, ls] = w[g]
            if first is None:
                first = pltpu.bitcast(w[0][:, 0:LANES], jnp.uint32)
        return first

    cr = PEER_CHUNK_ROWS
    zero = 0
    for c in range(rows // cr):
        bits = routing_weights(c * cr)
        for r in range(c * cr + 1, (c + 1) * cr):
            routing_weights(r)
        lo = 0 if c == 0 else pl.multiple_of(c * cr * nk + zero, cr * nk)
        ht = lax.dot_general(u_ref[pl.ds(lo, cr * nk), :], xb_ref[...], _NT,
                             preferred_element_type=F32)
        rows_c = slice(c * cr * nk, (c + 1) * cr * nk)
        act_ref[rows_c, :] = _gelu_tanh(ht.astype(BF16)) * w_ref[rows_c, :]
        zero = ((bits >> 16) >> 16)[0, 0].astype(jnp.int32)
    acc_ref[...] += jnp.dot(vt_ref[...], act_ref[...], preferred_element_type=F32)

    @pl.when(e == pl.num_programs(1) - 1)
    def _():
        y = _layer_norm(DN_ALPHA * x_ref[...] + acc_ref[...].T, g_ref[...], b_ref[...])
        o_ref[...] = y
        ob_ref[...] = y.astype(BF16)


def _peer(xb, x, u, vt, routing, g, b):
    T = xb.shape[0]
    tm = min(PEER_TM, T)
    rows = PEER_ROWS
    nk = PEER_N_KEYS
    rspec = pl.BlockSpec((PEER_HEADS, nk, tm), lambda t, e: (0, 0, t))
    tok = pl.BlockSpec((tm, D_MODEL), lambda t, e: (t, 0))
    const = pl.BlockSpec((1, D_MODEL), lambda t, e: (0, 0))
    return pl.pallas_call(
        functools.partial(_peer_kernel, rows=rows),
        grid=(T // tm, nk // rows),
        in_specs=[tok,
                  pl.BlockSpec((rows * nk, D_MODEL), lambda t, e: (e, 0)),
                  pl.BlockSpec((D_MODEL, rows * nk), lambda t, e: (0, e)),
                  rspec, rspec, rspec, rspec, tok, const, const],
        out_specs=[tok, tok],
        out_shape=[jax.ShapeDtypeStruct((T, D_MODEL), F32), jax.ShapeDtypeStruct((T, D_MODEL), BF16)],
        scratch_shapes=[pltpu.VMEM((rows * nk, tm), BF16),
                        pltpu.VMEM((rows * nk, tm), BF16),
                        pltpu.VMEM((D_MODEL, tm), F32)],
        compiler_params=_params(("parallel", "arbitrary"), PEER_SCHED_FLAGS),
        name="peer_dense",
    )(xb, u, vt, *routing, x, g, b)


def _rope_tables(S):
    pos = jnp.arange(S, dtype=F32)
    d = MOBA_HEAD_DIM
    inv = ROPE_THETA ** (-jnp.arange(0, d, 2, dtype=F32) / d)
    ang = pos[:, None] * inv[None, :]
    cos, sin = jnp.cos(ang), jnp.sin(ang)
    reps = LANES // d
    cos_m = jnp.tile(jnp.concatenate([cos, cos], axis=1), (1, reps))
    sin_m = jnp.tile(jnp.concatenate([-sin, sin], axis=1), (1, reps))
    d = RET_QK_DIM
    inv = 1.0 / (ROPE_THETA ** jnp.linspace(0.0, 1.0, d // 2, dtype=F32))
    ang = pos[:, None] * inv[None, :]
    cos, sin = jnp.cos(ang), jnp.sin(ang)
    cos_r = jnp.concatenate([cos, cos], axis=1)
    sin_r = jnp.concatenate([-sin, sin], axis=1)
    return cos_m, sin_m, cos_r, sin_r


def _ret_column_perm():
    within = np.concatenate([np.arange(0, RET_QK_DIM, 2), np.arange(1, RET_QK_DIM, 2)])
    return np.concatenate([h * RET_QK_DIM + within for h in range(RET_HEADS)])


def kernel(x, w_in, w_moba_out, w_ret_out, w_out, ln1_g, ln1_b, peer_w_query, peer_sub_keys,
           peer_u, peer_v, ln2_g, ln2_b):
    B, S, D = x.shape
    assert D == D_MODEL and S % MOBA_BLOCK == 0 and S % RET_CHUNK == 0
    T = B * S
    tabs = _rope_tables(S)
    perm = _ret_column_perm()
    o = IN_OFFSETS
    xf = x.reshape(T, D).astype(F32)
    xb = xf.astype(BF16)
    for l in range(DEPTH):
        w = w_in[l]
        w_main = jnp.concatenate(
            [w[:, o[0]:o[2]], w[:, o[3]:o[4]][:, perm], w[:, o[4]:o[5]][:, perm], w[:, o[5]:o[9]]],
            axis=1).astype(BF16)
        w_vt = w[:, o[2]:o[3]].T.astype(BF16)
        proj = _inproj(xb, w_main, tabs, S)
        vt = _moba_values_t(xb, w_vt, S)
        yat = _moba(proj, vt, B, S)
        yr = _retention(proj, B, S)
        xf, xb = _merge(yat, yr, proj, xf, w_moba_out[l].astype(BF16), w_ret_out[l].astype(BF16),
                        w_out[l].astype(BF16), ln1_g[l].reshape(1, D), ln1_b[l].reshape(1, D), B, S)
        routing = _route(xb, peer_w_query[l].astype(BF16), peer_sub_keys[l].astype(BF16))
        xf, xb = _peer(xb, xf, peer_u[l].astype(BF16), peer_v[l].T.astype(BF16), routing,
                       ln2_g[l].reshape(1, D), ln2_b[l].reshape(1, D))
    return xf.reshape(B, S, D).astype(x.dtype)
```

```python
import functools
import math

import numpy as np
import jax
import jax.numpy as jnp
from jax import lax
from jax.experimental import pallas as pl
from jax.experimental.pallas import tpu as pltpu

F32 = jnp.float32
BF16 = jnp.bfloat16

D_MODEL = 1024
DEPTH = 2
MOBA_HEADS = 8
MOBA_HEAD_DIM = 64
MOBA_WIDTH = MOBA_HEADS * MOBA_HEAD_DIM
MOBA_BLOCK = 256
MOBA_TOPK = 3
ROPE_THETA = 10000.0
RET_HEADS = 4
RET_QK_DIM = 128
RET_V_DIM = 256
RET_QK_WIDTH = RET_HEADS * RET_QK_DIM
RET_V_WIDTH = RET_HEADS * RET_V_DIM
RET_CHUNK = 256
PEER_N_KEYS = 128
PEER_N_EXPERTS = PEER_N_KEYS * PEER_N_KEYS
PEER_HEADS = 8
PEER_KEY_DIM = 256
PEER_TOPK = 16
DN_ALPHA = (2.0 * DEPTH) ** 0.25
LN_EPS = 1e-5
NEG = -1e30

IN_SIZES = (MOBA_WIDTH, MOBA_WIDTH, MOBA_WIDTH, RET_QK_WIDTH, RET_QK_WIDTH,
            RET_V_WIDTH, RET_V_WIDTH, D_MODEL, D_MODEL)
IN_OFFSETS = tuple(int(v) for v in np.concatenate([[0], np.cumsum(IN_SIZES)]))

LANES = 128
SUBLANES = 8
VMEM_LIMIT = 56 * 1024 * 1024

PROJ_TM = 1024
PROJ_TN = 512
ROUTE_TM = 256
PEER_TM = 512
PEER_ROWS = 16
PEER_LANE_CHUNK = 256
PEER_CHUNK_ROWS = 2
PEER_SCHED_FLAGS = None

_NT = (((1,), (1,)), ((), ()))


def _params(sem, flags=None):
    return pltpu.CompilerParams(dimension_semantics=sem, vmem_limit_bytes=VMEM_LIMIT, flags=flags)


def _rotate_groups(acc, cos, sin, o_ref, partner_fn):
    for g in range(PROJ_TN // LANES):
        xg = acc[:, g * LANES:(g + 1) * LANES]
        o_ref[:, g * LANES:(g + 1) * LANES] = (xg * cos + partner_fn(xg) * sin).astype(o_ref.dtype)


def _inproj_kernel(x_ref, w_ref, cm_ref, sm_ref, cr_ref, sr_ref, o_ref):
    j = pl.program_id(1)
    acc = jnp.dot(x_ref[...], w_ref[...], preferred_element_type=F32)
    tm = acc.shape[0]

    @pl.when(j < 2)
    def _():
        lane = lax.broadcasted_iota(jnp.int32, (tm, LANES), 1)
        first = (lane % MOBA_HEAD_DIM) < (MOBA_HEAD_DIM // 2)

        def partner(xg):
            return jnp.where(first, pltpu.roll(xg, LANES - MOBA_HEAD_DIM // 2, axis=1),
                             pltpu.roll(xg, MOBA_HEAD_DIM // 2, axis=1))

        _rotate_groups(acc, cm_ref[...], sm_ref[...], o_ref, partner)

    @pl.when((j >= 2) & (j < 4))
    def _():
        _rotate_groups(acc, cr_ref[...], sr_ref[...], o_ref,
                       lambda xg: pltpu.roll(xg, RET_QK_DIM // 2, axis=1))

    @pl.when(j >= 4)
    def _():
        o_ref[...] = acc.astype(o_ref.dtype)


def _inproj(xb, w_main, tabs, S):
    T = xb.shape[0]
    tm = min(PROJ_TM, S)
    n_col = w_main.shape[1] // PROJ_TN
    pos_blocks = S // tm
    tab_spec = pl.BlockSpec((tm, LANES), lambda i, j: (i % pos_blocks, 0))
    return pl.pallas_call(
        _inproj_kernel,
        grid=(T // tm, n_col),
        in_specs=[pl.BlockSpec((tm, D_MODEL), lambda i, j: (i, 0)),
                  pl.BlockSpec((D_MODEL, PROJ_TN), lambda i, j: (0, j)),
                  tab_spec, tab_spec, tab_spec, tab_spec],
        out_specs=pl.BlockSpec((tm, PROJ_TN), lambda i, j: (i, j)),
        out_shape=jax.ShapeDtypeStruct((T, w_main.shape[1]), BF16),
        compiler_params=_params(("parallel", "arbitrary")),
        name="inproj",
    )(xb, w_main, *tabs)


MOBA_VROWS = MOBA_HEAD_DIM + 2 * SUBLANES


def _vt_kernel(x_ref, w_ref, o_ref):
    res = lax.dot_general(w_ref[...], x_ref[...], _NT, preferred_element_type=F32)
    hd = MOBA_HEAD_DIM
    pad = MOBA_VROWS - hd
    ones_row = jnp.where(lax.broadcasted_iota(jnp.int32, (pad, MOBA_BLOCK), 0) == 0, 1.0, 0.0).astype(o_ref.dtype)
    for c in range(o_ref.shape[0]):
        for h in range(MOBA_HEADS):
            o_ref[c, h, 0:hd, :] = res[h * hd:(h + 1) * hd, c * MOBA_BLOCK:(c + 1) * MOBA_BLOCK].astype(o_ref.dtype)
            o_ref[c, h, hd:MOBA_VROWS, :] = ones_row


def _moba_values_t(xb, w_vt, S):
    T = xb.shape[0]
    tm = min(PROJ_TM, S)
    per = tm // MOBA_BLOCK
    return pl.pallas_call(
        _vt_kernel,
        grid=(T // tm,),
        in_specs=[pl.BlockSpec((tm, D_MODEL), lambda i: (i, 0)),
                  pl.BlockSpec((MOBA_WIDTH, D_MODEL), lambda i: (0, 0))],
        out_specs=pl.BlockSpec((per, MOBA_HEADS, MOBA_VROWS, MOBA_BLOCK), lambda i: (i, 0, 0, 0)),
        out_shape=jax.ShapeDtypeStruct((T // MOBA_BLOCK, MOBA_HEADS, MOBA_VROWS, MOBA_BLOCK), BF16),
        compiler_params=_params(("parallel",)),
        name="moba_vt",
    )(xb, w_vt)


def _moba_kernel(q_ref, k_ref, vt_ref, o_ref, kmean_ref, bias_ref, qs_ref, acc_ref, m_ref, sa_ref, sb_ref, *, nb):
    i = pl.program_id(1)
    L = MOBA_BLOCK
    hd = MOBA_HEAD_DIM

    @pl.when(i == 0)
    def _():
        def body(j, c):
            kb = k_ref[pl.ds(pl.multiple_of(j * L, L), L), :].astype(F32)
            kmean_ref[pl.ds(j, 1), :] = jnp.sum(kb, axis=0, keepdims=True) * (1.0 / L)
            return c
        lax.fori_loop(0, nb, body, 0)

    per = LANES // hd
    groups = MOBA_HEADS // per
    W = MOBA_HEADS * L
    lane = lax.broadcasted_iota(jnp.int32, (L, LANES), 1)
    row0 = pl.multiple_of(i * L, L)

    gates = []
    for g in range(groups):
        cols = slice(g * LANES, (g + 1) * LANES)
        q_pair = q_ref[:, cols]
        km = kmean_ref[:, cols].astype(BF16)
        for hh in range(per):
            qm = jnp.where((lane // hd) == hh, q_pair, jnp.zeros_like(q_pair))
            gates.append(lax.dot_general(km, qm, _NT, preferred_element_type=F32))
            qs_ref[g, hh * L:(hh + 1) * L, :] = (qm.astype(F32) * (hd ** -0.5 * math.log2(math.e))).astype(BF16)

    blk = lax.broadcasted_iota(jnp.int32, (nb, W), 0)
    blkf = blk.astype(F32)
    gm = jnp.where(blk < i, jnp.concatenate(gates, axis=1), -jnp.inf)
    keep = jnp.zeros((nb, W), F32)
    for _ in range(MOBA_TOPK):
        best = jnp.max(gm, axis=0, keepdims=True)
        first = jnp.min(jnp.where(gm == best, blkf, float(nb)), axis=0, keepdims=True)
        hit = blkf == first
        keep = jnp.where(hit, 1.0, keep)
        gm = jnp.where(hit, -jnp.inf, gm)
    bias_ref[...] = jnp.where(blk < i, jnp.where(keep > 0.0, 0.0, NEG), NEG)

    def scores(r):
        return jnp.concatenate(
            [lax.dot_general(k_ref[pl.ds(r, L), g * LANES:(g + 1) * LANES], qs_ref[g], _NT,
                             preferred_element_type=F32) for g in range(groups)], axis=1)

    def values(j, p):
        pb = p.astype(BF16)
        return [jnp.dot(vt_ref[j, h], pb[:, h * L:(h + 1) * L], preferred_element_type=F32)
                for h in range(MOBA_HEADS)]

    kpos = lax.broadcasted_iota(jnp.int32, (L, W), 0)
    qpos = lax.broadcasted_iota(jnp.int32, (L, W), 1) % L
    s = jnp.where(kpos <= qpos, scores(row0), NEG)
    m = jnp.max(s, axis=0, keepdims=True)
    m_ref[...] = jnp.broadcast_to(m, (SUBLANES, W))
    for h, pv in enumerate(values(i, jnp.exp2(s - m))):
        acc_ref[h] = pv

    def scores_into(dst_ref, r):
        first = None
        for g in range(groups):
            sg = lax.dot_general(k_ref[pl.ds(r, L), g * LANES:(g + 1) * LANES], qs_ref[g], _NT,
                                 preferred_element_type=F32)
            dst_ref[:, g * per * L:(g + 1) * per * L] = sg
            if first is None:
                first = pltpu.bitcast(sg[0:SUBLANES, 0:LANES], jnp.uint32)
        return ((first >> 16) >> 16)[0, 0].astype(jnp.int32)

    def softmax_update(j, cur_ref, heads, row_start):
        lanes = slice(heads[0] * L, (heads[-1] + 1) * L)
        s = cur_ref[pl.ds(row_start, L), lanes]
        bias = bias_ref[pl.ds(j, 1), lanes]
        m_old = m_ref[:, lanes]
        m_new = jnp.maximum(m_old, jnp.max(s, axis=0, keepdims=True) + bias)
        alpha = jnp.exp2(m_old - m_new)
        m_ref[:, lanes] = m_new
        pb = jnp.exp2(s - (m_new[0:1, :] - bias)).astype(BF16)
        for n, h in enumerate(heads):
            pv = jnp.dot(vt_ref[j, h], pb[:, n * L:(n + 1) * L], preferred_element_type=F32)
            acc_ref[h] = alpha[0:1, n * L:(n + 1) * L] * acc_ref[h] + pv

    half = MOBA_HEADS // 2

    def past(j, cur_ref, nxt_ref):
        start = 0
        if nxt_ref is not None:
            nxt = jnp.minimum(j + 1, i - 1)
            start = pl.multiple_of(scores_into(nxt_ref, pl.multiple_of(nxt * L, L)), L)
        softmax_update(j, cur_ref, tuple(range(half)), 0)
        softmax_update(j, cur_ref, tuple(range(half, MOBA_HEADS)), start)

    @pl.when(i > 0)
    def _():
        scores_into(sa_ref, 0)

    def two_past(jj, c):
        past(2 * jj, sa_ref, sb_ref)
        past(2 * jj + 1, sb_ref, sa_ref)
        return c

    lax.fori_loop(0, i // 2, two_past, 0)

    @pl.when(i % 2 == 1)
    def _():
        past(i - 1, sa_ref, None)

    for h in range(MOBA_HEADS):
        o_ref[0, h * hd:(h + 1) * hd, :] = acc_ref[h, 0:hd, :] / acc_ref[h, hd:hd + 1, :]


def _moba(proj, vt, B, S):
    nb = S // MOBA_BLOCK
    L = MOBA_BLOCK
    return pl.pallas_call(
        functools.partial(_moba_kernel, nb=nb),
        grid=(B, nb),
        in_specs=[pl.BlockSpec((L, MOBA_WIDTH), lambda b, i: (b * nb + i, 0)),
                  pl.BlockSpec((S, MOBA_WIDTH), lambda b, i: (b, 1)),
                  pl.BlockSpec((nb, MOBA_HEADS, MOBA_VROWS, L), lambda b, i: (b, 0, 0, 0))],
        out_specs=pl.BlockSpec((1, MOBA_WIDTH, L), lambda b, i: (b, 0, i)),
        out_shape=jax.ShapeDtypeStruct((B, MOBA_WIDTH, S), F32),
        scratch_shapes=[pltpu.VMEM((nb, MOBA_WIDTH), F32),
                        pltpu.VMEM((nb, MOBA_HEADS * L), F32),
                        pltpu.VMEM((MOBA_HEADS // (LANES // MOBA_HEAD_DIM), (LANES // MOBA_HEAD_DIM) * L, LANES),
                                   BF16),
                        pltpu.VMEM((MOBA_HEADS, MOBA_VROWS, L), F32),
                        pltpu.VMEM((SUBLANES, MOBA_HEADS * L), F32),
                        pltpu.VMEM((L, MOBA_HEADS * L), F32),
                        pltpu.VMEM((L, MOBA_HEADS * L), F32)],
        compiler_params=_params(("parallel", "arbitrary")),
        name="moba",
    )(proj, proj, vt)


def _ret_log_g():
    return jnp.log(1.0 - 2.0 ** (-5.0 - jnp.arange(RET_HEADS, dtype=F32)))


def _retention_kernel(q_ref, k_ref, v_ref, g_ref, dec_ref, qd_ref, kd_ref, cd_ref, o_ref, state_ref):
    n = pl.program_id(1)

    @pl.when(n == 0)
    def _():
        state_ref[...] = jnp.zeros_like(state_ref)

    for h in range(RET_HEADS):
        qk = slice(h * RET_QK_DIM, (h + 1) * RET_QK_DIM)
        vv = slice(h * RET_V_DIM, (h + 1) * RET_V_DIM)
        q = q_ref[:, qk]
        k = k_ref[:, qk]
        v = v_ref[:, vv]
        scores = lax.dot_general(q, k, _NT, preferred_element_type=F32) * dec_ref[h]
        y = jnp.dot(scores.astype(BF16), v, preferred_element_type=F32)
        state = state_ref[h]
        y = y + jnp.dot(q, state.astype(BF16), preferred_element_type=F32) * qd_ref[h]
        kt = (k.astype(F32) * kd_ref[h]).T.astype(BF16)
        kv = jnp.dot(kt, v, preferred_element_type=F32)
        state_ref[h] = state * cd_ref[h:h + 1, :] + kv
        mu = jnp.mean(y, axis=-1, keepdims=True)
        yc = y - mu
        var = jnp.mean(yc * yc, axis=-1, keepdims=True)
        yn = yc * lax.rsqrt(var + LN_EPS)
        gate = g_ref[:, vv].astype(F32)
        o_ref[:, vv] = (gate * jax.nn.sigmoid(gate) * yn).astype(o_ref.dtype)


def _retention(proj, B, S):
    C = RET_CHUNK
    nc = S // C
    T = B * S
    log_g = _ret_log_g()
    pos = jnp.arange(C, dtype=F32)
    diff = pos[:, None] - pos[None, :]
    scale = RET_QK_DIM ** -0.5
    decay = jnp.where(diff >= 0, jnp.exp(log_g[:, None, None] * jnp.maximum(diff, 0.0)), 0.0) * scale
    q_decay = jnp.broadcast_to(jnp.exp(log_g[:, None] * (pos + 1.0))[:, :, None], (RET_HEADS, C, RET_V_DIM))
    k_decay = jnp.broadcast_to((jnp.exp(log_g[:, None] * (C - 1.0 - pos)) * scale)[:, :, None],
                               (RET_HEADS, C, RET_QK_DIM))
    chunk_decay = jnp.broadcast_to(jnp.exp(log_g * C)[:, None], (RET_HEADS, RET_V_DIM))
    const = lambda shape: pl.BlockSpec(shape, lambda b, n: (0,) * len(shape))
    return pl.pallas_call(
        _retention_kernel,
        grid=(B, nc),
        in_specs=[pl.BlockSpec((C, RET_QK_WIDTH), lambda b, n: (b * nc + n, 2)),
                  pl.BlockSpec((C, RET_QK_WIDTH), lambda b, n: (b * nc + n, 3)),
                  pl.BlockSpec((C, RET_V_WIDTH), lambda b, n: (b * nc + n, 2)),
                  pl.BlockSpec((C, RET_V_WIDTH), lambda b, n: (b * nc + n, 3)),
                  const((RET_HEADS, C, C)), const((RET_HEADS, C, RET_V_DIM)),
                  const((RET_HEADS, C, RET_QK_DIM)), const((RET_HEADS, RET_V_DIM))],
        out_specs=pl.BlockSpec((C, RET_V_WIDTH), lambda b, n: (b * nc + n, 0)),
        out_shape=jax.ShapeDtypeStruct((T, RET_V_WIDTH), BF16),
        scratch_shapes=[pltpu.VMEM((RET_HEADS, RET_QK_DIM, RET_V_DIM), F32)],
        compiler_params=_params(("parallel", "arbitrary")),
        name="retention",
    )(proj, proj, proj, proj, decay, q_decay, k_decay, chunk_decay)


def _layer_norm(y, g, b):
    mu = jnp.mean(y, axis=-1, keepdims=True)
    yc = y - mu
    var = jnp.mean(yc * yc, axis=-1, keepdims=True)
    return yc * lax.rsqrt(var + LN_EPS) * g + b


def _merge_kernel(yat_ref, yr_ref, ga_ref, gr_ref, x_ref, wa_ref, wr_ref, wo_ref, g_ref, b_ref, o_ref, ob_ref):
    ya = yat_ref[0].T.astype(BF16)
    branch_a = jnp.dot(ya, wa_ref[...], preferred_element_type=F32)
    branch_r = jnp.dot(yr_ref[...], wr_ref[...], preferred_element_type=F32)
    merged = (jax.nn.sigmoid(ga_ref[...].astype(F32)) * branch_a
              + jax.nn.sigmoid(gr_ref[...].astype(F32)) * branch_r)
    mix = jnp.dot(merged.astype(BF16), wo_ref[...], preferred_element_type=F32)
    y = _layer_norm(DN_ALPHA * x_ref[...] + mix, g_ref[...], b_ref[...])
    o_ref[...] = y
    ob_ref[...] = y.astype(BF16)


def _merge(yat, yr, proj, x, wa, wr, wo, g, b, B, S):
    L = MOBA_BLOCK
    nb = S // L
    T = B * S
    tok = lambda c: pl.BlockSpec((L, D_MODEL), lambda bb, i: (bb * nb + i, c))
    const = lambda shape: pl.BlockSpec(shape, lambda bb, i: (0,) * len(shape))
    return pl.pallas_call(
        _merge_kernel,
        grid=(B, nb),
        in_specs=[pl.BlockSpec((1, MOBA_WIDTH, L), lambda bb, i: (bb, 0, i)),
                  tok(0), tok(4), tok(5), tok(0),
                  const((MOBA_WIDTH, D_MODEL)), const((RET_V_WIDTH, D_MODEL)), const((D_MODEL, D_MODEL)),
                  const((1, D_MODEL)), const((1, D_MODEL))],
        out_specs=[tok(0), tok(0)],
        out_shape=[jax.ShapeDtypeStruct((T, D_MODEL), F32), jax.ShapeDtypeStruct((T, D_MODEL), BF16)],
        compiler_params=_params(("parallel", "parallel")),
        name="merge_ln1",
    )(yat, yr, proj, proj, x, wa, wr, wo, g, b)


def _cmpx(xs, a, b):
    hi = jnp.maximum(xs[a], xs[b])
    lo = jnp.minimum(xs[a], xs[b])
    xs[a], xs[b] = hi, lo


def _bitonic_merge_desc(xs):
    n = len(xs)
    j = n // 2
    while j >= 1:
        for a in range(n):
            b = a ^ j
            if b > a:
                _cmpx(xs, a, b)
        j //= 2
    return xs


def _sort_desc(xs):
    xs = list(xs)
    n = len(xs)
    k = 2
    while k <= n:
        j = k // 2
        while j >= 1:
            for a in range(n):
                b = a ^ j
                if b > a:
                    if (a & k) == 0:
                        _cmpx(xs, a, b)
                    else:
                        _cmpx(xs, b, a)
            j //= 2
        k *= 2
    return xs


def _merge_top(xs, ys):
    n = len(xs)
    return _bitonic_merge_desc([jnp.maximum(xs[v], ys[n - 1 - v]) for v in range(n)])


def _across_sublanes(xs):
    for shift in (4, 2, 1):
        xs = _merge_top(xs, [pltpu.roll(x, shift, axis=0) for x in xs])
    return xs


def _top16_desc(s):
    rows = [s[SUBLANES * v:SUBLANES * (v + 1), :] for v in range(s.shape[0] // SUBLANES)]
    return _across_sublanes(_sort_desc(rows))


def _min_over_sublanes(x):
    for shift in (4, 2, 1):
        x = jnp.minimum(x, pltpu.roll(x, shift, axis=0))
    return x


def _route_kernel(x_ref, wq_ref, sk_ref, r1_ref, e1_ref, n0_ref, c0_ref):
    q = jnp.dot(x_ref[...], wq_ref[...], preferred_element_type=F32)
    tm = q.shape[0]
    K = PEER_TOPK
    half = PEER_KEY_DIM // 2
    sub = lax.broadcasted_iota(jnp.int32, (SUBLANES, tm), 0)
    inf = jnp.inf
    for h in range(PEER_HEADS):
        q0 = q[:, (2 * h) * half:(2 * h + 1) * half].astype(BF16)
        q1 = q[:, (2 * h + 1) * half:(2 * h + 2) * half].astype(BF16)
        s0 = lax.dot_general(sk_ref[0], q0, _NT, preferred_element_type=F32)
        s1 = lax.dot_general(sk_ref[1], q1, _NT, preferred_element_type=F32)
        a = _top16_desc(s0)
        b = _top16_desc(s1)
        a_lo, a_hi = a[0], a[SUBLANES]
        for r in range(1, SUBLANES):
            a_lo = jnp.where(sub == r, a[r], a_lo)
            a_hi = jnp.where(sub == r, a[SUBLANES + r], a_hi)
        x_lo = [a_lo + b[c] for c in range(K)]
        x_hi = [a_hi + b[c] for c in range(K)]
        z = _across_sublanes(_merge_top(x_lo, x_hi))
        tau = z[K - 1]
        zsum = jnp.exp(z[0] - z[0])
        for v in range(1, K):
            zsum = zsum + jnp.exp(z[v] - z[0])
        inv_z = (1.0 / zsum)[0:1, :]
        count = jnp.zeros(s0.shape, F32)
        for c in range(K):
            alpha = jnp.minimum(jnp.where(x_lo[c] >= tau, a_lo, inf), jnp.where(x_hi[c] >= tau, a_hi, inf))
            alpha = _min_over_sublanes(alpha)[0:1, :]
            count = jnp.where(s0 >= alpha, float(c + 1), count)
        rank = jnp.full(s1.shape, float(K), F32)
        for c in reversed(range(K)):
            rank = jnp.where(s1 >= b[c][0:1, :], float(c), rank)
        r1_ref[h] = rank.astype(BF16)
        e1_ref[h] = jnp.exp(s1 - b[0][0:1, :]).astype(BF16)
        n0_ref[h] = count
        c0_ref[h] = jnp.exp(s0 - a[0][0:1, :]) * inv_z


def _route(xb, wq, sk):
    T = xb.shape[0]
    tm = ROUTE_TM
    spec = pl.BlockSpec((PEER_HEADS, PEER_N_KEYS, tm), lambda t: (0, 0, t))
    half = jax.ShapeDtypeStruct((PEER_HEADS, PEER_N_KEYS, T), BF16)
    word = jax.ShapeDtypeStruct((PEER_HEADS, PEER_N_KEYS, T), F32)
    return pl.pallas_call(
        _route_kernel,
        grid=(T // tm,),
        in_specs=[pl.BlockSpec((tm, D_MODEL), lambda t: (t, 0)),
                  pl.BlockSpec(wq.shape, lambda t: (0, 0)),
                  pl.BlockSpec(sk.shape, lambda t: (0, 0, 0))],
        out_specs=[spec, spec, spec, spec],
        out_shape=[half, half, word, word],
        compiler_params=_params(("parallel",)),
        name="peer_route",
    )(xb, wq, sk)


def _gelu_tanh(x):
    k = -2.0 * math.sqrt(2.0 / math.pi) * math.log2(math.e)
    z = x * (k + (k * 0.044715) * (x * x))
    return x / (1.0 + jnp.exp2(z))


def _peer_kernel(xb_ref, u_ref, vt_ref, r1_ref, e1_ref, n0_ref, c0_ref, x_ref, g_ref, b_ref,
                 o_ref, ob_ref, w_ref, act_ref, acc_ref, *, rows):
    e = pl.program_id(1)
    nk = PEER_N_KEYS
    tm = xb_ref.shape[0]
    pk = 2 * SUBLANES

    @pl.when(e == 0)
    def _():
        acc_ref[...] = jnp.zeros_like(acc_ref)

    tc = min(PEER_LANE_CHUNK, tm)

    def routing_weights(r):
        i = e * rows + r
        first = None
        for ch in range(tm // tc):
            ls = slice(ch * tc, (ch + 1) * tc)
            w = [None] * (nk // pk)
            for h in range(PEER_HEADS):
                count = jnp.broadcast_to(n0_ref[h, pl.ds(i, 1), ls], (pk, tc)).astype(BF16)
                weight = jnp.broadcast_to(c0_ref[h, pl.ds(i, 1), ls], (pk, tc)).astype(BF16)
                for g in range(nk // pk):
                    rs = slice(g * pk, (g + 1) * pk)
                    sel = jnp.where(r1_ref[h, rs, ls] < count, e1_ref[h, rs, ls], jnp.zeros((pk, tc), BF16))
                    term = sel * weight
                    w[g] = term if w[g] is None else w[g] + term
            for g in range(nk // pk):
                w_ref[r * nk + g * pk:r * nk + (g + 1) * pk, ls] = w[g]
            if first is None:
                first = pltpu.bitcast(w[0][:, 0:LANES], jnp.uint32)
        return first

    cr = PEER_CHUNK_ROWS
    zero = 0
    for c in range(rows // cr):
        bits = routing_weights(c * cr)
        for r in range(c * cr + 1, (c + 1) * cr):
            routing_weights(r)
        lo = 0 if c == 0 else pl.multiple_of(c * cr * nk + zero, cr * nk)
        ht = lax.dot_general(u_ref[pl.ds(lo, cr * nk), :], xb_ref[...], _NT,
                             preferred_element_type=F32)
        rows_c = slice(c * cr * nk, (c + 1) * cr * nk)
        act_ref[rows_c, :] = _gelu_tanh(ht.astype(BF16)) * w_ref[rows_c, :]
        zero = ((bits >> 16) >> 16)[0, 0].astype(jnp.int32)
    acc_ref[...] += jnp.dot(vt_ref[...], act_ref[...], preferred_element_type=F32)

    @pl.when(e == pl.num_programs(1) - 1)
    def _():
        y = _layer_norm(DN_ALPHA * x_ref[...] + acc_ref[...].T, g_ref[...], b_ref[...])
        o_ref[...] = y
        ob_ref[...] = y.astype(BF16)


def _peer(xb, x, u, vt, routing, g, b):
    T = xb.shape[0]
    tm = min(PEER_TM, T)
    rows = PEER_ROWS
    nk = PEER_N_KEYS
    rspec = pl.BlockSpec((PEER_HEADS, nk, tm), lambda t, e: (0, 0, t))
    tok = pl.BlockSpec((tm, D_MODEL), lambda t, e: (t, 0))
    const = pl.BlockSpec((1, D_MODEL), lambda t, e: (0, 0))
    return pl.pallas_call(
        functools.partial(_peer_kernel, rows=rows),
        grid=(T // tm, nk // rows),
        in_specs=[tok,
                  pl.BlockSpec((rows * nk, D_MODEL), lambda t, e: (e, 0)),
                  pl.BlockSpec((D_MODEL, rows * nk), lambda t, e: (0, e)),
                  rspec, rspec, rspec, rspec, tok, const, const],
        out_specs=[tok, tok],
        out_shape=[jax.ShapeDtypeStruct((T, D_MODEL), F32), jax.ShapeDtypeStruct((T, D_MODEL), BF16)],
        scratch_shapes=[pltpu.VMEM((rows * nk, tm), BF16),
                        pltpu.VMEM((rows * nk, tm), BF16),
                        pltpu.VMEM((D_MODEL, tm), F32)],
        compiler_params=_params(("parallel", "arbitrary"), PEER_SCHED_FLAGS),
        name="peer_dense",
    )(xb, u, vt, *routing, x, g, b)


def _rope_tables(S):
    pos = jnp.arange(S, dtype=F32)
    d = MOBA_HEAD_DIM
    inv = ROPE_THETA ** (-jnp.arange(0, d, 2, dtype=F32) / d)
    ang = pos[:, None] * inv[None, :]
    cos, sin = jnp.cos(ang), jnp.sin(ang)
    reps = LANES // d
    cos_m = jnp.tile(jnp.concatenate([cos, cos], axis=1), (1, reps))
    sin_m = jnp.tile(jnp.concatenate([-sin, sin], axis=1), (1, reps))
    d = RET_QK_DIM
    inv = 1.0 / (ROPE_THETA ** jnp.linspace(0.0, 1.0, d // 2, dtype=F32))
    ang = pos[:, None] * inv[None, :]
    cos, sin = jnp.cos(ang), jnp.sin(ang)
    cos_r = jnp.concatenate([cos, cos], axis=1)
    sin_r = jnp.concatenate([-sin, sin], axis=1)
    return cos_m, sin_m, cos_r, sin_r


def _ret_column_perm():
    within = np.concatenate([np.arange(0, RET_QK_DIM, 2), np.arange(1, RET_QK_DIM, 2)])
    return np.concatenate([h * RET_QK_DIM + within for h in range(RET_HEADS)])


def kernel(x, w_in, w_moba_out, w_ret_out, w_out, ln1_g, ln1_b, peer_w_query, peer_sub_keys,
           peer_u, peer_v, ln2_g, ln2_b):
    B, S, D = x.shape
    assert D == D_MODEL and S % MOBA_BLOCK == 0 and S % RET_CHUNK == 0
    T = B * S
    tabs = _rope_tables(S)
    perm = _ret_column_perm()
    o = IN_OFFSETS
    xf = x.reshape(T, D).astype(F32)
    xb = xf.astype(BF16)
    for l in range(DEPTH):
        w = w_in[l]
        w_main = jnp.concatenate(
            [w[:, o[0]:o[2]], w[:, o[3]:o[4]][:, perm], w[:, o[4]:o[5]][:, perm], w[:, o[5]:o[9]]],
            axis=1).astype(BF16)
        w_vt = w[:, o[2]:o[3]].T.astype(BF16)
        proj = _inproj(xb, w_main, tabs, S)
        vt = _moba_values_t(xb, w_vt, S)
        yat = _moba(proj, vt, B, S)
        yr = _retention(proj, B, S)
        xf, xb = _merge(yat, yr, proj, xf, w_moba_out[l].astype(BF16), w_ret_out[l].astype(BF16),
                        w_out[l].astype(BF16), ln1_g[l].reshape(1, D), ln1_b[l].reshape(1, D), B, S)
        routing = _route(xb, peer_w_query[l].astype(BF16), peer_sub_keys[l].astype(BF16))
        xf, xb = _peer(xb, xf, peer_u[l].astype(BF16), peer_v[l].T.astype(BF16), routing,
                       ln2_g[l].reshape(1, D), ln2_b[l].reshape(1, D))
    return xf.reshape(B, S, D).astype(x.dtype)
```

```python
import functools
import math

import numpy as np
import jax
import jax.numpy as jnp
from jax import lax
from jax.experimental import pallas as pl
from jax.experimental.pallas import tpu as pltpu

F32 = jnp.float32
BF16 = jnp.bfloat16

D_MODEL = 1024
DEPTH = 2
MOBA_HEADS = 8
MOBA_HEAD_DIM = 64
MOBA_WIDTH = MOBA_HEADS * MOBA_HEAD_DIM
MOBA_BLOCK = 256
MOBA_TOPK = 3
ROPE_THETA = 10000.0
RET_HEADS = 4
RET_QK_DIM = 128
RET_V_DIM = 256
RET_QK_WIDTH = RET_HEADS * RET_QK_DIM
RET_V_WIDTH = RET_HEADS * RET_V_DIM
RET_CHUNK = 256
PEER_N_KEYS = 128
PEER_N_EXPERTS = PEER_N_KEYS * PEER_N_KEYS
PEER_HEADS = 8
PEER_KEY_DIM = 256
PEER_TOPK = 16
DN_ALPHA = (2.0 * DEPTH) ** 0.25
LN_EPS = 1e-5
NEG = -1e30

IN_SIZES = (MOBA_WIDTH, MOBA_WIDTH, MOBA_WIDTH, RET_QK_WIDTH, RET_QK_WIDTH,
            RET_V_WIDTH, RET_V_WIDTH, D_MODEL, D_MODEL)
IN_OFFSETS = tuple(int(v) for v in np.concatenate([[0], np.cumsum(IN_SIZES)]))

LANES = 128
SUBLANES = 8
VMEM_LIMIT = 56 * 1024 * 1024

PROJ_TM = 1024
PROJ_TN = 512
ROUTE_TM = 512
ROUTE_KEY_CHUNK = 4
PEER_TM = 512
PEER_ROWS = 16
PEER_LANE_CHUNK = 256
PEER_CHUNK_ROWS = 2
PEER_SCHED_FLAGS = None

_NT = (((1,), (1,)), ((), ()))


def _params(sem, flags=None):
    return pltpu.CompilerParams(dimension_semantics=sem, vmem_limit_bytes=VMEM_LIMIT, flags=flags)


def _rotate_groups(acc, cos, sin, o_ref, partner_fn):
    for g in range(PROJ_TN // LANES):
        xg = acc[:, g * LANES:(g + 1) * LANES]
        o_ref[:, g * LANES:(g + 1) * LANES] = (xg * cos + partner_fn(xg) * sin).astype(o_ref.dtype)


def _inproj_kernel(x_ref, w_ref, cm_ref, sm_ref, cr_ref, sr_ref, o_ref):
    j = pl.program_id(1)
    acc = jnp.dot(x_ref[...], w_ref[...], preferred_element_type=F32)
    tm = acc.shape[0]

    @pl.when(j < 2)
    def _():
        lane = lax.broadcasted_iota(jnp.int32, (tm, LANES), 1)
        first = (lane % MOBA_HEAD_DIM) < (MOBA_HEAD_DIM // 2)

        def partner(xg):
            return jnp.where(first, pltpu.roll(xg, LANES - MOBA_HEAD_DIM // 2, axis=1),
                             pltpu.roll(xg, MOBA_HEAD_DIM // 2, axis=1))

        _rotate_groups(acc, cm_ref[...], sm_ref[...], o_ref, partner)

    @pl.when((j >= 2) & (j < 4))
    def _():
        _rotate_groups(acc, cr_ref[...], sr_ref[...], o_ref,
                       lambda xg: pltpu.roll(xg, RET_QK_DIM // 2, axis=1))

    @pl.when(j >= 4)
    def _():
        o_ref[...] = acc.astype(o_ref.dtype)


def _inproj(xb, w_main, tabs, S):
    T = xb.shape[0]
    tm = min(PROJ_TM, S)
    n_col = w_main.shape[1] // PROJ_TN
    pos_blocks = S // tm
    tab_spec = pl.BlockSpec((tm, LANES), lambda i, j: (i % pos_blocks, 0))
    return pl.pallas_call(
        _inproj_kernel,
        grid=(T // tm, n_col),
        in_specs=[pl.BlockSpec((tm, D_MODEL), lambda i, j: (i, 0)),
                  pl.BlockSpec((D_MODEL, PROJ_TN), lambda i, j: (0, j)),
                  tab_spec, tab_spec, tab_spec, tab_spec],
        out_specs=pl.BlockSpec((tm, PROJ_TN), lambda i, j: (i, j)),
        out_shape=jax.ShapeDtypeStruct((T, w_main.shape[1]), BF16),
        compiler_params=_params(("parallel", "arbitrary")),
        name="inproj",
    )(xb, w_main, *tabs)


MOBA_VROWS = MOBA_HEAD_DIM + 2 * SUBLANES


def _vt_kernel(x_ref, w_ref, o_ref):
    res = lax.dot_general(w_ref[...], x_ref[...], _NT, preferred_element_type=F32)
    hd = MOBA_HEAD_DIM
    pad = MOBA_VROWS - hd
    ones_row = jnp.where(lax.broadcasted_iota(jnp.int32, (pad, MOBA_BLOCK), 0) == 0, 1.0, 0.0).astype(o_ref.dtype)
    for c in range(o_ref.shape[0]):
        for h in range(MOBA_HEADS):
            o_ref[c, h, 0:hd, :] = res[h * hd:(h + 1) * hd, c * MOBA_BLOCK:(c + 1) * MOBA_BLOCK].astype(o_ref.dtype)
            o_ref[c, h, hd:MOBA_VROWS, :] = ones_row


def _moba_values_t(xb, w_vt, S):
    T = xb.shape[0]
    tm = min(PROJ_TM, S)
    per = tm // MOBA_BLOCK
    return pl.pallas_call(
        _vt_kernel,
        grid=(T // tm,),
        in_specs=[pl.BlockSpec((tm, D_MODEL), lambda i: (i, 0)),
                  pl.BlockSpec((MOBA_WIDTH, D_MODEL), lambda i: (0, 0))],
        out_specs=pl.BlockSpec((per, MOBA_HEADS, MOBA_VROWS, MOBA_BLOCK), lambda i: (i, 0, 0, 0)),
        out_shape=jax.ShapeDtypeStruct((T // MOBA_BLOCK, MOBA_HEADS, MOBA_VROWS, MOBA_BLOCK), BF16),
        compiler_params=_params(("parallel",)),
        name="moba_vt",
    )(xb, w_vt)


def _moba_kernel(q_ref, k_ref, vt_ref, o_ref, kmean_ref, bias_ref, qs_ref, acc_ref, m_ref, sa_ref, sb_ref, *, nb):
    i = pl.program_id(1)
    L = MOBA_BLOCK
    hd = MOBA_HEAD_DIM

    @pl.when(i == 0)
    def _():
        def body(j, c):
            kb = k_ref[pl.ds(pl.multiple_of(j * L, L), L), :].astype(F32)
            kmean_ref[pl.ds(j, 1), :] = jnp.sum(kb, axis=0, keepdims=True) * (1.0 / L)
            return c
        lax.fori_loop(0, nb, body, 0)

    per = LANES // hd
    groups = MOBA_HEADS // per
    W = MOBA_HEADS * L
    lane = lax.broadcasted_iota(jnp.int32, (L, LANES), 1)
    row0 = pl.multiple_of(i * L, L)

    gates = []
    for g in range(groups):
        cols = slice(g * LANES, (g + 1) * LANES)
        q_pair = q_ref[:, cols]
        km = kmean_ref[:, cols].astype(BF16)
        for hh in range(per):
            qm = jnp.where((lane // hd) == hh, q_pair, jnp.zeros_like(q_pair))
            gates.append(lax.dot_general(km, qm, _NT, preferred_element_type=F32))
            qs_ref[g, hh * L:(hh + 1) * L, :] = (qm.astype(F32) * (hd ** -0.5 * math.log2(math.e))).astype(BF16)

    blk = lax.broadcasted_iota(jnp.int32, (nb, W), 0)
    blkf = blk.astype(F32)
    gm = jnp.where(blk < i, jnp.concatenate(gates, axis=1), -jnp.inf)
    keep = jnp.zeros((nb, W), F32)
    for _ in range(MOBA_TOPK):
        best = jnp.max(gm, axis=0, keepdims=True)
        first = jnp.min(jnp.where(gm == best, blkf, float(nb)), axis=0, keepdims=True)
        hit = blkf == first
        keep = jnp.where(hit, 1.0, keep)
        gm = jnp.where(hit, -jnp.inf, gm)
    bias_ref[...] = jnp.where(blk < i, jnp.where(keep > 0.0, 0.0, NEG), NEG)

    def scores(r):
        return jnp.concatenate(
            [lax.dot_general(k_ref[pl.ds(r, L), g * LANES:(g + 1) * LANES], qs_ref[g], _NT,
                             preferred_element_type=F32) for g in range(groups)], axis=1)

    def values(j, p):
        pb = p.astype(BF16)
        return [jnp.dot(vt_ref[j, h], pb[:, h * L:(h + 1) * L], preferred_element_type=F32)
                for h in range(MOBA_HEADS)]

    kpos = lax.broadcasted_iota(jnp.int32, (L, W), 0)
    qpos = lax.broadcasted_iota(jnp.int32, (L, W), 1) % L
    s = jnp.where(kpos <= qpos, scores(row0), NEG)
    m = jnp.max(s, axis=0, keepdims=True)
    m_ref[...] = jnp.broadcast_to(m, (SUBLANES, W))
    for h, pv in enumerate(values(i, jnp.exp2(s - m))):
        acc_ref[h] = pv

    def scores_into(dst_ref, r):
        first = None
        for g in range(groups):
            sg = lax.dot_general(k_ref[pl.ds(r, L), g * LANES:(g + 1) * LANES], qs_ref[g], _NT,
                                 preferred_element_type=F32)
            dst_ref[:, g * per * L:(g + 1) * per * L] = sg
            if first is None:
                first = pltpu.bitcast(sg[0:SUBLANES, 0:LANES], jnp.uint32)
        return ((first >> 16) >> 16)[0, 0].astype(jnp.int32)

    def softmax_update(j, cur_ref, heads, row_start):
        lanes = slice(heads[0] * L, (heads[-1] + 1) * L)
        s = cur_ref[pl.ds(row_start, L), lanes]
        bias = bias_ref[pl.ds(j, 1), lanes]
        m_old = m_ref[:, lanes]
        m_new = jnp.maximum(m_old, jnp.max(s, axis=0, keepdims=True) + bias)
        alpha = jnp.exp2(m_old - m_new)
        m_ref[:, lanes] = m_new
        pb = jnp.exp2(s - (m_new[0:1, :] - bias)).astype(BF16)
        for n, h in enumerate(heads):
            pv = jnp.dot(vt_ref[j, h], pb[:, n * L:(n + 1) * L], preferred_element_type=F32)
            acc_ref[h] = alpha[0:1, n * L:(n + 1) * L] * acc_ref[h] + pv

    half = MOBA_HEADS // 2

    def past(j, cur_ref, nxt_ref):
        start = 0
        if nxt_ref is not None:
            nxt = jnp.minimum(j + 1, i - 1)
            start = pl.multiple_of(scores_into(nxt_ref, pl.multiple_of(nxt * L, L)), L)
        softmax_update(j, cur_ref, tuple(range(half)), 0)
        softmax_update(j, cur_ref, tuple(range(half, MOBA_HEADS)), start)

    @pl.when(i > 0)
    def _():
        scores_into(sa_ref, 0)

    def two_past(jj, c):
        past(2 * jj, sa_ref, sb_ref)
        past(2 * jj + 1, sb_ref, sa_ref)
        return c

    lax.fori_loop(0, i // 2, two_past, 0)

    @pl.when(i % 2 == 1)
    def _():
        past(i - 1, sa_ref, None)

    for h in range(MOBA_HEADS):
        o_ref[0, h * hd:(h + 1) * hd, :] = acc_ref[h, 0:hd, :] / acc_ref[h, hd:hd + 1, :]


def _moba(proj, vt, B, S):
    nb = S // MOBA_BLOCK
    L = MOBA_BLOCK
    return pl.pallas_call(
        functools.partial(_moba_kernel, nb=nb),
        grid=(B, nb),
        in_specs=[pl.BlockSpec((L, MOBA_WIDTH), lambda b, i: (b * nb + i, 0)),
                  pl.BlockSpec((S, MOBA_WIDTH), lambda b, i: (b, 1)),
                  pl.BlockSpec((nb, MOBA_HEADS, MOBA_VROWS, L), lambda b, i: (b, 0, 0, 0))],
        out_specs=pl.BlockSpec((1, MOBA_WIDTH, L), lambda b, i: (b, 0, i)),
        out_shape=jax.ShapeDtypeStruct((B, MOBA_WIDTH, S), F32),
        scratch_shapes=[pltpu.VMEM((nb, MOBA_WIDTH), F32),
                        pltpu.VMEM((nb, MOBA_HEADS * L), F32),
                        pltpu.VMEM((MOBA_HEADS // (LANES // MOBA_HEAD_DIM), (LANES // MOBA_HEAD_DIM) * L, LANES),
                                   BF16),
                        pltpu.VMEM((MOBA_HEADS, MOBA_VROWS, L), F32),
                        pltpu.VMEM((SUBLANES, MOBA_HEADS * L), F32),
                        pltpu.VMEM((L, MOBA_HEADS * L), F32),
                        pltpu.VMEM((L, MOBA_HEADS * L), F32)],
        compiler_params=_params(("parallel", "arbitrary")),
        name="moba",
    )(proj, proj, vt)


def _ret_log_g():
    return jnp.log(1.0 - 2.0 ** (-5.0 - jnp.arange(RET_HEADS, dtype=F32)))


def _retention_kernel(q_ref, k_ref, v_ref, g_ref, dec_ref, qd_ref, kd_ref, cd_ref, o_ref, state_ref):
    n = pl.program_id(1)

    @pl.when(n == 0)
    def _():
        state_ref[...] = jnp.zeros_like(state_ref)

    for h in range(RET_HEADS):
        qk = slice(h * RET_QK_DIM, (h + 1) * RET_QK_DIM)
        vv = slice(h * RET_V_DIM, (h + 1) * RET_V_DIM)
        q = q_ref[:, qk]
        k = k_ref[:, qk]
        v = v_ref[:, vv]
        scores = lax.dot_general(q, k, _NT, preferred_element_type=F32) * dec_ref[h]
        y = jnp.dot(scores.astype(BF16), v, preferred_element_type=F32)
        state = state_ref[h]
        y = y + jnp.dot(q, state.astype(BF16), preferred_element_type=F32) * qd_ref[h]
        kt = (k.astype(F32) * kd_ref[h]).T.astype(BF16)
        kv = jnp.dot(kt, v, preferred_element_type=F32)
        state_ref[h] = state * cd_ref[h:h + 1, :] + kv
        mu = jnp.mean(y, axis=-1, keepdims=True)
        yc = y - mu
        var = jnp.mean(yc * yc, axis=-1, keepdims=True)
        yn = yc * lax.rsqrt(var + LN_EPS)
        gate = g_ref[:, vv].astype(F32)
        o_ref[:, vv] = (gate * jax.nn.sigmoid(gate) * yn).astype(o_ref.dtype)


def _retention(proj, B, S):
    C = RET_CHUNK
    nc = S // C
    T = B * S
    log_g = _ret_log_g()
    pos = jnp.arange(C, dtype=F32)
    diff = pos[:, None] - pos[None, :]
    scale = RET_QK_DIM ** -0.5
    decay = jnp.where(diff >= 0, jnp.exp(log_g[:, None, None] * jnp.maximum(diff, 0.0)), 0.0) * scale
    q_decay = jnp.broadcast_to(jnp.exp(log_g[:, None] * (pos + 1.0))[:, :, None], (RET_HEADS, C, RET_V_DIM))
    k_decay = jnp.broadcast_to((jnp.exp(log_g[:, None] * (C - 1.0 - pos)) * scale)[:, :, None],
                               (RET_HEADS, C, RET_QK_DIM))
    chunk_decay = jnp.broadcast_to(jnp.exp(log_g * C)[:, None], (RET_HEADS, RET_V_DIM))
    const = lambda shape: pl.BlockSpec(shape, lambda b, n: (0,) * len(shape))
    return pl.pallas_call(
        _retention_kernel,
        grid=(B, nc),
        in_specs=[pl.BlockSpec((C, RET_QK_WIDTH), lambda b, n: (b * nc + n, 2)),
                  pl.BlockSpec((C, RET_QK_WIDTH), lambda b, n: (b * nc + n, 3)),
                  pl.BlockSpec((C, RET_V_WIDTH), lambda b, n: (b * nc + n, 2)),
                  pl.BlockSpec((C, RET_V_WIDTH), lambda b, n: (b * nc + n, 3)),
                  const((RET_HEADS, C, C)), const((RET_HEADS, C, RET_V_DIM)),
                  const((RET_HEADS, C, RET_QK_DIM)), const((RET_HEADS, RET_V_DIM))],
        out_specs=pl.BlockSpec((C, RET_V_WIDTH), lambda b, n: (b * nc + n, 0)),
        out_shape=jax.ShapeDtypeStruct((T, RET_V_WIDTH), BF16),
        scratch_shapes=[pltpu.VMEM((RET_HEADS, RET_QK_DIM, RET_V_DIM), F32)],
        compiler_params=_params(("parallel", "arbitrary")),
        name="retention",
    )(proj, proj, proj, proj, decay, q_decay, k_decay, chunk_decay)


def _layer_norm(y, g, b):
    mu = jnp.mean(y, axis=-1, keepdims=True)
    yc = y - mu
    var = jnp.mean(yc * yc, axis=-1, keepdims=True)
    return yc * lax.rsqrt(var + LN_EPS) * g + b


def _merge_kernel(yat_ref, yr_ref, ga_ref, gr_ref, x_ref, wa_ref, wr_ref, wo_ref, g_ref, b_ref, o_ref, ob_ref):
    ya = yat_ref[0].T.astype(BF16)
    branch_a = jnp.dot(ya, wa_ref[...], preferred_element_type=F32)
    branch_r = jnp.dot(yr_ref[...], wr_ref[...], preferred_element_type=F32)
    merged = (jax.nn.sigmoid(ga_ref[...].astype(F32)) * branch_a
              + jax.nn.sigmoid(gr_ref[...].astype(F32)) * branch_r)
    mix = jnp.dot(merged.astype(BF16), wo_ref[...], preferred_element_type=F32)
    y = _layer_norm(DN_ALPHA * x_ref[...] + mix, g_ref[...], b_ref[...])
    o_ref[...] = y
    ob_ref[...] = y.astype(BF16)


def _merge(yat, yr, proj, x, wa, wr, wo, g, b, B, S):
    L = MOBA_BLOCK
    nb = S // L
    T = B * S
    tok = lambda c: pl.BlockSpec((L, D_MODEL), lambda bb, i: (bb * nb + i, c))
    const = lambda shape: pl.BlockSpec(shape, lambda bb, i: (0,) * len(shape))
    return pl.pallas_call(
        _merge_kernel,
        grid=(B, nb),
        in_specs=[pl.BlockSpec((1, MOBA_WIDTH, L), lambda bb, i: (bb, 0, i)),
                  tok(0), tok(4), tok(5), tok(0),
                  const((MOBA_WIDTH, D_MODEL)), const((RET_V_WIDTH, D_MODEL)), const((D_MODEL, D_MODEL)),
                  const((1, D_MODEL)), const((1, D_MODEL))],
        out_specs=[tok(0), tok(0)],
        out_shape=[jax.ShapeDtypeStruct((T, D_MODEL), F32), jax.ShapeDtypeStruct((T, D_MODEL), BF16)],
        compiler_params=_params(("parallel", "parallel")),
        name="merge_ln1",
    )(yat, yr, proj, proj, x, wa, wr, wo, g, b)


def _cmpx(xs, a, b):
    if xs[b] is None:
        return
    if xs[a] is None:
        xs[a], xs[b] = xs[b], None
        return
    hi = jnp.maximum(xs[a], xs[b])
    lo = jnp.minimum(xs[a], xs[b])
    xs[a], xs[b] = hi, lo


def _bitonic_merge_desc(xs):
    n = len(xs)
    j = n // 2
    while j >= 1:
        for a in range(n):
            b = a ^ j
            if b > a:
                _cmpx(xs, a, b)
        j //= 2
    return xs


def _sort_desc(xs):
    xs = list(xs)
    n = len(xs)
    k = 2
    while k <= n:
        j = k // 2
        while j >= 1:
            for a in range(n):
                b = a ^ j
                if b > a:
                    if (a & k) == 0:
                        _cmpx(xs, a, b)
                    else:
                        _cmpx(xs, b, a)
            j //= 2
        k *= 2
    return xs


def _max_or_none(x, y):
    if x is None:
        return y
    if y is None:
        return x
    return jnp.maximum(x, y)


def _merge_top(xs, ys):
    n = len(xs)
    return _bitonic_merge_desc([_max_or_none(xs[v], ys[n - 1 - v]) for v in range(n)])


def _top_desc(rows, k):
    lists = [_sort_desc(rows[g:g + k]) for g in range(0, len(rows), k)]
    while len(lists) > 1:
        lists = [_merge_top(lists[g], lists[g + 1]) for g in range(0, len(lists), 2)]
    return lists[0]


def _route_kernel(x_ref, wq_ref, sk_ref, r1_ref, e1_ref, n0_ref, c0_ref, sc_ref, tmp_ref):
    half_dim = PEER_KEY_DIM // 2
    for sub in range(x_ref.shape[0] // LANES):
        toks = slice(sub * LANES, (sub + 1) * LANES)
        q = jnp.dot(x_ref[toks, :], wq_ref[...], preferred_element_type=F32).astype(BF16)
        for p in range(2):
            for h in range(PEER_HEADS):
                qhp = q[:, (2 * h + p) * half_dim:(2 * h + p + 1) * half_dim]
                sc_ref[sub, p, pl.ds(h, PEER_N_KEYS, stride=PEER_HEADS), :] = lax.dot_general(
                    sk_ref[p], qhp, _NT, preferred_element_type=F32)
        _route_tokens(sc_ref.at[sub], tmp_ref.at[sub], toks, r1_ref, e1_ref, n0_ref, c0_ref)


def _route_tokens(sc_ref, tmp_ref, toks, r1_ref, e1_ref, n0_ref, c0_ref):
    K = PEER_TOPK
    nk = PEER_N_KEYS
    H = PEER_HEADS
    tm = LANES
    inf = jnp.inf
    s0 = sc_ref[0].reshape(nk, H, tm)
    s1 = sc_ref[1].reshape(nk, H, tm)
    a = _top_desc([s0[k] for k in range(nk)], K)
    b = _top_desc([s1[k] for k in range(nk)], K)

    sums = [[a[r] + b[c] for c in range(K // (r + 1))] for r in range(K)]
    pad = lambda xs: xs + [None] * (K - len(xs))
    z = _merge_top(sums[0], pad(sums[1]))
    mid = [x for r in range(2, 7) for x in sums[r]]
    z = _merge_top(z, _sort_desc(pad(mid)))
    low = [x for r in range(7, K) for x in sums[r]]
    z = _merge_top(z, _sort_desc(pad(low)))
    tau = z[K - 1]
    zsum = jnp.ones_like(tau)
    for v in range(1, K):
        zsum = zsum + jnp.exp(z[v] - z[0])
    inv_z = 1.0 / zsum

    alphas = []
    for c in range(K):
        alpha = None
        for r in range(K // (c + 1)):
            cand = jnp.where(sums[r][c] >= tau, a[r], inf)
            alpha = cand if alpha is None else jnp.minimum(alpha, cand)
        alphas.append(alpha)

    for k0 in range(0, nk, ROUTE_KEY_CHUNK):
        blk0 = s0[k0:k0 + ROUTE_KEY_CHUNK]
        blk1 = s1[k0:k0 + ROUTE_KEY_CHUNK]
        count = jnp.zeros(blk0.shape, F32)
        for c in range(K):
            count = jnp.where(blk0 >= alphas[c][None], float(c + 1), count)
        rank = jnp.full(blk1.shape, float(K), F32)
        for c in reversed(range(K)):
            rank = jnp.where(blk1 >= b[c][None], float(c), rank)
        e1 = jnp.exp(blk1 - b[0][None])
        c0 = jnp.exp(blk0 - a[0][None]) * inv_z[None]
        rows = slice(k0 * H, (k0 + ROUTE_KEY_CHUNK) * H)
        for n, val in enumerate((rank, e1, count, c0)):
            tmp_ref[n, rows, :] = val.reshape(ROUTE_KEY_CHUNK * H, tm)

    for n, dst in enumerate((r1_ref, e1_ref, n0_ref, c0_ref)):
        for h in range(H):
            dst[h, :, toks] = tmp_ref[n, pl.ds(h, nk, stride=H), :].astype(dst.dtype)


def _route(xb, wq, sk):
    T = xb.shape[0]
    tm = ROUTE_TM
    H = PEER_HEADS
    nk = PEER_N_KEYS
    subs = tm // LANES
    spec = pl.BlockSpec((H, nk, tm), lambda t: (0, 0, t))
    half = jax.ShapeDtypeStruct((H, nk, T), BF16)
    word = jax.ShapeDtypeStruct((H, nk, T), F32)
    return pl.pallas_call(
        _route_kernel,
        grid=(T // tm,),
        in_specs=[pl.BlockSpec((tm, D_MODEL), lambda t: (t, 0)),
                  pl.BlockSpec(wq.shape, lambda t: (0, 0)),
                  pl.BlockSpec(sk.shape, lambda t: (0, 0, 0))],
        out_specs=[spec, spec, spec, spec],
        out_shape=[half, half, word, word],
        scratch_shapes=[pltpu.VMEM((subs, 2, nk * H, LANES), F32),
                        pltpu.VMEM((subs, 4, nk * H, LANES), F32)],
        compiler_params=_params(("parallel",)),
        name="peer_route",
    )(xb, wq, sk)


def _gelu_tanh(x):
    k = -2.0 * math.sqrt(2.0 / math.pi) * math.log2(math.e)
    z = x * (k + (k * 0.044715) * (x * x))
    return x / (1.0 + jnp.exp2(z))


def _peer_kernel(xb_ref, u_ref, vt_ref, r1_ref, e1_ref, n0_ref, c0_ref, x_ref, g_ref, b_ref,
                 o_ref, ob_ref, w_ref, act_ref, acc_ref, *, rows):
    e = pl.program_id(1)
    nk = PEER_N_KEYS
    tm = xb_ref.shape[0]
    pk = 2 * SUBLANES

    @pl.when(e == 0)
    def _():
        acc_ref[...] = jnp.zeros_like(acc_ref)

    tc = min(PEER_LANE_CHUNK, tm)

    def routing_weights(r):
        i = e * rows + r
        first = None
        for ch in range(tm // tc):
            ls = slice(ch * tc, (ch + 1) * tc)
            w = [None] * (nk // pk)
            for h in range(PEER_HEADS):
                count = jnp.broadcast_to(n0_ref[h, pl.ds(i, 1), ls], (pk, tc)).astype(BF16)
                weight = jnp.broadcast_to(c0_ref[h, pl.ds(i, 1), ls], (pk, tc)).astype(BF16)
                for g in range(nk // pk):
                    rs = slice(g * pk, (g + 1) * pk)
                    sel = jnp.where(r1_ref[h, rs, ls] < count, e1_ref[h, rs, ls], jnp.zeros((pk, tc), BF16))
                    term = sel * weight
                    w[g] = term if w[g] is None else w[g] + term
            for g in range(nk // pk):
                w_ref[r * nk + g * pk:r * nk + (g + 1) * pk, ls] = w[g]
            if first is None:
                first = pltpu.bitcast(w[0][:, 0:LANES], jnp.uint32)
        return first

    cr = PEER_CHUNK_ROWS
    zero = 0
    for c in range(rows // cr):
        bits = routing_weights(c * cr)
        for r in range(c * cr + 1, (c + 1) * cr):
            routing_weights(r)
        lo = 0 if c == 0 else pl.multiple_of(c * cr * nk + zero, cr * nk)
        ht = lax.dot_general(u_ref[pl.ds(lo, cr * nk), :], xb_ref[...], _NT,
                             preferred_element_type=F32)
        rows_c = slice(c * cr * nk, (c + 1) * cr * nk)
        act_ref[rows_c, :] = _gelu_tanh(ht.astype(BF16)) * w_ref[rows_c, :]
        zero = ((bits >> 16) >> 16)[0, 0].astype(jnp.int32)
    acc_ref[...] += jnp.dot(vt_ref[...], act_ref[...], preferred_element_type=F32)

    @pl.when(e == pl.num_programs(1) - 1)
    def _():
        y = _layer_norm(DN_ALPHA * x_ref[...] + acc_ref[...].T, g_ref[...], b_ref[...])
        o_ref[...] = y
        ob_ref[...] = y.astype(BF16)


def _peer(xb, x, u, vt, routing, g, b):
    T = xb.shape[0]
    tm = min(PEER_TM, T)
    rows = PEER_ROWS
    nk = PEER_N_KEYS
    rspec = pl.BlockSpec((PEER_HEADS, nk, tm), lambda t, e: (0, 0, t))
    tok = pl.BlockSpec((tm, D_MODEL), lambda t, e: (t, 0))
    const = pl.BlockSpec((1, D_MODEL), lambda t, e: (0, 0))
    return pl.pallas_call(
        functools.partial(_peer_kernel, rows=rows),
        grid=(T // tm, nk // rows),
        in_specs=[tok,
                  pl.BlockSpec((rows * nk, D_MODEL), lambda t, e: (e, 0)),
                  pl.BlockSpec((D_MODEL, rows * nk), lambda t, e: (0, e)),
                  rspec, rspec, rspec, rspec, tok, const, const],
        out_specs=[tok, tok],
        out_shape=[jax.ShapeDtypeStruct((T, D_MODEL), F32), jax.ShapeDtypeStruct((T, D_MODEL), BF16)],
        scratch_shapes=[pltpu.VMEM((rows * nk, tm), BF16),
                        pltpu.VMEM((rows * nk, tm), BF16),
                        pltpu.VMEM((D_MODEL, tm), F32)],
        compiler_params=_params(("parallel", "arbitrary"), PEER_SCHED_FLAGS),
        name="peer_dense",
    )(xb, u, vt, *routing, x, g, b)


def _rope_tables(S):
    pos = jnp.arange(S, dtype=F32)
    d = MOBA_HEAD_DIM
    inv = ROPE_THETA ** (-jnp.arange(0, d, 2, dtype=F32) / d)
    ang = pos[:, None] * inv[None, :]
    cos, sin = jnp.cos(ang), jnp.sin(ang)
    reps = LANES // d
    cos_m = jnp.tile(jnp.concatenate([cos, cos], axis=1), (1, reps))
    sin_m = jnp.tile(jnp.concatenate([-sin, sin], axis=1), (1, reps))
    d = RET_QK_DIM
    inv = 1.0 / (ROPE_THETA ** jnp.linspace(0.0, 1.0, d // 2, dtype=F32))
    ang = pos[:, None] * inv[None, :]
    cos, sin = jnp.cos(ang), jnp.sin(ang)
    cos_r = jnp.concatenate([cos, cos], axis=1)
    sin_r = jnp.concatenate([-sin, sin], axis=1)
    return cos_m, sin_m, cos_r, sin_r


def _ret_column_perm():
    within = np.concatenate([np.arange(0, RET_QK_DIM, 2), np.arange(1, RET_QK_DIM, 2)])
    return np.concatenate([h * RET_QK_DIM + within for h in range(RET_HEADS)])


def kernel(x, w_in, w_moba_out, w_ret_out, w_out, ln1_g, ln1_b, peer_w_query, peer_sub_keys,
           peer_u, peer_v, ln2_g, ln2_b):
    B, S, D = x.shape
    assert D == D_MODEL and S % MOBA_BLOCK == 0 and S % RET_CHUNK == 0
    T = B * S
    tabs = _rope_tables(S)
    perm = _ret_column_perm()
    o = IN_OFFSETS
    xf = x.reshape(T, D).astype(F32)
    xb = xf.astype(BF16)
    for l in range(DEPTH):
        w = w_in[l]
        w_main = jnp.concatenate(
            [w[:, o[0]:o[2]], w[:, o[3]:o[4]][:, perm], w[:, o[4]:o[5]][:, perm], w[:, o[5]:o[9]]],
            axis=1).astype(BF16)
        w_vt = w[:, o[2]:o[3]].T.astype(BF16)
        proj = _inproj(xb, w_main, tabs, S)
        vt = _moba_values_t(xb, w_vt, S)
        yat = _moba(proj, vt, B, S)
        yr = _retention(proj, B, S)
        xf, xb = _merge(yat, yr, proj, xf, w_moba_out[l].astype(BF16), w_ret_out[l].astype(BF16),
                        w_out[l].astype(BF16), ln1_g[l].reshape(1, D), ln1_b[l].reshape(1, D), B, S)
        routing = _route(xb, peer_w_query[l].astype(BF16), peer_sub_keys[l].astype(BF16))
        xf, xb = _peer(xb, xf, peer_u[l].astype(BF16), peer_v[l].T.astype(BF16), routing,
                       ln2_g[l].reshape(1, D), ln2_b[l].reshape(1, D))
    return xf.reshape(B, S, D).astype(x.dtype)
```

```python
import functools
import math

import numpy as np
import jax
import jax.numpy as jnp
from jax import lax
from jax.experimental import pallas as pl
from jax.experimental.pallas import tpu as pltpu

F32 = jnp.float32
BF16 = jnp.bfloat16

D_MODEL = 1024
DEPTH = 2
MOBA_HEADS = 8
MOBA_HEAD_DIM = 64
MOBA_WIDTH = MOBA_HEADS * MOBA_HEAD_DIM
MOBA_BLOCK = 256
MOBA_TOPK = 3
ROPE_THETA = 10000.0
RET_HEADS = 4
RET_QK_DIM = 128
RET_V_DIM = 256
RET_QK_WIDTH = RET_HEADS * RET_QK_DIM
RET_V_WIDTH = RET_HEADS * RET_V_DIM
RET_CHUNK = 256
PEER_N_KEYS = 128
PEER_N_EXPERTS = PEER_N_KEYS * PEER_N_KEYS
PEER_HEADS = 8
PEER_KEY_DIM = 256
PEER_TOPK = 16
DN_ALPHA = (2.0 * DEPTH) ** 0.25
LN_EPS = 1e-5
NEG = -1e30

IN_SIZES = (MOBA_WIDTH, MOBA_WIDTH, MOBA_WIDTH, RET_QK_WIDTH, RET_QK_WIDTH,
            RET_V_WIDTH, RET_V_WIDTH, D_MODEL, D_MODEL)
IN_OFFSETS = tuple(int(v) for v in np.concatenate([[0], np.cumsum(IN_SIZES)]))

LANES = 128
SUBLANES = 8
VMEM_LIMIT = 56 * 1024 * 1024

PROJ_TM = 1024
PROJ_TN = 1024
ROUTE_TM = 512
ROUTE_KEY_CHUNK = 4
PEER_TM = 512
PEER_ROWS = 16
PEER_LANE_CHUNK = 256
PEER_CHUNK_ROWS = 2
PEER_SCHED_FLAGS = None

_NT = (((1,), (1,)), ((), ()))


def _params(sem, flags=None):
    return pltpu.CompilerParams(dimension_semantics=sem, vmem_limit_bytes=VMEM_LIMIT, flags=flags)


def _rotate_groups(acc, cos, sin, o_ref, partner_fn):
    for g in range(PROJ_TN // LANES):
        xg = acc[:, g * LANES:(g + 1) * LANES]
        o_ref[:, g * LANES:(g + 1) * LANES] = (xg * cos + partner_fn(xg) * sin).astype(o_ref.dtype)


def _inproj_kernel(x_ref, w_ref, cm_ref, sm_ref, cr_ref, sr_ref, o_ref):
    j = pl.program_id(1)
    acc = jnp.dot(x_ref[...], w_ref[...], preferred_element_type=F32)
    tm = acc.shape[0]

    moba_tiles = 2 * MOBA_WIDTH // PROJ_TN
    rope_tiles = moba_tiles + 2 * RET_QK_WIDTH // PROJ_TN

    @pl.when(j < moba_tiles)
    def _():
        lane = lax.broadcasted_iota(jnp.int32, (tm, LANES), 1)
        first = (lane % MOBA_HEAD_DIM) < (MOBA_HEAD_DIM // 2)

        def partner(xg):
            return jnp.where(first, pltpu.roll(xg, LANES - MOBA_HEAD_DIM // 2, axis=1),
                             pltpu.roll(xg, MOBA_HEAD_DIM // 2, axis=1))

        _rotate_groups(acc, cm_ref[...], sm_ref[...], o_ref, partner)

    @pl.when((j >= moba_tiles) & (j < rope_tiles))
    def _():
        _rotate_groups(acc, cr_ref[...], sr_ref[...], o_ref,
                       lambda xg: pltpu.roll(xg, RET_QK_DIM // 2, axis=1))

    @pl.when(j >= rope_tiles)
    def _():
        o_ref[...] = acc.astype(o_ref.dtype)


def _inproj(xb, w_main, tabs, S):
    T = xb.shape[0]
    tm = min(PROJ_TM, S)
    n_col = w_main.shape[1] // PROJ_TN
    pos_blocks = S // tm
    tab_spec = pl.BlockSpec((tm, LANES), lambda i, j: (i % pos_blocks, 0))
    return pl.pallas_call(
        _inproj_kernel,
        grid=(T // tm, n_col),
        in_specs=[pl.BlockSpec((tm, D_MODEL), lambda i, j: (i, 0)),
                  pl.BlockSpec((D_MODEL, PROJ_TN), lambda i, j: (0, j)),
                  tab_spec, tab_spec, tab_spec, tab_spec],
        out_specs=pl.BlockSpec((tm, PROJ_TN), lambda i, j: (i, j)),
        out_shape=jax.ShapeDtypeStruct((T, w_main.shape[1]), BF16),
        compiler_params=_params(("parallel", "arbitrary")),
        name="inproj",
    )(xb, w_main, *tabs)


MOBA_VROWS = MOBA_HEAD_DIM + 2 * SUBLANES


def _vt_kernel(x_ref, w_ref, o_ref):
    res = lax.dot_general(w_ref[...], x_ref[...], _NT, preferred_element_type=F32)
    hd = MOBA_HEAD_DIM
    pad = MOBA_VROWS - hd
    ones_row = jnp.where(lax.broadcasted_iota(jnp.int32, (pad, MOBA_BLOCK), 0) == 0, 1.0, 0.0).astype(o_ref.dtype)
    for c in range(o_ref.shape[0]):
        for h in range(MOBA_HEADS):
            o_ref[c, h, 0:hd, :] = res[h * hd:(h + 1) * hd, c * MOBA_BLOCK:(c + 1) * MOBA_BLOCK].astype(o_ref.dtype)
            o_ref[c, h, hd:MOBA_VROWS, :] = ones_row


def _moba_values_t(xb, w_vt, S):
    T = xb.shape[0]
    tm = min(PROJ_TM, S)
    per = tm // MOBA_BLOCK
    return pl.pallas_call(
        _vt_kernel,
        grid=(T // tm,),
        in_specs=[pl.BlockSpec((tm, D_MODEL), lambda i: (i, 0)),
                  pl.BlockSpec((MOBA_WIDTH, D_MODEL), lambda i: (0, 0))],
        out_specs=pl.BlockSpec((per, MOBA_HEADS, MOBA_VROWS, MOBA_BLOCK), lambda i: (i, 0, 0, 0)),
        out_shape=jax.ShapeDtypeStruct((T // MOBA_BLOCK, MOBA_HEADS, MOBA_VROWS, MOBA_BLOCK), BF16),
        compiler_params=_params(("parallel",)),
        name="moba_vt",
    )(xb, w_vt)


def _moba_kernel(q_ref, k_ref, vt_ref, o_ref, kmean_ref, bias_ref, qs_ref, acc_ref, m_ref, sa_ref, sb_ref, *, nb):
    i = pl.program_id(1)
    L = MOBA_BLOCK
    hd = MOBA_HEAD_DIM

    @pl.when(i == 0)
    def _():
        def body(j, c):
            kb = k_ref[pl.ds(pl.multiple_of(j * L, L), L), :].astype(F32)
            kmean_ref[pl.ds(j, 1), :] = jnp.sum(kb, axis=0, keepdims=True) * (1.0 / L)
            return c
        lax.fori_loop(0, nb, body, 0)

    per = LANES // hd
    groups = MOBA_HEADS // per
    W = MOBA_HEADS * L
    lane = lax.broadcasted_iota(jnp.int32, (L, LANES), 1)
    row0 = pl.multiple_of(i * L, L)

    gates = []
    for g in range(groups):
        cols = slice(g * LANES, (g + 1) * LANES)
        q_pair = q_ref[:, cols]
        km = kmean_ref[:, cols].astype(BF16)
        for hh in range(per):
            qm = jnp.where((lane // hd) == hh, q_pair, jnp.zeros_like(q_pair))
            gates.append(lax.dot_general(km, qm, _NT, preferred_element_type=F32))
            qs_ref[g, hh * L:(hh + 1) * L, :] = (qm.astype(F32) * (hd ** -0.5 * math.log2(math.e))).astype(BF16)

    blk = lax.broadcasted_iota(jnp.int32, (nb, W), 0)
    blkf = blk.astype(F32)
    gm = jnp.where(blk < i, jnp.concatenate(gates, axis=1), -jnp.inf)
    keep = jnp.zeros((nb, W), F32)
    for _ in range(MOBA_TOPK):
        best = jnp.max(gm, axis=0, keepdims=True)
        first = jnp.min(jnp.where(gm == best, blkf, float(nb)), axis=0, keepdims=True)
        hit = blkf == first
        keep = jnp.where(hit, 1.0, keep)
        gm = jnp.where(hit, -jnp.inf, gm)
    bias_ref[...] = jnp.where(blk < i, jnp.where(keep > 0.0, 0.0, NEG), NEG)

    def scores(r):
        return jnp.concatenate(
            [lax.dot_general(k_ref[pl.ds(r, L), g * LANES:(g + 1) * LANES], qs_ref[g], _NT,
                             preferred_element_type=F32) for g in range(groups)], axis=1)

    def values(j, p):
        pb = p.astype(BF16)
        return [jnp.dot(vt_ref[j, h], pb[:, h * L:(h + 1) * L], preferred_element_type=F32)
                for h in range(MOBA_HEADS)]

    kpos = lax.broadcasted_iota(jnp.int32, (L, W), 0)
    qpos = lax.broadcasted_iota(jnp.int32, (L, W), 1) % L
    s = jnp.where(kpos <= qpos, scores(row0), NEG)
    m = jnp.max(s, axis=0, keepdims=True)
    m_ref[...] = jnp.broadcast_to(m, (SUBLANES, W))
    for h, pv in enumerate(values(i, jnp.exp2(s - m))):
        acc_ref[h] = pv

    def scores_into(dst_ref, r):
        first = None
        for g in range(groups):
            sg = lax.dot_general(k_ref[pl.ds(r, L), g * LANES:(g + 1) * LANES], qs_ref[g], _NT,
                                 preferred_element_type=F32)
            dst_ref[:, g * per * L:(g + 1) * per * L] = sg
            if first is None:
                first = pltpu.bitcast(sg[0:SUBLANES, 0:LANES], jnp.uint32)
        return ((first >> 16) >> 16)[0, 0].astype(jnp.int32)

    def softmax_update(j, cur_ref, heads, row_start):
        lanes = slice(heads[0] * L, (heads[-1] + 1) * L)
        s = cur_ref[pl.ds(row_start, L), lanes]
        bias = bias_ref[pl.ds(j, 1), lanes]
        m_old = m_ref[:, lanes]
        m_new = jnp.maximum(m_old, jnp.max(s, axis=0, keepdims=True) + bias)
        alpha = jnp.exp2(m_old - m_new)
        m_ref[:, lanes] = m_new
        pb = jnp.exp2(s - (m_new[0:1, :] - bias)).astype(BF16)
        for n, h in enumerate(heads):
            pv = jnp.dot(vt_ref[j, h], pb[:, n * L:(n + 1) * L], preferred_element_type=F32)
            acc_ref[h] = alpha[0:1, n * L:(n + 1) * L] * acc_ref[h] + pv

    half = MOBA_HEADS // 2

    def past(j, cur_ref, nxt_ref):
        start = 0
        if nxt_ref is not None:
            nxt = jnp.minimum(j + 1, i - 1)
            start = pl.multiple_of(scores_into(nxt_ref, pl.multiple_of(nxt * L, L)), L)
        softmax_update(j, cur_ref, tuple(range(half)), 0)
        softmax_update(j, cur_ref, tuple(range(half, MOBA_HEADS)), start)

    @pl.when(i > 0)
    def _():
        scores_into(sa_ref, 0)

    def two_past(jj, c):
        past(2 * jj, sa_ref, sb_ref)
        past(2 * jj + 1, sb_ref, sa_ref)
        return c

    lax.fori_loop(0, i // 2, two_past, 0)

    @pl.when(i % 2 == 1)
    def _():
        past(i - 1, sa_ref, None)

    for h in range(MOBA_HEADS):
        o_ref[0, h * hd:(h + 1) * hd, :] = acc_ref[h, 0:hd, :] / acc_ref[h, hd:hd + 1, :]


def _moba(proj, vt, B, S):
    nb = S // MOBA_BLOCK
    L = MOBA_BLOCK
    return pl.pallas_call(
        functools.partial(_moba_kernel, nb=nb),
        grid=(B, nb),
        in_specs=[pl.BlockSpec((L, MOBA_WIDTH), lambda b, i: (b * nb + i, 0)),
                  pl.BlockSpec((S, MOBA_WIDTH), lambda b, i: (b, 1)),
                  pl.BlockSpec((nb, MOBA_HEADS, MOBA_VROWS, L), lambda b, i: (b, 0, 0, 0))],
        out_specs=pl.BlockSpec((1, MOBA_WIDTH, L), lambda b, i: (b, 0, i)),
        out_shape=jax.ShapeDtypeStruct((B, MOBA_WIDTH, S), F32),
        scratch_shapes=[pltpu.VMEM((nb, MOBA_WIDTH), F32),
                        pltpu.VMEM((nb, MOBA_HEADS * L), F32),
                        pltpu.VMEM((MOBA_HEADS // (LANES // MOBA_HEAD_DIM), (LANES // MOBA_HEAD_DIM) * L, LANES),
                                   BF16),
                        pltpu.VMEM((MOBA_HEADS, MOBA_VROWS, L), F32),
                        pltpu.VMEM((SUBLANES, MOBA_HEADS * L), F32),
                        pltpu.VMEM((L, MOBA_HEADS * L), F32),
                        pltpu.VMEM((L, MOBA_HEADS * L), F32)],
        compiler_params=_params(("parallel", "arbitrary")),
        name="moba",
    )(proj, proj, vt)


def _ret_log_g():
    return jnp.log(1.0 - 2.0 ** (-5.0 - jnp.arange(RET_HEADS, dtype=F32)))


def _retention_kernel(q_ref, k_ref, v_ref, g_ref, dec_ref, qd_ref, kd_ref, cd_ref, o_ref, state_ref):
    n = pl.program_id(1)

    @pl.when(n == 0)
    def _():
        state_ref[...] = jnp.zeros_like(state_ref)

    for h in range(RET_HEADS):
        qk = slice(h * RET_QK_DIM, (h + 1) * RET_QK_DIM)
        vv = slice(h * RET_V_DIM, (h + 1) * RET_V_DIM)
        q = q_ref[:, qk]
        k = k_ref[:, qk]
        v = v_ref[:, vv]
        scores = lax.dot_general(q, k, _NT, preferred_element_type=F32) * dec_ref[h]
        y = jnp.dot(scores.astype(BF16), v, preferred_element_type=F32)
        state = state_ref[h]
        y = y + jnp.dot(q, state.astype(BF16), preferred_element_type=F32) * qd_ref[h]
        kt = (k.astype(F32) * kd_ref[h]).T.astype(BF16)
        kv = jnp.dot(kt, v, preferred_element_type=F32)
        state_ref[h] = state * cd_ref[h:h + 1, :] + kv
        mu = jnp.mean(y, axis=-1, keepdims=True)
        yc = y - mu
        var = jnp.mean(yc * yc, axis=-1, keepdims=True)
        yn = yc * lax.rsqrt(var + LN_EPS)
        gate = g_ref[:, vv].astype(F32)
        o_ref[:, vv] = (gate * jax.nn.sigmoid(gate) * yn).astype(o_ref.dtype)


def _retention(proj, B, S):
    C = RET_CHUNK
    nc = S // C
    T = B * S
    log_g = _ret_log_g()
    pos = jnp.arange(C, dtype=F32)
    diff = pos[:, None] - pos[None, :]
    scale = RET_QK_DIM ** -0.5
    decay = jnp.where(diff >= 0, jnp.exp(log_g[:, None, None] * jnp.maximum(diff, 0.0)), 0.0) * scale
    q_decay = jnp.broadcast_to(jnp.exp(log_g[:, None] * (pos + 1.0))[:, :, None], (RET_HEADS, C, RET_V_DIM))
    k_decay = jnp.broadcast_to((jnp.exp(log_g[:, None] * (C - 1.0 - pos)) * scale)[:, :, None],
                               (RET_HEADS, C, RET_QK_DIM))
    chunk_decay = jnp.broadcast_to(jnp.exp(log_g * C)[:, None], (RET_HEADS, RET_V_DIM))
    const = lambda shape: pl.BlockSpec(shape, lambda b, n: (0,) * len(shape))
    return pl.pallas_call(
        _retention_kernel,
        grid=(B, nc),
        in_specs=[pl.BlockSpec((C, RET_QK_WIDTH), lambda b, n: (b * nc + n, 2)),
                  pl.BlockSpec((C, RET_QK_WIDTH), lambda b, n: (b * nc + n, 3)),
                  pl.BlockSpec((C, RET_V_WIDTH), lambda b, n: (b * nc + n, 2)),
                  pl.BlockSpec((C, RET_V_WIDTH), lambda b, n: (b * nc + n, 3)),
                  const((RET_HEADS, C, C)), const((RET_HEADS, C, RET_V_DIM)),
                  const((RET_HEADS, C, RET_QK_DIM)), const((RET_HEADS, RET_V_DIM))],
        out_specs=pl.BlockSpec((C, RET_V_WIDTH), lambda b, n: (b * nc + n, 0)),
        out_shape=jax.ShapeDtypeStruct((T, RET_V_WIDTH), BF16),
        scratch_shapes=[pltpu.VMEM((RET_HEADS, RET_QK_DIM, RET_V_DIM), F32)],
        compiler_params=_params(("parallel", "arbitrary")),
        name="retention",
    )(proj, proj, proj, proj, decay, q_decay, k_decay, chunk_decay)


def _layer_norm(y, g, b):
    mu = jnp.mean(y, axis=-1, keepdims=True)
    yc = y - mu
    var = jnp.mean(yc * yc, axis=-1, keepdims=True)
    return yc * lax.rsqrt(var + LN_EPS) * g + b


def _merge_kernel(yat_ref, yr_ref, ga_ref, gr_ref, x_ref, wa_ref, wr_ref, wo_ref, g_ref, b_ref, o_ref, ob_ref):
    ya = yat_ref[0].T.astype(BF16)
    branch_a = jnp.dot(ya, wa_ref[...], preferred_element_type=F32)
    branch_r = jnp.dot(yr_ref[...], wr_ref[...], preferred_element_type=F32)
    merged = (jax.nn.sigmoid(ga_ref[...].astype(F32)) * branch_a
              + jax.nn.sigmoid(gr_ref[...].astype(F32)) * branch_r)
    mix = jnp.dot(merged.astype(BF16), wo_ref[...], preferred_element_type=F32)
    y = _layer_norm(DN_ALPHA * x_ref[...] + mix, g_ref[...], b_ref[...])
    o_ref[...] = y
    ob_ref[...] = y.astype(BF16)


def _merge(yat, yr, proj, x, wa, wr, wo, g, b, B, S):
    L = MOBA_BLOCK
    nb = S // L
    T = B * S
    tok = lambda c: pl.BlockSpec((L, D_MODEL), lambda bb, i: (bb * nb + i, c))
    const = lambda shape: pl.BlockSpec(shape, lambda bb, i: (0,) * len(shape))
    return pl.pallas_call(
        _merge_kernel,
        grid=(B, nb),
        in_specs=[pl.BlockSpec((1, MOBA_WIDTH, L), lambda bb, i: (bb, 0, i)),
                  tok(0), tok(4), tok(5), tok(0),
                  const((MOBA_WIDTH, D_MODEL)), const((RET_V_WIDTH, D_MODEL)), const((D_MODEL, D_MODEL)),
                  const((1, D_MODEL)), const((1, D_MODEL))],
        out_specs=[tok(0), tok(0)],
        out_shape=[jax.ShapeDtypeStruct((T, D_MODEL), F32), jax.ShapeDtypeStruct((T, D_MODEL), BF16)],
        compiler_params=_params(("parallel", "parallel")),
        name="merge_ln1",
    )(yat, yr, proj, proj, x, wa, wr, wo, g, b)


def _cmpx(xs, a, b):
    if xs[b] is None:
        return
    if xs[a] is None:
        xs[a], xs[b] = xs[b], None
        return
    hi = jnp.maximum(xs[a], xs[b])
    lo = jnp.minimum(xs[a], xs[b])
    xs[a], xs[b] = hi, lo


def _bitonic_merge_desc(xs):
    n = len(xs)
    j = n // 2
    while j >= 1:
        for a in range(n):
            b = a ^ j
            if b > a:
                _cmpx(xs, a, b)
        j //= 2
    return xs


def _sort_desc(xs):
    xs = list(xs)
    n = len(xs)
    k = 2
    while k <= n:
        j = k // 2
        while j >= 1:
            for a in range(n):
                b = a ^ j
                if b > a:
                    if (a & k) == 0:
                        _cmpx(xs, a, b)
                    else:
                        _cmpx(xs, b, a)
            j //= 2
        k *= 2
    return xs


def _max_or_none(x, y):
    if x is None:
        return y
    if y is None:
        return x
    return jnp.maximum(x, y)


def _merge_top(xs, ys):
    n = len(xs)
    return _bitonic_merge_desc([_max_or_none(xs[v], ys[n - 1 - v]) for v in range(n)])


def _top_desc(rows, k):
    lists = [_sort_desc(rows[g:g + k]) for g in range(0, len(rows), k)]
    while len(lists) > 1:
        lists = [_merge_top(lists[g], lists[g + 1]) for g in range(0, len(lists), 2)]
    return lists[0]


def _route_kernel(x_ref, wq_ref, sk_ref, r1_ref, e1_ref, n0_ref, c0_ref, sc_ref, tmp_ref):
    half_dim = PEER_KEY_DIM // 2
    for sub in range(x_ref.shape[0] // LANES):
        toks = slice(sub * LANES, (sub + 1) * LANES)
        q = jnp.dot(x_ref[toks, :], wq_ref[...], preferred_element_type=F32).astype(BF16)
        for p in range(2):
            for h in range(PEER_HEADS):
                qhp = q[:, (2 * h + p) * half_dim:(2 * h + p + 1) * half_dim]
                sc_ref[sub, p, pl.ds(h, PEER_N_KEYS, stride=PEER_HEADS), :] = lax.dot_general(
                    sk_ref[p], qhp, _NT, preferred_element_type=F32)
        _route_tokens(sc_ref.at[sub], tmp_ref.at[sub], toks, r1_ref, e1_ref, n0_ref, c0_ref)


def _route_tokens(sc_ref, tmp_ref, toks, r1_ref, e1_ref, n0_ref, c0_ref):
    K = PEER_TOPK
    nk = PEER_N_KEYS
    H = PEER_HEADS
    tm = LANES
    inf = jnp.inf
    s0 = sc_ref[0].reshape(nk, H, tm)
    s1 = sc_ref[1].reshape(nk, H, tm)
    a = _top_desc([s0[k] for k in range(nk)], K)
    b = _top_desc([s1[k] for k in range(nk)], K)

    sums = [[a[r] + b[c] for c in range(K // (r + 1))] for r in range(K)]
    pad = lambda xs: xs + [None] * (K - len(xs))
    z = _merge_top(sums[0], pad(sums[1]))
    mid = [x for r in range(2, 7) for x in sums[r]]
    z = _merge_top(z, _sort_desc(pad(mid)))
    low = [x for r in range(7, K) for x in sums[r]]
    z = _merge_top(z, _sort_desc(pad(low)))
    tau = z[K - 1]
    zsum = jnp.ones_like(tau)
    for v in range(1, K):
        zsum = zsum + jnp.exp(z[v] - z[0])
    inv_z = 1.0 / zsum

    alphas = []
    for c in range(K):
        alpha = None
        for r in range(K // (c + 1)):
            cand = jnp.where(sums[r][c] >= tau, a[r], inf)
            alpha = cand if alpha is None else jnp.minimum(alpha, cand)
        alphas.append(alpha)

    for k0 in range(0, nk, ROUTE_KEY_CHUNK):
        blk0 = s0[k0:k0 + ROUTE_KEY_CHUNK]
        blk1 = s1[k0:k0 + ROUTE_KEY_CHUNK]
        count = jnp.zeros(blk0.shape, F32)
        for c in range(K):
            count = jnp.where(blk0 >= alphas[c][None], float(c + 1), count)
        rank = jnp.full(blk1.shape, float(K), F32)
        for c in reversed(range(K)):
            rank = jnp.where(blk1 >= b[c][None], float(c), rank)
        e1 = jnp.exp(blk1 - b[0][None])
        c0 = jnp.exp(blk0 - a[0][None]) * inv_z[None]
        rows = slice(k0 * H, (k0 + ROUTE_KEY_CHUNK) * H)
        for n, val in enumerate((rank, e1, count, c0)):
            tmp_ref[n, rows, :] = val.reshape(ROUTE_KEY_CHUNK * H, tm)

    for n, dst in enumerate((r1_ref, e1_ref, n0_ref, c0_ref)):
        for h in range(H):
            dst[h, :, toks] = tmp_ref[n, pl.ds(h, nk, stride=H), :].astype(dst.dtype)


def _route(xb, wq, sk):
    T = xb.shape[0]
    tm = ROUTE_TM
    H = PEER_HEADS
    nk = PEER_N_KEYS
    subs = tm // LANES
    spec = pl.BlockSpec((H, nk, tm), lambda t: (0, 0, t))
    half = jax.ShapeDtypeStruct((H, nk, T), BF16)
    word = jax.ShapeDtypeStruct((H, nk, T), F32)
    return pl.pallas_call(
        _route_kernel,
        grid=(T // tm,),
        in_specs=[pl.BlockSpec((tm, D_MODEL), lambda t: (t, 0)),
                  pl.BlockSpec(wq.shape, lambda t: (0, 0)),
                  pl.BlockSpec(sk.shape, lambda t: (0, 0, 0))],
        out_specs=[spec, spec, spec, spec],
        out_shape=[half, half, word, word],
        scratch_shapes=[pltpu.VMEM((subs, 2, nk * H, LANES), F32),
                        pltpu.VMEM((subs, 4, nk * H, LANES), F32)],
        compiler_params=_params(("parallel",)),
        name="peer_route",
    )(xb, wq, sk)


def _gelu_tanh(x):
    k = -2.0 * math.sqrt(2.0 / math.pi) * math.log2(math.e)
    z = x * (k + (k * 0.044715) * (x * x))
    return x / (1.0 + jnp.exp2(z))


def _peer_kernel(xb_ref, u_ref, v_ref, r1_ref, e1_ref, n0_ref, c0_ref, x_ref, g_ref, b_ref,
                 o_ref, ob_ref, w_ref, act_ref, acc_ref, *, rows):
    e = pl.program_id(1)
    nk = PEER_N_KEYS
    tm = xb_ref.shape[0]
    pk = 2 * SUBLANES

    @pl.when(e == 0)
    def _():
        acc_ref[...] = jnp.zeros_like(acc_ref)

    tc = min(PEER_LANE_CHUNK, tm)

    def routing_weights(r):
        i = e * rows + r
        first = None
        for ch in range(tm // tc):
            ls = slice(ch * tc, (ch + 1) * tc)
            w = [None] * (nk // pk)
            for h in range(PEER_HEADS):
                count = jnp.broadcast_to(n0_ref[h, pl.ds(i, 1), ls], (pk, tc)).astype(BF16)
                weight = jnp.broadcast_to(c0_ref[h, pl.ds(i, 1), ls], (pk, tc)).astype(BF16)
                for g in range(nk // pk):
                    rs = slice(g * pk, (g + 1) * pk)
                    sel = jnp.where(r1_ref[h, rs, ls] < count, e1_ref[h, rs, ls], jnp.zeros((pk, tc), BF16))
                    term = sel * weight
                    w[g] = term if w[g] is None else w[g] + term
            for g in range(nk // pk):
                w_ref[r * nk + g * pk:r * nk + (g + 1) * pk, ls] = w[g]
            if first is None:
                first = pltpu.bitcast(w[0][:, 0:LANES], jnp.uint32)
        return first

    cr = PEER_CHUNK_ROWS
    zero = 0
    for c in range(rows // cr):
        bits = routing_weights(c * cr)
        for r in range(c * cr + 1, (c + 1) * cr):
            routing_weights(r)
        lo = 0 if c == 0 else pl.multiple_of(c * cr * nk + zero, cr * nk)
        ht = lax.dot_general(u_ref[pl.ds(lo, cr * nk), :], xb_ref[...], _NT,
                             preferred_element_type=F32)
        rows_c = slice(c * cr * nk, (c + 1) * cr * nk)
        act_ref[rows_c, :] = _gelu_tanh(ht.astype(BF16)) * w_ref[rows_c, :]
        zero = ((bits >> 16) >> 16)[0, 0].astype(jnp.int32)
    acc_ref[...] += lax.dot_general(v_ref[...], act_ref[...], (((0,), (0,)), ((), ())),
                                    preferred_element_type=F32)

    @pl.when(e == pl.num_programs(1) - 1)
    def _():
        y = _layer_norm(DN_ALPHA * x_ref[...] + acc_ref[...].T, g_ref[...], b_ref[...])
        o_ref[...] = y
        ob_ref[...] = y.astype(BF16)


def _peer(xb, x, u, vt, routing, g, b):
    T = xb.shape[0]
    tm = min(PEER_TM, T)
    rows = PEER_ROWS
    nk = PEER_N_KEYS
    rspec = pl.BlockSpec((PEER_HEADS, nk, tm), lambda t, e: (0, 0, t))
    tok = pl.BlockSpec((tm, D_MODEL), lambda t, e: (t, 0))
    const = pl.BlockSpec((1, D_MODEL), lambda t, e: (0, 0))
    return pl.pallas_call(
        functools.partial(_peer_kernel, rows=rows),
        grid=(T // tm, nk // rows),
        in_specs=[tok,
                  pl.BlockSpec((rows * nk, D_MODEL), lambda t, e: (e, 0)),
                  pl.BlockSpec((rows * nk, D_MODEL), lambda t, e: (e, 0)),
                  rspec, rspec, rspec, rspec, tok, const, const],
        out_specs=[tok, tok],
        out_shape=[jax.ShapeDtypeStruct((T, D_MODEL), F32), jax.ShapeDtypeStruct((T, D_MODEL), BF16)],
        scratch_shapes=[pltpu.VMEM((rows * nk, tm), BF16),
                        pltpu.VMEM((rows * nk, tm), BF16),
                        pltpu.VMEM((D_MODEL, tm), F32)],
        compiler_params=_params(("parallel", "arbitrary"), PEER_SCHED_FLAGS),
        name="peer_dense",
    )(xb, u, vt, *routing, x, g, b)


def _rope_tables(S):
    pos = jnp.arange(S, dtype=F32)
    d = MOBA_HEAD_DIM
    inv = ROPE_THETA ** (-jnp.arange(0, d, 2, dtype=F32) / d)
    ang = pos[:, None] * inv[None, :]
    cos, sin = jnp.cos(ang), jnp.sin(ang)
    reps = LANES // d
    cos_m = jnp.tile(jnp.concatenate([cos, cos], axis=1), (1, reps))
    sin_m = jnp.tile(jnp.concatenate([-sin, sin], axis=1), (1, reps))
    d = RET_QK_DIM
    inv = 1.0 / (ROPE_THETA ** jnp.linspace(0.0, 1.0, d // 2, dtype=F32))
    ang = pos[:, None] * inv[None, :]
    cos, sin = jnp.cos(ang), jnp.sin(ang)
    cos_r = jnp.concatenate([cos, cos], axis=1)
    sin_r = jnp.concatenate([-sin, sin], axis=1)
    return cos_m, sin_m, cos_r, sin_r


def _ret_column_perm():
    within = np.concatenate([np.arange(0, RET_QK_DIM, 2), np.arange(1, RET_QK_DIM, 2)])
    return np.concatenate([h * RET_QK_DIM + within for h in range(RET_HEADS)])


def kernel(x, w_in, w_moba_out, w_ret_out, w_out, ln1_g, ln1_b, peer_w_query, peer_sub_keys,
           peer_u, peer_v, ln2_g, ln2_b):
    B, S, D = x.shape
    assert D == D_MODEL and S % MOBA_BLOCK == 0 and S % RET_CHUNK == 0
    T = B * S
    tabs = _rope_tables(S)
    perm = _ret_column_perm()
    o = IN_OFFSETS
    xf = x.reshape(T, D).astype(F32)
    xb = xf.astype(BF16)
    for l in range(DEPTH):
        w = w_in[l]
        w_main = jnp.concatenate(
            [w[:, o[0]:o[2]], w[:, o[3]:o[4]][:, perm], w[:, o[4]:o[5]][:, perm], w[:, o[5]:o[9]]],
            axis=1).astype(BF16)
        w_vt = w[:, o[2]:o[3]].T.astype(BF16)
        proj = _inproj(xb, w_main, tabs, S)
        vt = _moba_values_t(xb, w_vt, S)
        yat = _moba(proj, vt, B, S)
        yr = _retention(proj, B, S)
        xf, xb = _merge(yat, yr, proj, xf, w_moba_out[l].astype(BF16), w_ret_out[l].astype(BF16),
                        w_out[l].astype(BF16), ln1_g[l].reshape(1, D), ln1_b[l].reshape(1, D), B, S)
        routing = _route(xb, peer_w_query[l].astype(BF16), peer_sub_keys[l].astype(BF16))
        xf, xb = _peer(xb, xf, peer_u[l].astype(BF16), peer_v[l].astype(BF16), routing,
                       ln2_g[l].reshape(1, D), ln2_b[l].reshape(1, D))
    return xf.reshape(B, S, D).astype(x.dtype)
```

```python
import functools
import math

import numpy as np
import jax
import jax.numpy as jnp
from jax import lax
from jax.experimental import pallas as pl
from jax.experimental.pallas import tpu as pltpu

F32 = jnp.float32
BF16 = jnp.bfloat16

D_MODEL = 1024
DEPTH = 2
MOBA_HEADS = 8
MOBA_HEAD_DIM = 64
MOBA_WIDTH = MOBA_HEADS * MOBA_HEAD_DIM
MOBA_BLOCK = 256
MOBA_TOPK = 3
ROPE_THETA = 10000.0
RET_HEADS = 4
RET_QK_DIM = 128
RET_V_DIM = 256
RET_QK_WIDTH = RET_HEADS * RET_QK_DIM
RET_V_WIDTH = RET_HEADS * RET_V_DIM
RET_CHUNK = 256
PEER_N_KEYS = 128
PEER_N_EXPERTS = PEER_N_KEYS * PEER_N_KEYS
PEER_HEADS = 8
PEER_KEY_DIM = 256
PEER_TOPK = 16
DN_ALPHA = (2.0 * DEPTH) ** 0.25
LN_EPS = 1e-5
NEG = -1e30

IN_SIZES = (MOBA_WIDTH, MOBA_WIDTH, MOBA_WIDTH, RET_QK_WIDTH, RET_QK_WIDTH,
            RET_V_WIDTH, RET_V_WIDTH, D_MODEL, D_MODEL)
IN_OFFSETS = tuple(int(v) for v in np.concatenate([[0], np.cumsum(IN_SIZES)]))

LANES = 128
SUBLANES = 8
VMEM_LIMIT = 56 * 1024 * 1024

PROJ_TM = 1024
PROJ_TN = 1024
MERGE_TM = 512
ROUTE_TM = 512
ROUTE_KEY_CHUNK = 4
PEER_TM = 512
PEER_ROWS = 16
PEER_LANE_CHUNK = 256
PEER_CHUNK_ROWS = 2
PEER_SCHED_FLAGS = None

_NT = (((1,), (1,)), ((), ()))


def _params(sem, flags=None):
    return pltpu.CompilerParams(dimension_semantics=sem, vmem_limit_bytes=VMEM_LIMIT, flags=flags)


def _rotate_groups(acc, cos, sin, o_ref, partner_fn):
    for g in range(PROJ_TN // LANES):
        xg = acc[:, g * LANES:(g + 1) * LANES]
        o_ref[:, g * LANES:(g + 1) * LANES] = (xg * cos + partner_fn(xg) * sin).astype(o_ref.dtype)


def _inproj_kernel(x_ref, w_ref, cm_ref, sm_ref, cr_ref, sr_ref, o_ref):
    j = pl.program_id(1)
    acc = jnp.dot(x_ref[...], w_ref[...], preferred_element_type=F32)
    tm = acc.shape[0]

    moba_tiles = 2 * MOBA_WIDTH // PROJ_TN
    rope_tiles = moba_tiles + 2 * RET_QK_WIDTH // PROJ_TN

    @pl.when(j < moba_tiles)
    def _():
        lane = lax.broadcasted_iota(jnp.int32, (tm, LANES), 1)
        first = (lane % MOBA_HEAD_DIM) < (MOBA_HEAD_DIM // 2)

        def partner(xg):
            return jnp.where(first, pltpu.roll(xg, LANES - MOBA_HEAD_DIM // 2, axis=1),
                             pltpu.roll(xg, MOBA_HEAD_DIM // 2, axis=1))

        _rotate_groups(acc, cm_ref[...], sm_ref[...], o_ref, partner)

    @pl.when((j >= moba_tiles) & (j < rope_tiles))
    def _():
        _rotate_groups(acc, cr_ref[...], sr_ref[...], o_ref,
                       lambda xg: pltpu.roll(xg, RET_QK_DIM // 2, axis=1))

    @pl.when(j >= rope_tiles)
    def _():
        o_ref[...] = acc.astype(o_ref.dtype)


def _inproj(xb, w_main, tabs, S):
    T = xb.shape[0]
    tm = min(PROJ_TM, S)
    n_col = w_main.shape[1] // PROJ_TN
    pos_blocks = S // tm
    tab_spec = pl.BlockSpec((tm, LANES), lambda i, j: (i % pos_blocks, 0))
    return pl.pallas_call(
        _inproj_kernel,
        grid=(T // tm, n_col),
        in_specs=[pl.BlockSpec((tm, D_MODEL), lambda i, j: (i, 0)),
                  pl.BlockSpec((D_MODEL, PROJ_TN), lambda i, j: (0, j)),
                  tab_spec, tab_spec, tab_spec, tab_spec],
        out_specs=pl.BlockSpec((tm, PROJ_TN), lambda i, j: (i, j)),
        out_shape=jax.ShapeDtypeStruct((T, w_main.shape[1]), BF16),
        compiler_params=_params(("parallel", "arbitrary")),
        name="inproj",
    )(xb, w_main, *tabs)


MOBA_VROWS = MOBA_HEAD_DIM + 2 * SUBLANES


def _vt_kernel(x_ref, w_ref, o_ref):
    res = lax.dot_general(w_ref[...], x_ref[...], _NT, preferred_element_type=F32)
    hd = MOBA_HEAD_DIM
    pad = MOBA_VROWS - hd
    ones_row = jnp.where(lax.broadcasted_iota(jnp.int32, (pad, MOBA_BLOCK), 0) == 0, 1.0, 0.0).astype(o_ref.dtype)
    for c in range(o_ref.shape[0]):
        for h in range(MOBA_HEADS):
            o_ref[c, h, 0:hd, :] = res[h * hd:(h + 1) * hd, c * MOBA_BLOCK:(c + 1) * MOBA_BLOCK].astype(o_ref.dtype)
            o_ref[c, h, hd:MOBA_VROWS, :] = ones_row


def _moba_values_t(xb, w_vt, S):
    T = xb.shape[0]
    tm = min(PROJ_TM, S)
    per = tm // MOBA_BLOCK
    return pl.pallas_call(
        _vt_kernel,
        grid=(T // tm,),
        in_specs=[pl.BlockSpec((tm, D_MODEL), lambda i: (i, 0)),
                  pl.BlockSpec((MOBA_WIDTH, D_MODEL), lambda i: (0, 0))],
        out_specs=pl.BlockSpec((per, MOBA_HEADS, MOBA_VROWS, MOBA_BLOCK), lambda i: (i, 0, 0, 0)),
        out_shape=jax.ShapeDtypeStruct((T // MOBA_BLOCK, MOBA_HEADS, MOBA_VROWS, MOBA_BLOCK), BF16),
        compiler_params=_params(("parallel",)),
        name="moba_vt",
    )(xb, w_vt)


def _moba_kernel(q_ref, k_ref, vt_ref, o_ref, kmean_ref, bias_ref, qs_ref, acc_ref, m_ref, sa_ref, sb_ref, *, nb):
    i = pl.program_id(1)
    L = MOBA_BLOCK
    hd = MOBA_HEAD_DIM

    @pl.when(i == 0)
    def _():
        def body(j, c):
            kb = k_ref[pl.ds(pl.multiple_of(j * L, L), L), :].astype(F32)
            kmean_ref[pl.ds(j, 1), :] = jnp.sum(kb, axis=0, keepdims=True) * (1.0 / L)
            return c
        lax.fori_loop(0, nb, body, 0)

    per = LANES // hd
    groups = MOBA_HEADS // per
    W = MOBA_HEADS * L
    lane = lax.broadcasted_iota(jnp.int32, (L, LANES), 1)
    row0 = pl.multiple_of(i * L, L)

    gates = []
    for g in range(groups):
        cols = slice(g * LANES, (g + 1) * LANES)
        q_pair = q_ref[:, cols]
        km = kmean_ref[:, cols].astype(BF16)
        for hh in range(per):
            qm = jnp.where((lane // hd) == hh, q_pair, jnp.zeros_like(q_pair))
            gates.append(lax.dot_general(km, qm, _NT, preferred_element_type=F32))
            qs_ref[g, hh * L:(hh + 1) * L, :] = (qm.astype(F32) * (hd ** -0.5 * math.log2(math.e))).astype(BF16)

    blk = lax.broadcasted_iota(jnp.int32, (nb, W), 0)
    blkf = blk.astype(F32)
    gm = jnp.where(blk < i, jnp.concatenate(gates, axis=1), -jnp.inf)
    keep = jnp.zeros((nb, W), F32)
    for _ in range(MOBA_TOPK):
        best = jnp.max(gm, axis=0, keepdims=True)
        first = jnp.min(jnp.where(gm == best, blkf, float(nb)), axis=0, keepdims=True)
        hit = blkf == first
        keep = jnp.where(hit, 1.0, keep)
        gm = jnp.where(hit, -jnp.inf, gm)
    bias_ref[...] = jnp.where(blk < i, jnp.where(keep > 0.0, 0.0, NEG), NEG)

    def scores(r):
        return jnp.concatenate(
            [lax.dot_general(k_ref[pl.ds(r, L), g * LANES:(g + 1) * LANES], qs_ref[g], _NT,
                             preferred_element_type=F32) for g in range(groups)], axis=1)

    def values(j, p):
        pb = p.astype(BF16)
        return [jnp.dot(vt_ref[j, h], pb[:, h * L:(h + 1) * L], preferred_element_type=F32)
                for h in range(MOBA_HEADS)]

    kpos = lax.broadcasted_iota(jnp.int32, (L, W), 0)
    qpos = lax.broadcasted_iota(jnp.int32, (L, W), 1) % L
    s = jnp.where(kpos <= qpos, scores(row0), NEG)
    m = jnp.max(s, axis=0, keepdims=True)
    m_ref[...] = jnp.broadcast_to(m, (SUBLANES, W))
    for h, pv in enumerate(values(i, jnp.exp2(s - m))):
        acc_ref[h] = pv

    def scores_into(dst_ref, r):
        first = None
        for g in range(groups):
            sg = lax.dot_general(k_ref[pl.ds(r, L), g * LANES:(g + 1) * LANES], qs_ref[g], _NT,
                                 preferred_element_type=F32)
            dst_ref[:, g * per * L:(g + 1) * per * L] = sg
            if first is None:
                first = pltpu.bitcast(sg[0:SUBLANES, 0:LANES], jnp.uint32)
        return ((first >> 16) >> 16)[0, 0].astype(jnp.int32)

    def softmax_update(j, cur_ref, heads, row_start):
        lanes = slice(heads[0] * L, (heads[-1] + 1) * L)
        s = cur_ref[pl.ds(row_start, L), lanes]
        bias = bias_ref[pl.ds(j, 1), lanes]
        m_old = m_ref[:, lanes]
        m_new = jnp.maximum(m_old, jnp.max(s, axis=0, keepdims=True) + bias)
        alpha = jnp.exp2(m_old - m_new)
        m_ref[:, lanes] = m_new
        pb = jnp.exp2(s - (m_new[0:1, :] - bias)).astype(BF16)
        for n, h in enumerate(heads):
            pv = jnp.dot(vt_ref[j, h], pb[:, n * L:(n + 1) * L], preferred_element_type=F32)
            acc_ref[h] = alpha[0:1, n * L:(n + 1) * L] * acc_ref[h] + pv

    half = MOBA_HEADS // 2

    def past(j, cur_ref, nxt_ref):
        start = 0
        if nxt_ref is not None:
            nxt = jnp.minimum(j + 1, i - 1)
            start = pl.multiple_of(scores_into(nxt_ref, pl.multiple_of(nxt * L, L)), L)
        softmax_update(j, cur_ref, tuple(range(half)), 0)
        softmax_update(j, cur_ref, tuple(range(half, MOBA_HEADS)), start)

    @pl.when(i > 0)
    def _():
        scores_into(sa_ref, 0)

    def four_past(jj, c):
        past(4 * jj, sa_ref, sb_ref)
        past(4 * jj + 1, sb_ref, sa_ref)
        past(4 * jj + 2, sa_ref, sb_ref)
        past(4 * jj + 3, sb_ref, sa_ref)
        return c

    lax.fori_loop(0, i // 4, four_past, 0)

    rest = i % 4
    base = i - rest

    @pl.when(rest >= 1)
    def _():
        past(base, sa_ref, sb_ref)

    @pl.when(rest >= 2)
    def _():
        past(base + 1, sb_ref, sa_ref)

    @pl.when(rest >= 3)
    def _():
        past(base + 2, sa_ref, None)

    for h in range(MOBA_HEADS):
        o_ref[0, h * hd:(h + 1) * hd, :] = acc_ref[h, 0:hd, :] / acc_ref[h, hd:hd + 1, :]


def _moba(proj, vt, B, S):
    nb = S // MOBA_BLOCK
    L = MOBA_BLOCK
    return pl.pallas_call(
        functools.partial(_moba_kernel, nb=nb),
        grid=(B, nb),
        in_specs=[pl.BlockSpec((L, MOBA_WIDTH), lambda b, i: (b * nb + i, 0)),
                  pl.BlockSpec((S, MOBA_WIDTH), lambda b, i: (b, 1)),
                  pl.BlockSpec((nb, MOBA_HEADS, MOBA_VROWS, L), lambda b, i: (b, 0, 0, 0))],
        out_specs=pl.BlockSpec((1, MOBA_WIDTH, L), lambda b, i: (b, 0, i)),
        out_shape=jax.ShapeDtypeStruct((B, MOBA_WIDTH, S), F32),
        scratch_shapes=[pltpu.VMEM((nb, MOBA_WIDTH), F32),
                        pltpu.VMEM((nb, MOBA_HEADS * L), F32),
                        pltpu.VMEM((MOBA_HEADS // (LANES // MOBA_HEAD_DIM), (LANES // MOBA_HEAD_DIM) * L, LANES),
                                   BF16),
                        pltpu.VMEM((MOBA_HEADS, MOBA_VROWS, L), F32),
                        pltpu.VMEM((SUBLANES, MOBA_HEADS * L), F32),
                        pltpu.VMEM((L, MOBA_HEADS * L), F32),
                        pltpu.VMEM((L, MOBA_HEADS * L), F32)],
        compiler_params=_params(("parallel", "arbitrary")),
        name="moba",
    )(proj, proj, vt)


def _ret_log_g():
    return jnp.log(1.0 - 2.0 ** (-5.0 - jnp.arange(RET_HEADS, dtype=F32)))


def _retention_kernel(q_ref, k_ref, v_ref, g_ref, dec_ref, qd_ref, kd_ref, cd_ref, o_ref, state_ref):
    n = pl.program_id(1)

    @pl.when(n == 0)
    def _():
        state_ref[...] = jnp.zeros_like(state_ref)

    for h in range(RET_HEADS):
        qk = slice(h * RET_QK_DIM, (h + 1) * RET_QK_DIM)
        vv = slice(h * RET_V_DIM, (h + 1) * RET_V_DIM)
        q = q_ref[:, qk]
        k = k_ref[:, qk]
        v = v_ref[:, vv]
        scores = lax.dot_general(q, k, _NT, preferred_element_type=F32) * dec_ref[h]
        y = jnp.dot(scores.astype(BF16), v, preferred_element_type=F32)
        state = state_ref[h]
        y = y + jnp.dot(q, state.astype(BF16), preferred_element_type=F32) * qd_ref[h]
        kt = (k.astype(F32) * kd_ref[h]).T.astype(BF16)
        kv = jnp.dot(kt, v, preferred_element_type=F32)
        state_ref[h] = state * cd_ref[h:h + 1, :] + kv
        mu = jnp.mean(y, axis=-1, keepdims=True)
        yc = y - mu
        var = jnp.mean(yc * yc, axis=-1, keepdims=True)
        yn = yc * lax.rsqrt(var + LN_EPS)
        gate = g_ref[:, vv].astype(F32)
        o_ref[:, vv] = (gate * jax.nn.sigmoid(gate) * yn).astype(o_ref.dtype)


def _retention(proj, B, S):
    C = RET_CHUNK
    nc = S // C
    T = B * S
    log_g = _ret_log_g()
    pos = jnp.arange(C, dtype=F32)
    diff = pos[:, None] - pos[None, :]
    scale = RET_QK_DIM ** -0.5
    decay = jnp.where(diff >= 0, jnp.exp(log_g[:, None, None] * jnp.maximum(diff, 0.0)), 0.0) * scale
    q_decay = jnp.broadcast_to(jnp.exp(log_g[:, None] * (pos + 1.0))[:, :, None], (RET_HEADS, C, RET_V_DIM))
    k_decay = jnp.broadcast_to((jnp.exp(log_g[:, None] * (C - 1.0 - pos)) * scale)[:, :, None],
                               (RET_HEADS, C, RET_QK_DIM))
    chunk_decay = jnp.broadcast_to(jnp.exp(log_g * C)[:, None], (RET_HEADS, RET_V_DIM))
    const = lambda shape: pl.BlockSpec(shape, lambda b, n: (0,) * len(shape))
    return pl.pallas_call(
        _retention_kernel,
        grid=(B, nc),
        in_specs=[pl.BlockSpec((C, RET_QK_WIDTH), lambda b, n: (b * nc + n, 2)),
                  pl.BlockSpec((C, RET_QK_WIDTH), lambda b, n: (b * nc + n, 3)),
                  pl.BlockSpec((C, RET_V_WIDTH), lambda b, n: (b * nc + n, 2)),
                  pl.BlockSpec((C, RET_V_WIDTH), lambda b, n: (b * nc + n, 3)),
                  const((RET_HEADS, C, C)), const((RET_HEADS, C, RET_V_DIM)),
                  const((RET_HEADS, C, RET_QK_DIM)), const((RET_HEADS, RET_V_DIM))],
        out_specs=pl.BlockSpec((C, RET_V_WIDTH), lambda b, n: (b * nc + n, 0)),
        out_shape=jax.ShapeDtypeStruct((T, RET_V_WIDTH), BF16),
        scratch_shapes=[pltpu.VMEM((RET_HEADS, RET_QK_DIM, RET_V_DIM), F32)],
        compiler_params=_params(("parallel", "arbitrary")),
        name="retention",
    )(proj, proj, proj, proj, decay, q_decay, k_decay, chunk_decay)


def _layer_norm(y, g, b):
    mu = jnp.mean(y, axis=-1, keepdims=True)
    yc = y - mu
    var = jnp.mean(yc * yc, axis=-1, keepdims=True)
    return yc * lax.rsqrt(var + LN_EPS) * g + b


def _merge_kernel(yat_ref, yr_ref, ga_ref, gr_ref, x_ref, wa_ref, wr_ref, wo_ref, g_ref, b_ref, o_ref, ob_ref):
    ya = yat_ref[0].T.astype(BF16)
    branch_a = jnp.dot(ya, wa_ref[...], preferred_element_type=F32)
    branch_r = jnp.dot(yr_ref[...], wr_ref[...], preferred_element_type=F32)
    merged = (jax.nn.sigmoid(ga_ref[...].astype(F32)) * branch_a
              + jax.nn.sigmoid(gr_ref[...].astype(F32)) * branch_r)
    mix = jnp.dot(merged.astype(BF16), wo_ref[...], preferred_element_type=F32)
    y = _layer_norm(DN_ALPHA * x_ref[...] + mix, g_ref[...], b_ref[...])
    o_ref[...] = y
    ob_ref[...] = y.astype(BF16)


def _merge(yat, yr, proj, x, wa, wr, wo, g, b, B, S):
    L = min(MERGE_TM, S)
    nb = S // L
    T = B * S
    tok = lambda c: pl.BlockSpec((L, D_MODEL), lambda bb, i: (bb * nb + i, c))
    const = lambda shape: pl.BlockSpec(shape, lambda bb, i: (0,) * len(shape))
    return pl.pallas_call(
        _merge_kernel,
        grid=(B, nb),
        in_specs=[pl.BlockSpec((1, MOBA_WIDTH, L), lambda bb, i: (bb, 0, i)),
                  tok(0), tok(4), tok(5), tok(0),
                  const((MOBA_WIDTH, D_MODEL)), const((RET_V_WIDTH, D_MODEL)), const((D_MODEL, D_MODEL)),
                  const((1, D_MODEL)), const((1, D_MODEL))],
        out_specs=[tok(0), tok(0)],
        out_shape=[jax.ShapeDtypeStruct((T, D_MODEL), F32), jax.ShapeDtypeStruct((T, D_MODEL), BF16)],
        compiler_params=_params(("parallel", "parallel")),
        name="merge_ln1",
    )(yat, yr, proj, proj, x, wa, wr, wo, g, b)


def _cmpx(xs, a, b):
    if xs[b] is None:
        return
    if xs[a] is None:
        xs[a], xs[b] = xs[b], None
        return
    hi = jnp.maximum(xs[a], xs[b])
    lo = jnp.minimum(xs[a], xs[b])
    xs[a], xs[b] = hi, lo


def _bitonic_merge_desc(xs):
    n = len(xs)
    j = n // 2
    while j >= 1:
        for a in range(n):
            b = a ^ j
            if b > a:
                _cmpx(xs, a, b)
        j //= 2
    return xs


def _sort_desc(xs):
    xs = list(xs)
    n = len(xs)
    k = 2
    while k <= n:
        j = k // 2
        while j >= 1:
            for a in range(n):
                b = a ^ j
                if b > a:
                    if (a & k) == 0:
                        _cmpx(xs, a, b)
                    else:
                        _cmpx(xs, b, a)
            j //= 2
        k *= 2
    return xs


def _max_or_none(x, y):
    if x is None:
        return y
    if y is None:
        return x
    return jnp.maximum(x, y)


def _merge_top(xs, ys):
    n = len(xs)
    return _bitonic_merge_desc([_max_or_none(xs[v], ys[n - 1 - v]) for v in range(n)])


def _top_desc(rows, k):
    lists = [_sort_desc(rows[g:g + k]) for g in range(0, len(rows), k)]
    while len(lists) > 1:
        lists = [_merge_top(lists[g], lists[g + 1]) for g in range(0, len(lists), 2)]
    return lists[0]


def _route_kernel(x_ref, wq_ref, sk_ref, r1_ref, e1_ref, n0_ref, c0_ref, sc_ref, tmp_ref):
    half_dim = PEER_KEY_DIM // 2
    for sub in range(x_ref.shape[0] // LANES):
        toks = slice(sub * LANES, (sub + 1) * LANES)
        q = jnp.dot(x_ref[toks, :], wq_ref[...], preferred_element_type=F32).astype(BF16)
        for p in range(2):
            for h in range(PEER_HEADS):
                qhp = q[:, (2 * h + p) * half_dim:(2 * h + p + 1) * half_dim]
                sc_ref[sub, p, pl.ds(h, PEER_N_KEYS, stride=PEER_HEADS), :] = lax.dot_general(
                    sk_ref[p], qhp, _NT, preferred_element_type=F32)
        _route_tokens(sc_ref.at[sub], tmp_ref.at[sub], toks, r1_ref, e1_ref, n0_ref, c0_ref)


def _route_tokens(sc_ref, tmp_ref, toks, r1_ref, e1_ref, n0_ref, c0_ref):
    K = PEER_TOPK
    nk = PEER_N_KEYS
    H = PEER_HEADS
    tm = LANES
    inf = jnp.inf
    s0 = sc_ref[0].reshape(nk, H, tm)
    s1 = sc_ref[1].reshape(nk, H, tm)
    a = _top_desc([s0[k] for k in range(nk)], K)
    b = _top_desc([s1[k] for k in range(nk)], K)

    sums = [[a[r] + b[c] for c in range(K // (r + 1))] for r in range(K)]
    pad = lambda xs: xs + [None] * (K - len(xs))
    z = _merge_top(sums[0], pad(sums[1]))
    mid = [x for r in range(2, 7) for x in sums[r]]
    z = _merge_top(z, _sort_desc(pad(mid)))
    low = [x for r in range(7, K) for x in sums[r]]
    z = _merge_top(z, _sort_desc(pad(low)))
    tau = z[K - 1]
    zsum = jnp.ones_like(tau)
    for v in range(1, K):
        zsum = zsum + jnp.exp(z[v] - z[0])
    inv_z = 1.0 / zsum

    alphas = []
    for c in range(K):
        alpha = None
        for r in range(K // (c + 1)):
            cand = jnp.where(sums[r][c] >= tau, a[r], inf)
            alpha = cand if alpha is None else jnp.minimum(alpha, cand)
        alphas.append(alpha)

    for k0 in range(0, nk, ROUTE_KEY_CHUNK):
        blk0 = s0[k0:k0 + ROUTE_KEY_CHUNK]
        blk1 = s1[k0:k0 + ROUTE_KEY_CHUNK]
        count = jnp.zeros(blk0.shape, F32)
        for c in range(K):
            count = jnp.where(blk0 >= alphas[c][None], float(c + 1), count)
        rank = jnp.full(blk1.shape, float(K), F32)
        for c in reversed(range(K)):
            rank = jnp.where(blk1 >= b[c][None], float(c), rank)
        e1 = jnp.exp(blk1 - b[0][None])
        c0 = jnp.exp(blk0 - a[0][None]) * inv_z[None]
        rows = slice(k0 * H, (k0 + ROUTE_KEY_CHUNK) * H)
        for n, val in enumerate((rank, e1, count, c0)):
            tmp_ref[n, rows, :] = val.reshape(ROUTE_KEY_CHUNK * H, tm)

    for n, dst in enumerate((r1_ref, e1_ref, n0_ref, c0_ref)):
        for h in range(H):
            dst[h, :, toks] = tmp_ref[n, pl.ds(h, nk, stride=H), :].astype(dst.dtype)


def _route(xb, wq, sk):
    T = xb.shape[0]
    tm = ROUTE_TM
    H = PEER_HEADS
    nk = PEER_N_KEYS
    subs = tm // LANES
    spec = pl.BlockSpec((H, nk, tm), lambda t: (0, 0, t))
    half = jax.ShapeDtypeStruct((H, nk, T), BF16)
    word = jax.ShapeDtypeStruct((H, nk, T), F32)
    return pl.pallas_call(
        _route_kernel,
        grid=(T // tm,),
        in_specs=[pl.BlockSpec((tm, D_MODEL), lambda t: (t, 0)),
                  pl.BlockSpec(wq.shape, lambda t: (0, 0)),
                  pl.BlockSpec(sk.shape, lambda t: (0, 0, 0))],
        out_specs=[spec, spec, spec, spec],
        out_shape=[half, half, word, word],
        scratch_shapes=[pltpu.VMEM((subs, 2, nk * H, LANES), F32),
                        pltpu.VMEM((subs, 4, nk * H, LANES), F32)],
        compiler_params=_params(("parallel",)),
        name="peer_route",
    )(xb, wq, sk)


def _gelu_tanh(x):
    k = -2.0 * math.sqrt(2.0 / math.pi) * math.log2(math.e)
    z = x * (k + (k * 0.044715) * (x * x))
    return x / (1.0 + jnp.exp2(z))


def _peer_kernel(xb_ref, u_ref, vt_ref, r1_ref, e1_ref, n0_ref, c0_ref, x_ref, g_ref, b_ref,
                 o_ref, ob_ref, w_ref, act_ref, acc_ref, *, rows):
    e = pl.program_id(1)
    nk = PEER_N_KEYS
    tm = xb_ref.shape[0]
    pk = 2 * SUBLANES

    @pl.when(e == 0)
    def _():
        acc_ref[...] = jnp.zeros_like(acc_ref)

    tc = min(PEER_LANE_CHUNK, tm)

    def routing_weights(r):
        i = e * rows + r
        first = None
        for ch in range(tm // tc):
            ls = slice(ch * tc, (ch + 1) * tc)
            w = [None] * (nk // pk)
            for h in range(PEER_HEADS):
                count = jnp.broadcast_to(n0_ref[h, pl.ds(i, 1), ls], (pk, tc)).astype(BF16)
                weight = jnp.broadcast_to(c0_ref[h, pl.ds(i, 1), ls], (pk, tc)).astype(BF16)
                for g in range(nk // pk):
                    rs = slice(g * pk, (g + 1) * pk)
                    sel = jnp.where(r1_ref[h, rs, ls] < count, e1_ref[h, rs, ls], jnp.zeros((pk, tc), BF16))
                    term = sel * weight
                    w[g] = term if w[g] is None else w[g] + term
            for g in range(nk // pk):
                w_ref[r * nk + g * pk:r * nk + (g + 1) * pk, ls] = w[g]
            if first is None:
                first = pltpu.bitcast(w[0][:, 0:LANES], jnp.uint32)
        return first

    cr = PEER_CHUNK_ROWS
    zero = 0
    for c in range(rows // cr):
        bits = routing_weights(c * cr)
        for r in range(c * cr + 1, (c + 1) * cr):
            routing_weights(r)
        lo = 0 if c == 0 else pl.multiple_of(c * cr * nk + zero, cr * nk)
        ht = lax.dot_general(u_ref[pl.ds(lo, cr * nk), :], xb_ref[...], _NT,
                             preferred_element_type=F32)
        rows_c = slice(c * cr * nk, (c + 1) * cr * nk)
        act_ref[rows_c, :] = _gelu_tanh(ht.astype(BF16)) * w_ref[rows_c, :]
        zero = ((bits >> 16) >> 16)[0, 0].astype(jnp.int32)
    acc_ref[...] += jnp.dot(vt_ref[...], act_ref[...], preferred_element_type=F32)

    @pl.when(e == pl.num_programs(1) - 1)
    def _():
        y = _layer_norm(DN_ALPHA * x_ref[...] + acc_ref[...].T, g_ref[...], b_ref[...])
        o_ref[...] = y
        ob_ref[...] = y.astype(BF16)


def _peer(xb, x, u, vt, routing, g, b):
    T = xb.shape[0]
    tm = min(PEER_TM, T)
    rows = PEER_ROWS
    nk = PEER_N_KEYS
    rspec = pl.BlockSpec((PEER_HEADS, nk, tm), lambda t, e: (0, 0, t))
    tok = pl.BlockSpec((tm, D_MODEL), lambda t, e: (t, 0))
    const = pl.BlockSpec((1, D_MODEL), lambda t, e: (0, 0))
    return pl.pallas_call(
        functools.partial(_peer_kernel, rows=rows),
        grid=(T // tm, nk // rows),
        in_specs=[tok,
                  pl.BlockSpec((rows * nk, D_MODEL), lambda t, e: (e, 0)),
                  pl.BlockSpec((D_MODEL, rows * nk), lambda t, e: (0, e)),
                  rspec, rspec, rspec, rspec, tok, const, const],
        out_specs=[tok, tok],
        out_shape=[jax.ShapeDtypeStruct((T, D_MODEL), F32), jax.ShapeDtypeStruct((T, D_MODEL), BF16)],
        scratch_shapes=[pltpu.VMEM((rows * nk, tm), BF16),
                        pltpu.VMEM((rows * nk, tm), BF16),
                        pltpu.VMEM((D_MODEL, tm), F32)],
        compiler_params=_params(("parallel", "arbitrary"), PEER_SCHED_FLAGS),
        name="peer_dense",
    )(xb, u, vt, *routing, x, g, b)


def _rope_tables(S):
    pos = jnp.arange(S, dtype=F32)
    d = MOBA_HEAD_DIM
    inv = ROPE_THETA ** (-jnp.arange(0, d, 2, dtype=F32) / d)
    ang = pos[:, None] * inv[None, :]
    cos, sin = jnp.cos(ang), jnp.sin(ang)
    reps = LANES // d
    cos_m = jnp.tile(jnp.concatenate([cos, cos], axis=1), (1, reps))
    sin_m = jnp.tile(jnp.concatenate([-sin, sin], axis=1), (1, reps))
    d = RET_QK_DIM
    inv = 1.0 / (ROPE_THETA ** jnp.linspace(0.0, 1.0, d // 2, dtype=F32))
    ang = pos[:, None] * inv[None, :]
    cos, sin = jnp.cos(ang), jnp.sin(ang)
    cos_r = jnp.concatenate([cos, cos], axis=1)
    sin_r = jnp.concatenate([-sin, sin], axis=1)
    return cos_m, sin_m, cos_r, sin_r


def _ret_column_perm():
    within = np.concatenate([np.arange(0, RET_QK_DIM, 2), np.arange(1, RET_QK_DIM, 2)])
    return np.concatenate([h * RET_QK_DIM + within for h in range(RET_HEADS)])


def kernel(x, w_in, w_moba_out, w_ret_out, w_out, ln1_g, ln1_b, peer_w_query, peer_sub_keys,
           peer_u, peer_v, ln2_g, ln2_b):
    B, S, D = x.shape
    assert D == D_MODEL and S % MOBA_BLOCK == 0 and S % RET_CHUNK == 0
    T = B * S
    tabs = _rope_tables(S)
    perm = _ret_column_perm()
    o = IN_OFFSETS
    xf = x.reshape(T, D).astype(F32)
    xb = xf.astype(BF16)
    for l in range(DEPTH):
        w = w_in[l]
        w_main = jnp.concatenate(
            [w[:, o[0]:o[2]], w[:, o[3]:o[4]][:, perm], w[:, o[4]:o[5]][:, perm], w[:, o[5]:o[9]]],
            axis=1).astype(BF16)
        w_vt = w[:, o[2]:o[3]].T.astype(BF16)
        proj = _inproj(xb, w_main, tabs, S)
        vt = _moba_values_t(xb, w_vt, S)
        yat = _moba(proj, vt, B, S)
        yr = _retention(proj, B, S)
        xf, xb = _merge(yat, yr, proj, xf, w_moba_out[l].astype(BF16), w_ret_out[l].astype(BF16),
                        w_out[l].astype(BF16), ln1_g[l].reshape(1, D), ln1_b[l].reshape(1, D), B, S)
        routing = _route(xb, peer_w_query[l].astype(BF16), peer_sub_keys[l].astype(BF16))
        xf, xb = _peer(xb, xf, peer_u[l].astype(BF16), peer_v[l].T.astype(BF16), routing,
                       ln2_g[l].reshape(1, D), ln2_b[l].reshape(1, D))
    return xf.reshape(B, S, D).astype(x.dtype)
```

```python
import functools
import math

import numpy as np
import jax
import jax.numpy as jnp
from jax import lax
from jax.experimental import pallas as pl
from jax.experimental.pallas import tpu as pltpu

F32 = jnp.float32
BF16 = jnp.bfloat16

D_MODEL = 1024
DEPTH = 2
MOBA_HEADS = 8
MOBA_HEAD_DIM = 64
MOBA_WIDTH = MOBA_HEADS * MOBA_HEAD_DIM
MOBA_BLOCK = 256
MOBA_TOPK = 3
ROPE_THETA = 10000.0
RET_HEADS = 4
RET_QK_DIM = 128
RET_V_DIM = 256
RET_QK_WIDTH = RET_HEADS * RET_QK_DIM
RET_V_WIDTH = RET_HEADS * RET_V_DIM
RET_CHUNK = 256
PEER_N_KEYS = 128
PEER_N_EXPERTS = PEER_N_KEYS * PEER_N_KEYS
PEER_HEADS = 8
PEER_KEY_DIM = 256
PEER_TOPK = 16
DN_ALPHA = (2.0 * DEPTH) ** 0.25
LN_EPS = 1e-5
NEG = -1e30

IN_SIZES = (MOBA_WIDTH, MOBA_WIDTH, MOBA_WIDTH, RET_QK_WIDTH, RET_QK_WIDTH,
            RET_V_WIDTH, RET_V_WIDTH, D_MODEL, D_MODEL)
IN_OFFSETS = tuple(int(v) for v in np.concatenate([[0], np.cumsum(IN_SIZES)]))

LANES = 128
SUBLANES = 8
VMEM_LIMIT = 56 * 1024 * 1024

PROJ_TM = 1024
PROJ_TN = 1024
MERGE_TM = 512
ROUTE_TM = 512
ROUTE_KEY_CHUNK = 4
PEER_TM = 512
PEER_ROWS = 16
PEER_LANE_CHUNK = 256
PEER_CHUNK_ROWS = 2
PEER_SCHED_FLAGS = None

_NT = (((1,), (1,)), ((), ()))


def _params(sem, flags=None):
    return pltpu.CompilerParams(dimension_semantics=sem, vmem_limit_bytes=VMEM_LIMIT, flags=flags)


def _rotate_groups(acc, cos, sin, o_ref, partner_fn):
    for g in range(PROJ_TN // LANES):
        xg = acc[:, g * LANES:(g + 1) * LANES]
        o_ref[:, g * LANES:(g + 1) * LANES] = (xg * cos + partner_fn(xg) * sin).astype(o_ref.dtype)


def _inproj_kernel(x_ref, w_ref, cm_ref, sm_ref, cr_ref, sr_ref, o_ref):
    j = pl.program_id(1)
    acc = jnp.dot(x_ref[...], w_ref[...], preferred_element_type=F32)
    tm = acc.shape[0]

    moba_tiles = 2 * MOBA_WIDTH // PROJ_TN
    rope_tiles = moba_tiles + 2 * RET_QK_WIDTH // PROJ_TN

    @pl.when(j < moba_tiles)
    def _():
        lane = lax.broadcasted_iota(jnp.int32, (tm, LANES), 1)
        first = (lane % MOBA_HEAD_DIM) < (MOBA_HEAD_DIM // 2)

        def partner(xg):
            return jnp.where(first, pltpu.roll(xg, LANES - MOBA_HEAD_DIM // 2, axis=1),
                             pltpu.roll(xg, MOBA_HEAD_DIM // 2, axis=1))

        _rotate_groups(acc, cm_ref[...], sm_ref[...], o_ref, partner)

    @pl.when((j >= moba_tiles) & (j < rope_tiles))
    def _():
        _rotate_groups(acc, cr_ref[...], sr_ref[...], o_ref,
                       lambda xg: pltpu.roll(xg, RET_QK_DIM // 2, axis=1))

    @pl.when(j >= rope_tiles)
    def _():
        o_ref[...] = acc.astype(o_ref.dtype)


def _inproj(xb, w_main, tabs, S):
    T = xb.shape[0]
    tm = min(PROJ_TM, S)
    n_col = w_main.shape[1] // PROJ_TN
    pos_blocks = S // tm
    tab_spec = pl.BlockSpec((tm, LANES), lambda i, j: (i % pos_blocks, 0))
    return pl.pallas_call(
        _inproj_kernel,
        grid=(T // tm, n_col),
        in_specs=[pl.BlockSpec((tm, D_MODEL), lambda i, j: (i, 0)),
                  pl.BlockSpec((D_MODEL, PROJ_TN), lambda i, j: (0, j)),
                  tab_spec, tab_spec, tab_spec, tab_spec],
        out_specs=pl.BlockSpec((tm, PROJ_TN), lambda i, j: (i, j)),
        out_shape=jax.ShapeDtypeStruct((T, w_main.shape[1]), BF16),
        compiler_params=_params(("parallel", "arbitrary")),
        name="inproj",
    )(xb, w_main, *tabs)


MOBA_VROWS = MOBA_HEAD_DIM + 2 * SUBLANES


def _vt_kernel(x_ref, w_ref, o_ref):
    res = lax.dot_general(w_ref[...], x_ref[...], _NT, preferred_element_type=F32)
    hd = MOBA_HEAD_DIM
    pad = MOBA_VROWS - hd
    ones_row = jnp.where(lax.broadcasted_iota(jnp.int32, (pad, MOBA_BLOCK), 0) == 0, 1.0, 0.0).astype(o_ref.dtype)
    for c in range(o_ref.shape[0]):
        for h in range(MOBA_HEADS):
            o_ref[c, h, 0:hd, :] = res[h * hd:(h + 1) * hd, c * MOBA_BLOCK:(c + 1) * MOBA_BLOCK].astype(o_ref.dtype)
            o_ref[c, h, hd:MOBA_VROWS, :] = ones_row


def _moba_values_t(xb, w_vt, S):
    T = xb.shape[0]
    tm = min(PROJ_TM, S)
    per = tm // MOBA_BLOCK
    return pl.pallas_call(
        _vt_kernel,
        grid=(T // tm,),
        in_specs=[pl.BlockSpec((tm, D_MODEL), lambda i: (i, 0)),
                  pl.BlockSpec((MOBA_WIDTH, D_MODEL), lambda i: (0, 0))],
        out_specs=pl.BlockSpec((per, MOBA_HEADS, MOBA_VROWS, MOBA_BLOCK), lambda i: (i, 0, 0, 0)),
        out_shape=jax.ShapeDtypeStruct((T // MOBA_BLOCK, MOBA_HEADS, MOBA_VROWS, MOBA_BLOCK), BF16),
        compiler_params=_params(("parallel",)),
        name="moba_vt",
    )(xb, w_vt)


def _moba_kernel(q_ref, k_ref, vt_ref, o_ref, kmean_ref, bias_ref, qs_ref, acc_ref, m_ref, sa_ref, sb_ref, *, nb):
    i = pl.program_id(1)
    L = MOBA_BLOCK
    hd = MOBA_HEAD_DIM

    @pl.when(i == 0)
    def _():
        def body(j, c):
            kb = k_ref[pl.ds(pl.multiple_of(j * L, L), L), :].astype(F32)
            kmean_ref[pl.ds(j, 1), :] = jnp.sum(kb, axis=0, keepdims=True) * (1.0 / L)
            return c
        lax.fori_loop(0, nb, body, 0)

    per = LANES // hd
    groups = MOBA_HEADS // per
    W = MOBA_HEADS * L
    lane = lax.broadcasted_iota(jnp.int32, (L, LANES), 1)
    row0 = pl.multiple_of(i * L, L)

    gates = []
    for g in range(groups):
        cols = slice(g * LANES, (g + 1) * LANES)
        q_pair = q_ref[:, cols]
        km = kmean_ref[:, cols].astype(BF16)
        for hh in range(per):
            qm = jnp.where((lane // hd) == hh, q_pair, jnp.zeros_like(q_pair))
            gates.append(lax.dot_general(km, qm, _NT, preferred_element_type=F32))
            qs_ref[g, hh * L:(hh + 1) * L, :] = (qm.astype(F32) * (hd ** -0.5 * math.log2(math.e))).astype(BF16)

    blk = lax.broadcasted_iota(jnp.int32, (nb, W), 0)
    blkf = blk.astype(F32)
    gm = jnp.where(blk < i, jnp.concatenate(gates, axis=1), -jnp.inf)
    keep = jnp.zeros((nb, W), F32)
    for _ in range(MOBA_TOPK):
        best = jnp.max(gm, axis=0, keepdims=True)
        first = jnp.min(jnp.where(gm == best, blkf, float(nb)), axis=0, keepdims=True)
        hit = blkf == first
        keep = jnp.where(hit, 1.0, keep)
        gm = jnp.where(hit, -jnp.inf, gm)
    bias_ref[...] = jnp.where(blk < i, jnp.where(keep > 0.0, 0.0, NEG), NEG)

    def scores(r):
        return jnp.concatenate(
            [lax.dot_general(k_ref[pl.ds(r, L), g * LANES:(g + 1) * LANES], qs_ref[g], _NT,
                             preferred_element_type=F32) for g in range(groups)], axis=1)

    def values(j, p):
        pb = p.astype(BF16)
        return [jnp.dot(vt_ref[j, h], pb[:, h * L:(h + 1) * L], preferred_element_type=F32)
                for h in range(MOBA_HEADS)]

    kpos = lax.broadcasted_iota(jnp.int32, (L, W), 0)
    qpos = lax.broadcasted_iota(jnp.int32, (L, W), 1) % L
    s = jnp.where(kpos <= qpos, scores(row0), NEG)
    m = jnp.max(s, axis=0, keepdims=True)
    m_ref[...] = jnp.broadcast_to(m, (SUBLANES, W))
    for h, pv in enumerate(values(i, jnp.exp2(s - m))):
        acc_ref[h] = pv

    def scores_into(dst_ref, r):
        first = None
        for g in range(groups):
            sg = lax.dot_general(k_ref[pl.ds(r, L), g * LANES:(g + 1) * LANES], qs_ref[g], _NT,
                                 preferred_element_type=F32)
            dst_ref[:, g * per * L:(g + 1) * per * L] = sg
            if first is None:
                first = pltpu.bitcast(sg[0:SUBLANES, 0:LANES], jnp.uint32)
        return ((first >> 16) >> 16)[0, 0].astype(jnp.int32)

    def softmax_update(j, cur_ref, heads, row_start):
        lanes = slice(heads[0] * L, (heads[-1] + 1) * L)
        s = cur_ref[pl.ds(row_start, L), lanes]
        bias = bias_ref[pl.ds(j, 1), lanes]
        m_old = m_ref[:, lanes]
        m_new = jnp.maximum(m_old, jnp.max(s, axis=0, keepdims=True) + bias)
        alpha = jnp.exp2(m_old - m_new)
        m_ref[:, lanes] = m_new
        pb = jnp.exp2(s - (m_new[0:1, :] - bias)).astype(BF16)
        for n, h in enumerate(heads):
            pv = jnp.dot(vt_ref[j, h], pb[:, n * L:(n + 1) * L], preferred_element_type=F32)
            acc_ref[h] = alpha[0:1, n * L:(n + 1) * L] * acc_ref[h] + pv

    half = MOBA_HEADS // 2

    def past(j, cur_ref, nxt_ref):
        nxt = jnp.minimum(j + 1, nb - 1)
        start = pl.multiple_of(scores_into(nxt_ref, pl.multiple_of(nxt * L, L)), L)
        jc = jnp.minimum(j, nb - 1)
        softmax_update(jc, cur_ref, tuple(range(half)), 0)
        softmax_update(jc, cur_ref, tuple(range(half, MOBA_HEADS)), start)

    scores_into(sa_ref, 0)

    def four_past(jj, c):
        past(4 * jj, sa_ref, sb_ref)
        past(4 * jj + 1, sb_ref, sa_ref)
        past(4 * jj + 2, sa_ref, sb_ref)
        past(4 * jj + 3, sb_ref, sa_ref)
        return c

    lax.fori_loop(0, (i + 3) // 4, four_past, 0)

    for h in range(MOBA_HEADS):
        o_ref[0, h * hd:(h + 1) * hd, :] = acc_ref[h, 0:hd, :] / acc_ref[h, hd:hd + 1, :]


def _moba(proj, vt, B, S):
    nb = S // MOBA_BLOCK
    L = MOBA_BLOCK
    return pl.pallas_call(
        functools.partial(_moba_kernel, nb=nb),
        grid=(B, nb),
        in_specs=[pl.BlockSpec((L, MOBA_WIDTH), lambda b, i: (b * nb + i, 0)),
                  pl.BlockSpec((S, MOBA_WIDTH), lambda b, i: (b, 1)),
                  pl.BlockSpec((nb, MOBA_HEADS, MOBA_VROWS, L), lambda b, i: (b, 0, 0, 0))],
        out_specs=pl.BlockSpec((1, MOBA_WIDTH, L), lambda b, i: (b, 0, i)),
        out_shape=jax.ShapeDtypeStruct((B, MOBA_WIDTH, S), F32),
        scratch_shapes=[pltpu.VMEM((nb, MOBA_WIDTH), F32),
                        pltpu.VMEM((nb, MOBA_HEADS * L), F32),
                        pltpu.VMEM((MOBA_HEADS // (LANES // MOBA_HEAD_DIM), (LANES // MOBA_HEAD_DIM) * L, LANES),
                                   BF16),
                        pltpu.VMEM((MOBA_HEADS, MOBA_VROWS, L), F32),
                        pltpu.VMEM((SUBLANES, MOBA_HEADS * L), F32),
                        pltpu.VMEM((L, MOBA_HEADS * L), F32),
                        pltpu.VMEM((L, MOBA_HEADS * L), F32)],
        compiler_params=_params(("parallel", "arbitrary")),
        name="moba",
    )(proj, proj, vt)


def _ret_log_g():
    return jnp.log(1.0 - 2.0 ** (-5.0 - jnp.arange(RET_HEADS, dtype=F32)))


def _retention_kernel(q_ref, k_ref, v_ref, g_ref, dec_ref, qd_ref, kd_ref, cd_ref, o_ref, state_ref):
    n = pl.program_id(1)

    @pl.when(n == 0)
    def _():
        state_ref[...] = jnp.zeros_like(state_ref)

    for h in range(RET_HEADS):
        qk = slice(h * RET_QK_DIM, (h + 1) * RET_QK_DIM)
        vv = slice(h * RET_V_DIM, (h + 1) * RET_V_DIM)
        q = q_ref[:, qk]
        k = k_ref[:, qk]
        v = v_ref[:, vv]
        scores = lax.dot_general(q, k, _NT, preferred_element_type=F32) * dec_ref[h]
        y = jnp.dot(scores.astype(BF16), v, preferred_element_type=F32)
        state = state_ref[h]
        y = y + jnp.dot(q, state.astype(BF16), preferred_element_type=F32) * qd_ref[h]
        kt = (k.astype(F32) * kd_ref[h]).T.astype(BF16)
        kv = jnp.dot(kt, v, preferred_element_type=F32)
        state_ref[h] = state * cd_ref[h:h + 1, :] + kv
        mu = jnp.mean(y, axis=-1, keepdims=True)
        yc = y - mu
        var = jnp.mean(yc * yc, axis=-1, keepdims=True)
        yn = yc * lax.rsqrt(var + LN_EPS)
        gate = g_ref[:, vv].astype(F32)
        o_ref[:, vv] = (gate * jax.nn.sigmoid(gate) * yn).astype(o_ref.dtype)


def _retention(proj, B, S):
    C = RET_CHUNK
    nc = S // C
    T = B * S
    log_g = _ret_log_g()
    pos = jnp.arange(C, dtype=F32)
    diff = pos[:, None] - pos[None, :]
    scale = RET_QK_DIM ** -0.5
    decay = jnp.where(diff >= 0, jnp.exp(log_g[:, None, None] * jnp.maximum(diff, 0.0)), 0.0) * scale
    q_decay = jnp.broadcast_to(jnp.exp(log_g[:, None] * (pos + 1.0))[:, :, None], (RET_HEADS, C, RET_V_DIM))
    k_decay = jnp.broadcast_to((jnp.exp(log_g[:, None] * (C - 1.0 - pos)) * scale)[:, :, None],
                               (RET_HEADS, C, RET_QK_DIM))
    chunk_decay = jnp.broadcast_to(jnp.exp(log_g * C)[:, None], (RET_HEADS, RET_V_DIM))
    const = lambda shape: pl.BlockSpec(shape, lambda b, n: (0,) * len(shape))
    return pl.pallas_call(
        _retention_kernel,
        grid=(B, nc),
        in_specs=[pl.BlockSpec((C, RET_QK_WIDTH), lambda b, n: (b * nc + n, 2)),
                  pl.BlockSpec((C, RET_QK_WIDTH), lambda b, n: (b * nc + n, 3)),
                  pl.BlockSpec((C, RET_V_WIDTH), lambda b, n: (b * nc + n, 2)),
                  pl.BlockSpec((C, RET_V_WIDTH), lambda b, n: (b * nc + n, 3)),
                  const((RET_HEADS, C, C)), const((RET_HEADS, C, RET_V_DIM)),
                  const((RET_HEADS, C, RET_QK_DIM)), const((RET_HEADS, RET_V_DIM))],
        out_specs=pl.BlockSpec((C, RET_V_WIDTH), lambda b, n: (b * nc + n, 0)),
        out_shape=jax.ShapeDtypeStruct((T, RET_V_WIDTH), BF16),
        scratch_shapes=[pltpu.VMEM((RET_HEADS, RET_QK_DIM, RET_V_DIM), F32)],
        compiler_params=_params(("parallel", "arbitrary")),
        name="retention",
    )(proj, proj, proj, proj, decay, q_decay, k_decay, chunk_decay)


def _layer_norm(y, g, b):
    mu = jnp.mean(y, axis=-1, keepdims=True)
    yc = y - mu
    var = jnp.mean(yc * yc, axis=-1, keepdims=True)
    return yc * lax.rsqrt(var + LN_EPS) * g + b


def _merge_kernel(yat_ref, yr_ref, ga_ref, gr_ref, x_ref, wa_ref, wr_ref, wo_ref, g_ref, b_ref, o_ref, ob_ref):
    ya = yat_ref[0].T.astype(BF16)
    branch_a = jnp.dot(ya, wa_ref[...], preferred_element_type=F32)
    branch_r = jnp.dot(yr_ref[...], wr_ref[...], preferred_element_type=F32)
    merged = (jax.nn.sigmoid(ga_ref[...].astype(F32)) * branch_a
              + jax.nn.sigmoid(gr_ref[...].astype(F32)) * branch_r)
    mix = jnp.dot(merged.astype(BF16), wo_ref[...], preferred_element_type=F32)
    y = _layer_norm(DN_ALPHA * x_ref[...] + mix, g_ref[...], b_ref[...])
    o_ref[...] = y
    ob_ref[...] = y.astype(BF16)


def _merge(yat, yr, proj, x, wa, wr, wo, g, b, B, S):
    L = min(MERGE_TM, S)
    nb = S // L
    T = B * S
    tok = lambda c: pl.BlockSpec((L, D_MODEL), lambda bb, i: (bb * nb + i, c))
    const = lambda shape: pl.BlockSpec(shape, lambda bb, i: (0,) * len(shape))
    return pl.pallas_call(
        _merge_kernel,
        grid=(B, nb),
        in_specs=[pl.BlockSpec((1, MOBA_WIDTH, L), lambda bb, i: (bb, 0, i)),
                  tok(0), tok(4), tok(5), tok(0),
                  const((MOBA_WIDTH, D_MODEL)), const((RET_V_WIDTH, D_MODEL)), const((D_MODEL, D_MODEL)),
                  const((1, D_MODEL)), const((1, D_MODEL))],
        out_specs=[tok(0), tok(0)],
        out_shape=[jax.ShapeDtypeStruct((T, D_MODEL), F32), jax.ShapeDtypeStruct((T, D_MODEL), BF16)],
        compiler_params=_params(("parallel", "parallel")),
        name="merge_ln1",
    )(yat, yr, proj, proj, x, wa, wr, wo, g, b)


def _cmpx(xs, a, b):
    if xs[b] is None:
        return
    if xs[a] is None:
        xs[a], xs[b] = xs[b], None
        return
    hi = jnp.maximum(xs[a], xs[b])
    lo = jnp.minimum(xs[a], xs[b])
    xs[a], xs[b] = hi, lo


def _bitonic_merge_desc(xs):
    n = len(xs)
    j = n // 2
    while j >= 1:
        for a in range(n):
            b = a ^ j
            if b > a:
                _cmpx(xs, a, b)
        j //= 2
    return xs


def _sort_desc(xs):
    xs = list(xs)
    n = len(xs)
    k = 2
    while k <= n:
        j = k // 2
        while j >= 1:
            for a in range(n):
                b = a ^ j
                if b > a:
                    if (a & k) == 0:
                        _cmpx(xs, a, b)
                    else:
                        _cmpx(xs, b, a)
            j //= 2
        k *= 2
    return xs


def _max_or_none(x, y):
    if x is None:
        return y
    if y is None:
        return x
    return jnp.maximum(x, y)


def _merge_top(xs, ys):
    n = len(xs)
    return _bitonic_merge_desc([_max_or_none(xs[v], ys[n - 1 - v]) for v in range(n)])


def _top_desc(rows, k):
    lists = [_sort_desc(rows[g:g + k]) for g in range(0, len(rows), k)]
    while len(lists) > 1:
        lists = [_merge_top(lists[g], lists[g + 1]) for g in range(0, len(lists), 2)]
    return lists[0]


def _route_kernel(x_ref, wq_ref, sk_ref, r1_ref, e1_ref, n0_ref, c0_ref, sc_ref, tmp_ref):
    half_dim = PEER_KEY_DIM // 2
    for sub in range(x_ref.shape[0] // LANES):
        toks = slice(sub * LANES, (sub + 1) * LANES)
        q = jnp.dot(x_ref[toks, :], wq_ref[...], preferred_element_type=F32).astype(BF16)
        for p in range(2):
            for h in range(PEER_HEADS):
                qhp = q[:, (2 * h + p) * half_dim:(2 * h + p + 1) * half_dim]
                sc_ref[sub, p, pl.ds(h, PEER_N_KEYS, stride=PEER_HEADS), :] = lax.dot_general(
                    sk_ref[p], qhp, _NT, preferred_element_type=F32)
        _route_tokens(sc_ref.at[sub], tmp_ref.at[sub], toks, r1_ref, e1_ref, n0_ref, c0_ref)


def _route_tokens(sc_ref, tmp_ref, toks, r1_ref, e1_ref, n0_ref, c0_ref):
    K = PEER_TOPK
    nk = PEER_N_KEYS
    H = PEER_HEADS
    tm = LANES
    inf = jnp.inf
    s0 = sc_ref[0].reshape(nk, H, tm)
    s1 = sc_ref[1].reshape(nk, H, tm)
    a = _top_desc([s0[k] for k in range(nk)], K)
    b = _top_desc([s1[k] for k in range(nk)], K)

    sums = [[a[r] + b[c] for c in range(K // (r + 1))] for r in range(K)]
    pad = lambda xs: xs + [None] * (K - len(xs))
    z = _merge_top(sums[0], pad(sums[1]))
    mid = [x for r in range(2, 7) for x in sums[r]]
    z = _merge_top(z, _sort_desc(pad(mid)))
    low = [x for r in range(7, K) for x in sums[r]]
    z = _merge_top(z, _sort_desc(pad(low)))
    tau = z[K - 1]
    zsum = jnp.ones_like(tau)
    for v in range(1, K):
        zsum = zsum + jnp.exp(z[v] - z[0])
    inv_z = 1.0 / zsum

    alphas = []
    for c in range(K):
        alpha = None
        for r in range(K // (c + 1)):
            cand = jnp.where(sums[r][c] >= tau, a[r], inf)
            alpha = cand if alpha is None else jnp.minimum(alpha, cand)
        alphas.append(alpha)

    for k0 in range(0, nk, ROUTE_KEY_CHUNK):
        blk0 = s0[k0:k0 + ROUTE_KEY_CHUNK]
        blk1 = s1[k0:k0 + ROUTE_KEY_CHUNK]
        count = jnp.zeros(blk0.shape, F32)
        for c in range(K):
            count = jnp.where(blk0 >= alphas[c][None], float(c + 1), count)
        rank = jnp.full(blk1.shape, float(K), F32)
        for c in reversed(range(K)):
            rank = jnp.where(blk1 >= b[c][None], float(c), rank)
        e1 = jnp.exp(blk1 - b[0][None])
        c0 = jnp.exp(blk0 - a[0][None]) * inv_z[None]
        rows = slice(k0 * H, (k0 + ROUTE_KEY_CHUNK) * H)
        for n, val in enumerate((rank, e1, count, c0)):
            tmp_ref[n, rows, :] = val.reshape(ROUTE_KEY_CHUNK * H, tm)

    for n, dst in enumerate((r1_ref, e1_ref, n0_ref, c0_ref)):
        for h in range(H):
            dst[h, :, toks] = tmp_ref[n, pl.ds(h, nk, stride=H), :].astype(dst.dtype)


def _route(xb, wq, sk):
    T = xb.shape[0]
    tm = ROUTE_TM
    H = PEER_HEADS
    nk = PEER_N_KEYS
    subs = tm // LANES
    spec = pl.BlockSpec((H, nk, tm), lambda t: (0, 0, t))
    half = jax.ShapeDtypeStruct((H, nk, T), BF16)
    word = jax.ShapeDtypeStruct((H, nk, T), F32)
    return pl.pallas_call(
        _route_kernel,
        grid=(T // tm,),
        in_specs=[pl.BlockSpec((tm, D_MODEL), lambda t: (t, 0)),
                  pl.BlockSpec(wq.shape, lambda t: (0, 0)),
                  pl.BlockSpec(sk.shape, lambda t: (0, 0, 0))],
        out_specs=[spec, spec, spec, spec],
        out_shape=[half, half, word, word],
        scratch_shapes=[pltpu.VMEM((subs, 2, nk * H, LANES), F32),
                        pltpu.VMEM((subs, 4, nk * H, LANES), F32)],
        compiler_params=_params(("parallel",)),
        name="peer_route",
    )(xb, wq, sk)


def _gelu_tanh(x):
    k = -2.0 * math.sqrt(2.0 / math.pi) * math.log2(math.e)
    z = x * (k + (k * 0.044715) * (x * x))
    return x / (1.0 + jnp.exp2(z))


def _peer_kernel(xb_ref, u_ref, vt_ref, r1_ref, e1_ref, n0_ref, c0_ref, x_ref, g_ref, b_ref,
                 o_ref, ob_ref, w_ref, act_ref, acc_ref, *, rows):
    e = pl.program_id(1)
    nk = PEER_N_KEYS
    tm = xb_ref.shape[0]
    pk = 2 * SUBLANES

    @pl.when(e == 0)
    def _():
        acc_ref[...] = jnp.zeros_like(acc_ref)

    tc = min(PEER_LANE_CHUNK, tm)

    def routing_weights(r):
        i = e * rows + r
        first = None
        for ch in range(tm // tc):
            ls = slice(ch * tc, (ch + 1) * tc)
            w = [None] * (nk // pk)
            for h in range(PEER_HEADS):
                count = jnp.broadcast_to(n0_ref[h, pl.ds(i, 1), ls], (pk, tc)).astype(BF16)
                weight = jnp.broadcast_to(c0_ref[h, pl.ds(i, 1), ls], (pk, tc)).astype(BF16)
                for g in range(nk // pk):
                    rs = slice(g * pk, (g + 1) * pk)
                    sel = jnp.where(r1_ref[h, rs, ls] < count, e1_ref[h, rs, ls], jnp.zeros((pk, tc), BF16))
                    term = sel * weight
                    w[g] = term if w[g] is None else w[g] + term
            for g in range(nk // pk):
                w_ref[r * nk + g * pk:r * nk + (g + 1) * pk, ls] = w[g]
            if first is None:
                first = pltpu.bitcast(w[0][:, 0:LANES], jnp.uint32)
        return first

    cr = PEER_CHUNK_ROWS
    zero = 0
    for c in range(rows // cr):
        bits = routing_weights(c * cr)
        for r in range(c * cr + 1, (c + 1) * cr):
            routing_weights(r)
        lo = 0 if c == 0 else pl.multiple_of(c * cr * nk + zero, cr * nk)
        ht = lax.dot_general(u_ref[pl.ds(lo, cr * nk), :], xb_ref[...], _NT,
                             preferred_element_type=F32)
        rows_c = slice(c * cr * nk, (c + 1) * cr * nk)
        act_ref[rows_c, :] = _gelu_tanh(ht.astype(BF16)) * w_ref[rows_c, :]
        zero = ((bits >> 16) >> 16)[0, 0].astype(jnp.int32)
    acc_ref[...] += jnp.dot(vt_ref[...], act_ref[...], preferred_element_type=F32)

    @pl.when(e == pl.num_programs(1) - 1)
    def _():
        y = _layer_norm(DN_ALPHA * x_ref[...] + acc_ref[...].T, g_ref[...], b_ref[...])
        o_ref[...] = y
        ob_ref[...] = y.astype(BF16)


def _peer(xb, x, u, vt, routing, g, b):
    T = xb.shape[0]
    tm = min(PEER_TM, T)
    rows = PEER_ROWS
    nk = PEER_N_KEYS
    rspec = pl.BlockSpec((PEER_HEADS, nk, tm), lambda t, e: (0, 0, t))
    tok = pl.BlockSpec((tm, D_MODEL), lambda t, e: (t, 0))
    const = pl.BlockSpec((1, D_MODEL), lambda t, e: (0, 0))
    return pl.pallas_call(
        functools.partial(_peer_kernel, rows=rows),
        grid=(T // tm, nk // rows),
        in_specs=[tok,
                  pl.BlockSpec((rows * nk, D_MODEL), lambda t, e: (e, 0)),
                  pl.BlockSpec((D_MODEL, rows * nk), lambda t, e: (0, e)),
                  rspec, rspec, rspec, rspec, tok, const, const],
        out_specs=[tok, tok],
        out_shape=[jax.ShapeDtypeStruct((T, D_MODEL), F32), jax.ShapeDtypeStruct((T, D_MODEL), BF16)],
        scratch_shapes=[pltpu.VMEM((rows * nk, tm), BF16),
                        pltpu.VMEM((rows * nk, tm), BF16),
                        pltpu.VMEM((D_MODEL, tm), F32)],
        compiler_params=_params(("parallel", "arbitrary"), PEER_SCHED_FLAGS),
        name="peer_dense",
    )(xb, u, vt, *routing, x, g, b)


def _rope_tables(S):
    pos = jnp.arange(S, dtype=F32)
    d = MOBA_HEAD_DIM
    inv = ROPE_THETA ** (-jnp.arange(0, d, 2, dtype=F32) / d)
    ang = pos[:, None] * inv[None, :]
    cos, sin = jnp.cos(ang), jnp.sin(ang)
    reps = LANES // d
    cos_m = jnp.tile(jnp.concatenate([cos, cos], axis=1), (1, reps))
    sin_m = jnp.tile(jnp.concatenate([-sin, sin], axis=1), (1, reps))
    d = RET_QK_DIM
    inv = 1.0 / (ROPE_THETA ** jnp.linspace(0.0, 1.0, d // 2, dtype=F32))
    ang = pos[:, None] * inv[None, :]
    cos, sin = jnp.cos(ang), jnp.sin(ang)
    cos_r = jnp.concatenate([cos, cos], axis=1)
    sin_r = jnp.concatenate([-sin, sin], axis=1)
    return cos_m, sin_m, cos_r, sin_r


def _even_odd_columns(w):
    rows = w.shape[0]
    return (w.reshape(rows, RET_HEADS, RET_QK_DIM // 2, 2).transpose(0, 1, 3, 2)
            .reshape(rows, RET_HEADS * RET_QK_DIM))


def kernel(x, w_in, w_moba_out, w_ret_out, w_out, ln1_g, ln1_b, peer_w_query, peer_sub_keys,
           peer_u, peer_v, ln2_g, ln2_b):
    B, S, D = x.shape
    assert D == D_MODEL and S % MOBA_BLOCK == 0 and S % RET_CHUNK == 0
    T = B * S
    tabs = _rope_tables(S)
    o = IN_OFFSETS
    xf = x.reshape(T, D).astype(F32)
    xb = xf.astype(BF16)
    for l in range(DEPTH):
        w = w_in[l].astype(BF16)
        w_main = jnp.concatenate(
            [w[:, o[0]:o[2]], _even_odd_columns(w[:, o[3]:o[4]]), _even_odd_columns(w[:, o[4]:o[5]]),
             w[:, o[5]:o[9]]], axis=1)
        w_vt = w[:, o[2]:o[3]].T
        proj = _inproj(xb, w_main, tabs, S)
        vt = _moba_values_t(xb, w_vt, S)
        yat = _moba(proj, vt, B, S)
        yr = _retention(proj, B, S)
        xf, xb = _merge(yat, yr, proj, xf, w_moba_out[l].astype(BF16), w_ret_out[l].astype(BF16),
                        w_out[l].astype(BF16), ln1_g[l].reshape(1, D), ln1_b[l].reshape(1, D), B, S)
        routing = _route(xb, peer_w_query[l].astype(BF16), peer_sub_keys[l].astype(BF16))
        xf, xb = _peer(xb, xf, peer_u[l].astype(BF16), peer_v[l].T.astype(BF16), routing,
                       ln2_g[l].reshape(1, D), ln2_b[l].reshape(1, D))
    return xf.reshape(B, S, D).astype(x.dtype)
```

```python
import functools
import math

import numpy as np
import jax
import jax.numpy as jnp
from jax import lax
from jax.experimental import pallas as pl
from jax.experimental.pallas import tpu as pltpu

F32 = jnp.float32
BF16 = jnp.bfloat16

D_MODEL = 1024
DEPTH = 2
MOBA_HEADS = 8
MOBA_HEAD_DIM = 64
MOBA_WIDTH = MOBA_HEADS * MOBA_HEAD_DIM
MOBA_BLOCK = 256
MOBA_TOPK = 3
ROPE_THETA = 10000.0
RET_HEADS = 4
RET_QK_DIM = 128
RET_V_DIM = 256
RET_QK_WIDTH = RET_HEADS * RET_QK_DIM
RET_V_WIDTH = RET_HEADS * RET_V_DIM
RET_CHUNK = 256
PEER_N_KEYS = 128
PEER_N_EXPERTS = PEER_N_KEYS * PEER_N_KEYS
PEER_HEADS = 8
PEER_KEY_DIM = 256
PEER_TOPK = 16
DN_ALPHA = (2.0 * DEPTH) ** 0.25
LN_EPS = 1e-5
NEG = -1e30

IN_SIZES = (MOBA_WIDTH, MOBA_WIDTH, MOBA_WIDTH, RET_QK_WIDTH, RET_QK_WIDTH,
            RET_V_WIDTH, RET_V_WIDTH, D_MODEL, D_MODEL)
IN_OFFSETS = tuple(int(v) for v in np.concatenate([[0], np.cumsum(IN_SIZES)]))

LANES = 128
SUBLANES = 8
VMEM_LIMIT = 56 * 1024 * 1024

PROJ_TM = 1024
PROJ_TN = 1024
MERGE_TM = 512
ROUTE_TM = 512
ROUTE_KEY_CHUNK = 4
PEER_TM = 512
PEER_ROWS = 16
PEER_LANE_CHUNK = 256
PEER_CHUNK_ROWS = 2
PEER_SCHED_FLAGS = None

_NT = (((1,), (1,)), ((), ()))


def _params(sem, flags=None):
    return pltpu.CompilerParams(dimension_semantics=sem, vmem_limit_bytes=VMEM_LIMIT, flags=flags)


def _rotate_groups(acc, cos, sin, o_ref, partner_fn):
    for g in range(PROJ_TN // LANES):
        xg = acc[:, g * LANES:(g + 1) * LANES]
        o_ref[:, g * LANES:(g + 1) * LANES] = (xg * cos + partner_fn(xg) * sin).astype(o_ref.dtype)


def _inproj_kernel(x_ref, w_ref, cm_ref, sm_ref, cr_ref, sr_ref, o_ref):
    j = pl.program_id(1)
    acc = jnp.dot(x_ref[...], w_ref[...], preferred_element_type=F32)

    moba_tiles = 2 * MOBA_WIDTH // PROJ_TN
    rope_tiles = moba_tiles + 2 * RET_QK_WIDTH // PROJ_TN

    partner = lambda xg: pltpu.roll(xg, LANES // 2, axis=1)

    @pl.when(j < moba_tiles)
    def _():
        _rotate_groups(acc, cm_ref[...], sm_ref[...], o_ref, partner)

    @pl.when((j >= moba_tiles) & (j < rope_tiles))
    def _():
        _rotate_groups(acc, cr_ref[...], sr_ref[...], o_ref, partner)

    @pl.when(j >= rope_tiles)
    def _():
        o_ref[...] = acc.astype(o_ref.dtype)


def _inproj(xb, w_main, tabs, S):
    T = xb.shape[0]
    tm = min(PROJ_TM, S)
    n_col = w_main.shape[1] // PROJ_TN
    pos_blocks = S // tm
    tab_spec = pl.BlockSpec((tm, LANES), lambda i, j: (i % pos_blocks, 0))
    return pl.pallas_call(
        _inproj_kernel,
        grid=(T // tm, n_col),
        in_specs=[pl.BlockSpec((tm, D_MODEL), lambda i, j: (i, 0)),
                  pl.BlockSpec((D_MODEL, PROJ_TN), lambda i, j: (0, j)),
                  tab_spec, tab_spec, tab_spec, tab_spec],
        out_specs=pl.BlockSpec((tm, PROJ_TN), lambda i, j: (i, j)),
        out_shape=jax.ShapeDtypeStruct((T, w_main.shape[1]), BF16),
        compiler_params=_params(("parallel", "arbitrary")),
        name="inproj",
    )(xb, w_main, *tabs)


MOBA_VROWS = MOBA_HEAD_DIM + 2 * SUBLANES


def _vt_kernel(x_ref, w_ref, o_ref):
    res = lax.dot_general(w_ref[...], x_ref[...], _NT, preferred_element_type=F32)
    hd = MOBA_HEAD_DIM
    pad = MOBA_VROWS - hd
    ones_row = jnp.where(lax.broadcasted_iota(jnp.int32, (pad, MOBA_BLOCK), 0) == 0, 1.0, 0.0).astype(o_ref.dtype)
    for c in range(o_ref.shape[0]):
        for h in range(MOBA_HEADS):
            o_ref[c, h, 0:hd, :] = res[h * hd:(h + 1) * hd, c * MOBA_BLOCK:(c + 1) * MOBA_BLOCK].astype(o_ref.dtype)
            o_ref[c, h, hd:MOBA_VROWS, :] = ones_row


def _moba_values_t(xb, w_vt, S):
    T = xb.shape[0]
    tm = min(PROJ_TM, S)
    per = tm // MOBA_BLOCK
    return pl.pallas_call(
        _vt_kernel,
        grid=(T // tm,),
        in_specs=[pl.BlockSpec((tm, D_MODEL), lambda i: (i, 0)),
                  pl.BlockSpec((MOBA_WIDTH, D_MODEL), lambda i: (0, 0))],
        out_specs=pl.BlockSpec((per, MOBA_HEADS, MOBA_VROWS, MOBA_BLOCK), lambda i: (i, 0, 0, 0)),
        out_shape=jax.ShapeDtypeStruct((T // MOBA_BLOCK, MOBA_HEADS, MOBA_VROWS, MOBA_BLOCK), BF16),
        compiler_params=_params(("parallel",)),
        name="moba_vt",
    )(xb, w_vt)


def _moba_kernel(q_ref, k_ref, vt_ref, o_ref, kmean_ref, bias_ref, qs_ref, acc_ref, m_ref, sa_ref, sb_ref, *, nb):
    i = pl.program_id(1)
    L = MOBA_BLOCK
    hd = MOBA_HEAD_DIM

    @pl.when(i == 0)
    def _():
        def body(j, c):
            kb = k_ref[pl.ds(pl.multiple_of(j * L, L), L), :].astype(F32)
            kmean_ref[pl.ds(j, 1), :] = jnp.sum(kb, axis=0, keepdims=True) * (1.0 / L)
            return c
        lax.fori_loop(0, nb, body, 0)

    per = LANES // hd
    groups = MOBA_HEADS // per
    W = MOBA_HEADS * L
    lane = lax.broadcasted_iota(jnp.int32, (L, LANES), 1)
    row0 = pl.multiple_of(i * L, L)

    gates = []
    for g in range(groups):
        cols = slice(g * LANES, (g + 1) * LANES)
        q_pair = q_ref[:, cols]
        km = kmean_ref[:, cols].astype(BF16)
        for hh in range(per):
            qm = jnp.where((lane // (hd // 2)) % per == hh, q_pair, jnp.zeros_like(q_pair))
            gates.append(lax.dot_general(km, qm, _NT, preferred_element_type=F32))
            qs_ref[g, hh * L:(hh + 1) * L, :] = (qm.astype(F32) * (hd ** -0.5 * math.log2(math.e))).astype(BF16)

    blk = lax.broadcasted_iota(jnp.int32, (nb, W), 0)
    blkf = blk.astype(F32)
    gm = jnp.where(blk < i, jnp.concatenate(gates, axis=1), -jnp.inf)
    keep = jnp.zeros((nb, W), F32)
    for _ in range(MOBA_TOPK):
        best = jnp.max(gm, axis=0, keepdims=True)
        first = jnp.min(jnp.where(gm == best, blkf, float(nb)), axis=0, keepdims=True)
        hit = blkf == first
        keep = jnp.where(hit, 1.0, keep)
        gm = jnp.where(hit, -jnp.inf, gm)
    bias_ref[...] = jnp.where(blk < i, jnp.where(keep > 0.0, 0.0, NEG), NEG)

    def scores(r):
        return jnp.concatenate(
            [lax.dot_general(k_ref[pl.ds(r, L), g * LANES:(g + 1) * LANES], qs_ref[g], _NT,
                             preferred_element_type=F32) for g in range(groups)], axis=1)

    def values(j, p):
        pb = p.astype(BF16)
        return [jnp.dot(vt_ref[j, h], pb[:, h * L:(h + 1) * L], preferred_element_type=F32)
                for h in range(MOBA_HEADS)]

    kpos = lax.broadcasted_iota(jnp.int32, (L, W), 0)
    qpos = lax.broadcasted_iota(jnp.int32, (L, W), 1) % L
    s = jnp.where(kpos <= qpos, scores(row0), NEG)
    m = jnp.max(s, axis=0, keepdims=True)
    m_ref[...] = jnp.broadcast_to(m, (SUBLANES, W))
    for h, pv in enumerate(values(i, jnp.exp2(s - m))):
        acc_ref[h] = pv

    def scores_into(dst_ref, r):
        first = None
        for g in range(groups):
            sg = lax.dot_general(k_ref[pl.ds(r, L), g * LANES:(g + 1) * LANES], qs_ref[g], _NT,
                                 preferred_element_type=F32)
            dst_ref[:, g * per * L:(g + 1) * per * L] = sg
            if first is None:
                first = pltpu.bitcast(sg[0:SUBLANES, 0:LANES], jnp.uint32)
        return ((first >> 16) >> 16)[0, 0].astype(jnp.int32)

    def softmax_update(j, cur_ref, heads, row_start):
        lanes = slice(heads[0] * L, (heads[-1] + 1) * L)
        s = cur_ref[pl.ds(row_start, L), lanes]
        bias = bias_ref[pl.ds(j, 1), lanes]
        m_old = m_ref[:, lanes]
        m_new = jnp.maximum(m_old, jnp.max(s, axis=0, keepdims=True) + bias)
        alpha = jnp.exp2(m_old - m_new)
        m_ref[:, lanes] = m_new
        pb = jnp.exp2(s - (m_new[0:1, :] - bias)).astype(BF16)
        for n, h in enumerate(heads):
            pv = jnp.dot(vt_ref[j, h], pb[:, n * L:(n + 1) * L], preferred_element_type=F32)
            acc_ref[h] = alpha[0:1, n * L:(n + 1) * L] * acc_ref[h] + pv

    half = MOBA_HEADS // 2

    def past(j, cur_ref, nxt_ref):
        start = 0
        if nxt_ref is not None:
            nxt = jnp.minimum(j + 1, i - 1)
            start = pl.multiple_of(scores_into(nxt_ref, pl.multiple_of(nxt * L, L)), L)
        softmax_update(j, cur_ref, tuple(range(half)), 0)
        softmax_update(j, cur_ref, tuple(range(half, MOBA_HEADS)), start)

    scores_into(sa_ref, 0)

    def four_past(jj, c):
        past(4 * jj, sa_ref, sb_ref)
        past(4 * jj + 1, sb_ref, sa_ref)
        past(4 * jj + 2, sa_ref, sb_ref)
        past(4 * jj + 3, sb_ref, sa_ref)
        return c

    lax.fori_loop(0, i // 4, four_past, 0)

    rest = i % 4
    base = i - rest

    @pl.when(rest >= 1)
    def _():
        past(base, sa_ref, sb_ref)

    @pl.when(rest >= 2)
    def _():
        past(base + 1, sb_ref, sa_ref)

    @pl.when(rest >= 3)
    def _():
        past(base + 2, sa_ref, None)

    for h in range(MOBA_HEADS):
        o_ref[0, h * hd:(h + 1) * hd, :] = acc_ref[h, 0:hd, :] / acc_ref[h, hd:hd + 1, :]


def _moba(proj, vt, B, S):
    nb = S // MOBA_BLOCK
    L = MOBA_BLOCK
    return pl.pallas_call(
        functools.partial(_moba_kernel, nb=nb),
        grid=(B, nb),
        in_specs=[pl.BlockSpec((L, MOBA_WIDTH), lambda b, i: (b * nb + i, 0)),
                  pl.BlockSpec((S, MOBA_WIDTH), lambda b, i: (b, 1)),
                  pl.BlockSpec((nb, MOBA_HEADS, MOBA_VROWS, L), lambda b, i: (b, 0, 0, 0))],
        out_specs=pl.BlockSpec((1, MOBA_WIDTH, L), lambda b, i: (b, 0, i)),
        out_shape=jax.ShapeDtypeStruct((B, MOBA_WIDTH, S), F32),
        scratch_shapes=[pltpu.VMEM((nb, MOBA_WIDTH), F32),
                        pltpu.VMEM((nb, MOBA_HEADS * L), F32),
                        pltpu.VMEM((MOBA_HEADS // (LANES // MOBA_HEAD_DIM), (LANES // MOBA_HEAD_DIM) * L, LANES),
                                   BF16),
                        pltpu.VMEM((MOBA_HEADS, MOBA_VROWS, L), F32),
                        pltpu.VMEM((SUBLANES, MOBA_HEADS * L), F32),
                        pltpu.VMEM((L, MOBA_HEADS * L), F32),
                        pltpu.VMEM((L, MOBA_HEADS * L), F32)],
        compiler_params=_params(("parallel", "arbitrary")),
        name="moba",
    )(proj, proj, vt)


def _ret_log_g():
    return jnp.log(1.0 - 2.0 ** (-5.0 - jnp.arange(RET_HEADS, dtype=F32)))


def _retention_kernel(q_ref, k_ref, v_ref, g_ref, dec_ref, qd_ref, kd_ref, cd_ref, o_ref, state_ref):
    n = pl.program_id(1)

    @pl.when(n == 0)
    def _():
        state_ref[...] = jnp.zeros_like(state_ref)

    for h in range(RET_HEADS):
        qk = slice(h * RET_QK_DIM, (h + 1) * RET_QK_DIM)
        vv = slice(h * RET_V_DIM, (h + 1) * RET_V_DIM)
        q = q_ref[:, qk]
        k = k_ref[:, qk]
        v = v_ref[:, vv]
        scores = lax.dot_general(q, k, _NT, preferred_element_type=F32) * dec_ref[h]
        y = jnp.dot(scores.astype(BF16), v, preferred_element_type=F32)
        state = state_ref[h]
        y = y + jnp.dot(q, state.astype(BF16), preferred_element_type=F32) * qd_ref[h]
        kt = (k.astype(F32) * kd_ref[h]).T.astype(BF16)
        kv = jnp.dot(kt, v, preferred_element_type=F32)
        state_ref[h] = state * cd_ref[h:h + 1, :] + kv
        mu = jnp.mean(y, axis=-1, keepdims=True)
        yc = y - mu
        var = jnp.mean(yc * yc, axis=-1, keepdims=True)
        yn = yc * lax.rsqrt(var + LN_EPS)
        gate = g_ref[:, vv].astype(F32)
        o_ref[:, vv] = (gate * jax.nn.sigmoid(gate) * yn).astype(o_ref.dtype)


def _retention(proj, B, S):
    C = RET_CHUNK
    nc = S // C
    T = B * S
    log_g = _ret_log_g()
    pos = jnp.arange(C, dtype=F32)
    diff = pos[:, None] - pos[None, :]
    scale = RET_QK_DIM ** -0.5
    decay = jnp.where(diff >= 0, jnp.exp(log_g[:, None, None] * jnp.maximum(diff, 0.0)), 0.0) * scale
    q_decay = jnp.broadcast_to(jnp.exp(log_g[:, None] * (pos + 1.0))[:, :, None], (RET_HEADS, C, RET_V_DIM))
    k_decay = jnp.broadcast_to((jnp.exp(log_g[:, None] * (C - 1.0 - pos)) * scale)[:, :, None],
                               (RET_HEADS, C, RET_QK_DIM))
    chunk_decay = jnp.broadcast_to(jnp.exp(log_g * C)[:, None], (RET_HEADS, RET_V_DIM))
    const = lambda shape: pl.BlockSpec(shape, lambda b, n: (0,) * len(shape))
    return pl.pallas_call(
        _retention_kernel,
        grid=(B, nc),
        in_specs=[pl.BlockSpec((C, RET_QK_WIDTH), lambda b, n: (b * nc + n, 2)),
                  pl.BlockSpec((C, RET_QK_WIDTH), lambda b, n: (b * nc + n, 3)),
                  pl.BlockSpec((C, RET_V_WIDTH), lambda b, n: (b * nc + n, 2)),
                  pl.BlockSpec((C, RET_V_WIDTH), lambda b, n: (b * nc + n, 3)),
                  const((RET_HEADS, C, C)), const((RET_HEADS, C, RET_V_DIM)),
                  const((RET_HEADS, C, RET_QK_DIM)), const((RET_HEADS, RET_V_DIM))],
        out_specs=pl.BlockSpec((C, RET_V_WIDTH), lambda b, n: (b * nc + n, 0)),
        out_shape=jax.ShapeDtypeStruct((T, RET_V_WIDTH), BF16),
        scratch_shapes=[pltpu.VMEM((RET_HEADS, RET_QK_DIM, RET_V_DIM), F32)],
        compiler_params=_params(("parallel", "arbitrary")),
        name="retention",
    )(proj, proj, proj, proj, decay, q_decay, k_decay, chunk_decay)


def _layer_norm(y, g, b):
    mu = jnp.mean(y, axis=-1, keepdims=True)
    yc = y - mu
    var = jnp.mean(yc * yc, axis=-1, keepdims=True)
    return yc * lax.rsqrt(var + LN_EPS) * g + b


def _merge_kernel(yat_ref, yr_ref, ga_ref, gr_ref, x_ref, wa_ref, wr_ref, wo_ref, g_ref, b_ref, o_ref, ob_ref):
    ya = yat_ref[0].T.astype(BF16)
    branch_a = jnp.dot(ya, wa_ref[...], preferred_element_type=F32)
    branch_r = jnp.dot(yr_ref[...], wr_ref[...], preferred_element_type=F32)
    merged = (jax.nn.sigmoid(ga_ref[...].astype(F32)) * branch_a
              + jax.nn.sigmoid(gr_ref[...].astype(F32)) * branch_r)
    mix = jnp.dot(merged.astype(BF16), wo_ref[...], preferred_element_type=F32)
    y = _layer_norm(DN_ALPHA * x_ref[...] + mix, g_ref[...], b_ref[...])
    o_ref[...] = y
    ob_ref[...] = y.astype(BF16)


def _merge(yat, yr, proj, x, wa, wr, wo, g, b, B, S):
    L = min(MERGE_TM, S)
    nb = S // L
    T = B * S
    tok = lambda c: pl.BlockSpec((L, D_MODEL), lambda bb, i: (bb * nb + i, c))
    const = lambda shape: pl.BlockSpec(shape, lambda bb, i: (0,) * len(shape))
    return pl.pallas_call(
        _merge_kernel,
        grid=(B, nb),
        in_specs=[pl.BlockSpec((1, MOBA_WIDTH, L), lambda bb, i: (bb, 0, i)),
                  tok(0), tok(4), tok(5), tok(0),
                  const((MOBA_WIDTH, D_MODEL)), const((RET_V_WIDTH, D_MODEL)), const((D_MODEL, D_MODEL)),
                  const((1, D_MODEL)), const((1, D_MODEL))],
        out_specs=[tok(0), tok(0)],
        out_shape=[jax.ShapeDtypeStruct((T, D_MODEL), F32), jax.ShapeDtypeStruct((T, D_MODEL), BF16)],
        compiler_params=_params(("parallel", "parallel")),
        name="merge_ln1",
    )(yat, yr, proj, proj, x, wa, wr, wo, g, b)


def _cmpx(xs, a, b):
    if xs[b] is None:
        return
    if xs[a] is None:
        xs[a], xs[b] = xs[b], None
        return
    hi = jnp.maximum(xs[a], xs[b])
    lo = jnp.minimum(xs[a], xs[b])
    xs[a], xs[b] = hi, lo


def _bitonic_merge_desc(xs):
    n = len(xs)
    j = n // 2
    while j >= 1:
        for a in range(n):
            b = a ^ j
            if b > a:
                _cmpx(xs, a, b)
        j //= 2
    return xs


def _sort_desc(xs):
    xs = list(xs)
    n = len(xs)
    k = 2
    while k <= n:
        j = k // 2
        while j >= 1:
            for a in range(n):
                b = a ^ j
                if b > a:
                    if (a & k) == 0:
                        _cmpx(xs, a, b)
                    else:
                        _cmpx(xs, b, a)
            j //= 2
        k *= 2
    return xs


def _max_or_none(x, y):
    if x is None:
        return y
    if y is None:
        return x
    return jnp.maximum(x, y)


def _merge_top(xs, ys):
    n = len(xs)
    return _bitonic_merge_desc([_max_or_none(xs[v], ys[n - 1 - v]) for v in range(n)])


def _top_desc(rows, k):
    lists = [_sort_desc(rows[g:g + k]) for g in range(0, len(rows), k)]
    while len(lists) > 1:
        lists = [_merge_top(lists[g], lists[g + 1]) for g in range(0, len(lists), 2)]
    return lists[0]


def _route_kernel(x_ref, wq_ref, sk_ref, r1_ref, e1_ref, n0_ref, c0_ref, sc_ref, tmp_ref):
    half_dim = PEER_KEY_DIM // 2
    for sub in range(x_ref.shape[0] // LANES):
        toks = slice(sub * LANES, (sub + 1) * LANES)
        q = jnp.dot(x_ref[toks, :], wq_ref[...], preferred_element_type=F32).astype(BF16)
        for p in range(2):
            for h in range(PEER_HEADS):
                qhp = q[:, (2 * h + p) * half_dim:(2 * h + p + 1) * half_dim]
                sc_ref[sub, p, pl.ds(h, PEER_N_KEYS, stride=PEER_HEADS), :] = lax.dot_general(
                    sk_ref[p], qhp, _NT, preferred_element_type=F32)
        _route_tokens(sc_ref.at[sub], tmp_ref.at[sub], toks, r1_ref, e1_ref, n0_ref, c0_ref)


def _route_tokens(sc_ref, tmp_ref, toks, r1_ref, e1_ref, n0_ref, c0_ref):
    K = PEER_TOPK
    nk = PEER_N_KEYS
    H = PEER_HEADS
    tm = LANES
    inf = jnp.inf
    s0 = sc_ref[0].reshape(nk, H, tm)
    s1 = sc_ref[1].reshape(nk, H, tm)
    a = _top_desc([s0[k] for k in range(nk)], K)
    b = _top_desc([s1[k] for k in range(nk)], K)

    sums = [[a[r] + b[c] for c in range(K // (r + 1))] for r in range(K)]
    pad = lambda xs: xs + [None] * (K - len(xs))
    z = _merge_top(sums[0], pad(sums[1]))
    mid = [x for r in range(2, 7) for x in sums[r]]
    z = _merge_top(z, _sort_desc(pad(mid)))
    low = [x for r in range(7, K) for x in sums[r]]
    z = _merge_top(z, _sort_desc(pad(low)))
    tau = z[K - 1]
    zsum = jnp.ones_like(tau)
    for v in range(1, K):
        zsum = zsum + jnp.exp(z[v] - z[0])
    inv_z = 1.0 / zsum

    alphas = []
    for c in range(K):
        alpha = None
        for r in range(K // (c + 1)):
            cand = jnp.where(sums[r][c] >= tau, a[r], inf)
            alpha = cand if alpha is None else jnp.minimum(alpha, cand)
        alphas.append(alpha)

    for k0 in range(0, nk, ROUTE_KEY_CHUNK):
        blk0 = s0[k0:k0 + ROUTE_KEY_CHUNK]
        blk1 = s1[k0:k0 + ROUTE_KEY_CHUNK]
        count = jnp.zeros(blk0.shape, F32)
        for c in range(K):
            count = jnp.where(blk0 >= alphas[c][None], float(c + 1), count)
        rank = jnp.full(blk1.shape, float(K), F32)
        for c in reversed(range(K)):
            rank = jnp.where(blk1 >= b[c][None], float(c), rank)
        e1 = jnp.exp(blk1 - b[0][None])
        c0 = jnp.exp(blk0 - a[0][None]) * inv_z[None]
        rows = slice(k0 * H, (k0 + ROUTE_KEY_CHUNK) * H)
        for n, val in enumerate((rank, e1, count, c0)):
            tmp_ref[n, rows, :] = val.reshape(ROUTE_KEY_CHUNK * H, tm)

    for n, dst in enumerate((r1_ref, e1_ref, n0_ref, c0_ref)):
        for h in range(H):
            dst[h, :, toks] = tmp_ref[n, pl.ds(h, nk, stride=H), :].astype(dst.dtype)


def _route(xb, wq, sk):
    T = xb.shape[0]
    tm = ROUTE_TM
    H = PEER_HEADS
    nk = PEER_N_KEYS
    subs = tm // LANES
    spec = pl.BlockSpec((H, nk, tm), lambda t: (0, 0, t))
    half = jax.ShapeDtypeStruct((H, nk, T), BF16)
    word = jax.ShapeDtypeStruct((H, nk, T), F32)
    return pl.pallas_call(
        _route_kernel,
        grid=(T // tm,),
        in_specs=[pl.BlockSpec((tm, D_MODEL), lambda t: (t, 0)),
                  pl.BlockSpec(wq.shape, lambda t: (0, 0)),
                  pl.BlockSpec(sk.shape, lambda t: (0, 0, 0))],
        out_specs=[spec, spec, spec, spec],
        out_shape=[half, half, word, word],
        scratch_shapes=[pltpu.VMEM((subs, 2, nk * H, LANES), F32),
                        pltpu.VMEM((subs, 4, nk * H, LANES), F32)],
        compiler_params=_params(("parallel",)),
        name="peer_route",
    )(xb, wq, sk)


def _gelu_tanh(x):
    k = -2.0 * math.sqrt(2.0 / math.pi) * math.log2(math.e)
    z = x * (k + (k * 0.044715) * (x * x))
    return x / (1.0 + jnp.exp2(z))


def _peer_kernel(xb_ref, u_ref, vt_ref, r1_ref, e1_ref, n0_ref, c0_ref, x_ref, g_ref, b_ref,
                 o_ref, ob_ref, w_ref, act_ref, acc_ref, *, rows):
    e = pl.program_id(1)
    nk = PEER_N_KEYS
    tm = xb_ref.shape[0]
    pk = 2 * SUBLANES

    @pl.when(e == 0)
    def _():
        acc_ref[...] = jnp.zeros_like(acc_ref)

    tc = min(PEER_LANE_CHUNK, tm)

    def routing_weights(r):
        i = e * rows + r
        first = None
        for ch in range(tm // tc):
            ls = slice(ch * tc, (ch + 1) * tc)
            w = [None] * (nk // pk)
            for h in range(PEER_HEADS):
                count = jnp.broadcast_to(n0_ref[h, pl.ds(i, 1), ls], (pk, tc)).astype(BF16)
                weight = jnp.broadcast_to(c0_ref[h, pl.ds(i, 1), ls], (pk, tc)).astype(BF16)
                for g in range(nk // pk):
                    rs = slice(g * pk, (g + 1) * pk)
                    sel = jnp.where(r1_ref[h, rs, ls] < count, e1_ref[h, rs, ls], jnp.zeros((pk, tc), BF16))
                    term = sel * weight
                    w[g] = term if w[g] is None else w[g] + term
            for g in range(nk // pk):
                w_ref[r * nk + g * pk:r * nk + (g + 1) * pk, ls] = w[g]
            if first is None:
                first = pltpu.bitcast(w[0][:, 0:LANES], jnp.uint32)
        return first

    cr = PEER_CHUNK_ROWS
    zero = 0
    for c in range(rows // cr):
        for r in range(c * cr, (c + 1) * cr):
            bits = routing_weights(r)
        lo = 0 if c == 0 else pl.multiple_of(c * cr * nk + zero, cr * nk)
        ht = lax.dot_general(u_ref[pl.ds(lo, cr * nk), :], xb_ref[...], _NT,
                             preferred_element_type=F32)
        rows_c = slice(c * cr * nk, (c + 1) * cr * nk)
        act_ref[rows_c, :] = _gelu_tanh(ht.astype(BF16)) * w_ref[rows_c, :]
        zero = ((bits >> 16) >> 16)[0, 0].astype(jnp.int32)
    acc_ref[...] += jnp.dot(vt_ref[...], act_ref[...], preferred_element_type=F32)

    @pl.when(e == pl.num_programs(1) - 1)
    def _():
        y = _layer_norm(DN_ALPHA * x_ref[...] + acc_ref[...].T, g_ref[...], b_ref[...])
        o_ref[...] = y
        ob_ref[...] = y.astype(BF16)


def _peer(xb, x, u, vt, routing, g, b):
    T = xb.shape[0]
    tm = min(PEER_TM, T)
    rows = PEER_ROWS
    nk = PEER_N_KEYS
    rspec = pl.BlockSpec((PEER_HEADS, nk, tm), lambda t, e: (0, 0, t))
    tok = pl.BlockSpec((tm, D_MODEL), lambda t, e: (t, 0))
    const = pl.BlockSpec((1, D_MODEL), lambda t, e: (0, 0))
    return pl.pallas_call(
        functools.partial(_peer_kernel, rows=rows),
        grid=(T // tm, nk // rows),
        in_specs=[tok,
                  pl.BlockSpec((rows * nk, D_MODEL), lambda t, e: (e, 0)),
                  pl.BlockSpec((D_MODEL, rows * nk), lambda t, e: (0, e)),
                  rspec, rspec, rspec, rspec, tok, const, const],
        out_specs=[tok, tok],
        out_shape=[jax.ShapeDtypeStruct((T, D_MODEL), F32), jax.ShapeDtypeStruct((T, D_MODEL), BF16)],
        scratch_shapes=[pltpu.VMEM((rows * nk, tm), BF16),
                        pltpu.VMEM((rows * nk, tm), BF16),
                        pltpu.VMEM((D_MODEL, tm), F32)],
        compiler_params=_params(("parallel", "arbitrary"), PEER_SCHED_FLAGS),
        name="peer_dense",
    )(xb, u, vt, *routing, x, g, b)


def _rope_tables(S):
    pos = jnp.arange(S, dtype=F32)
    d = MOBA_HEAD_DIM
    inv = ROPE_THETA ** (-jnp.arange(0, d, 2, dtype=F32) / d)
    ang = pos[:, None] * inv[None, :]
    cos, sin = jnp.cos(ang), jnp.sin(ang)
    reps = LANES // d
    cos_m = jnp.tile(cos, (1, 2 * reps))
    sin_m = jnp.concatenate([jnp.tile(-sin, (1, reps)), jnp.tile(sin, (1, reps))], axis=1)
    d = RET_QK_DIM
    inv = 1.0 / (ROPE_THETA ** jnp.linspace(0.0, 1.0, d // 2, dtype=F32))
    ang = pos[:, None] * inv[None, :]
    cos, sin = jnp.cos(ang), jnp.sin(ang)
    cos_r = jnp.concatenate([cos, cos], axis=1)
    sin_r = jnp.concatenate([-sin, sin], axis=1)
    return cos_m, sin_m, cos_r, sin_r


def _moba_half_split(w):
    rows, cols = w.shape
    per = LANES // MOBA_HEAD_DIM
    return (w.reshape(rows, cols // LANES, per, 2, MOBA_HEAD_DIM // 2).transpose(0, 1, 3, 2, 4)
            .reshape(rows, cols))


def _ret_column_perm():
    within = np.concatenate([np.arange(0, RET_QK_DIM, 2), np.arange(1, RET_QK_DIM, 2)])
    return np.concatenate([h * RET_QK_DIM + within for h in range(RET_HEADS)])


def kernel(x, w_in, w_moba_out, w_ret_out, w_out, ln1_g, ln1_b, peer_w_query, peer_sub_keys,
           peer_u, peer_v, ln2_g, ln2_b):
    B, S, D = x.shape
    assert D == D_MODEL and S % MOBA_BLOCK == 0 and S % RET_CHUNK == 0
    T = B * S
    tabs = _rope_tables(S)
    perm = _ret_column_perm()
    o = IN_OFFSETS
    xf = x.reshape(T, D).astype(F32)
    xb = xf.astype(BF16)
    for l in range(DEPTH):
        w = w_in[l]
        w_main = jnp.concatenate(
            [_moba_half_split(w[:, o[0]:o[2]]), w[:, o[3]:o[4]][:, perm], w[:, o[4]:o[5]][:, perm], w[:, o[5]:o[9]]],
            axis=1).astype(BF16)
        w_vt = w[:, o[2]:o[3]].T.astype(BF16)
        proj = _inproj(xb, w_main, tabs, S)
        vt = _moba_values_t(xb, w_vt, S)
        yat = _moba(proj, vt, B, S)
        yr = _retention(proj, B, S)
        xf, xb = _merge(yat, yr, proj, xf, w_moba_out[l].astype(BF16), w_ret_out[l].astype(BF16),
                        w_out[l].astype(BF16), ln1_g[l].reshape(1, D), ln1_b[l].reshape(1, D), B, S)
        routing = _route(xb, peer_w_query[l].astype(BF16), peer_sub_keys[l].astype(BF16))
        xf, xb = _peer(xb, xf, peer_u[l].astype(BF16), peer_v[l].T.astype(BF16), routing,
                       ln2_g[l].reshape(1, D), ln2_b[l].reshape(1, D))
    return xf.reshape(B, S, D).astype(x.dtype)
```

```python
import functools
import math

import numpy as np
import jax
import jax.numpy as jnp
from jax import lax
from jax.experimental import pallas as pl
from jax.experimental.pallas import tpu as pltpu

F32 = jnp.float32
BF16 = jnp.bfloat16

D_MODEL = 1024
DEPTH = 2
MOBA_HEADS = 8
MOBA_HEAD_DIM = 64
MOBA_WIDTH = MOBA_HEADS * MOBA_HEAD_DIM
MOBA_BLOCK = 256
MOBA_TOPK = 3
ROPE_THETA = 10000.0
RET_HEADS = 4
RET_QK_DIM = 128
RET_V_DIM = 256
RET_QK_WIDTH = RET_HEADS * RET_QK_DIM
RET_V_WIDTH = RET_HEADS * RET_V_DIM
RET_CHUNK = 256
PEER_N_KEYS = 128
PEER_HEADS = 8
PEER_KEY_DIM = 256
PEER_TOPK = 16
DN_ALPHA = (2.0 * DEPTH) ** 0.25
LN_EPS = 1e-5
NEG = -1e30

IN_SIZES = (MOBA_WIDTH, MOBA_WIDTH, MOBA_WIDTH, RET_QK_WIDTH, RET_QK_WIDTH,
            RET_V_WIDTH, RET_V_WIDTH, D_MODEL, D_MODEL)
IN_OFFSETS = tuple(int(v) for v in np.concatenate([[0], np.cumsum(IN_SIZES)]))

LANES = 128
SUBLANES = 8
VMEM_LIMIT = 56 * 1024 * 1024

PROJ_TM = 1024
PROJ_TN = 1024
MERGE_TM = 512
ROUTE_TM = 512
ROUTE_KEY_CHUNK = 4
PEER_TM = 512
PEER_ROWS = 16
PEER_LANE_CHUNK = 256
PEER_CHUNK_ROWS = 2

_NT = (((1,), (1,)), ((), ()))


def _params(sem):
    return pltpu.CompilerParams(dimension_semantics=sem, vmem_limit_bytes=VMEM_LIMIT)


def _rotate_groups(acc, cos, sin, o_ref, partner_fn):
    for g in range(PROJ_TN // LANES):
        xg = acc[:, g * LANES:(g + 1) * LANES]
        o_ref[:, g * LANES:(g + 1) * LANES] = (xg * cos + partner_fn(xg) * sin).astype(o_ref.dtype)


def _inproj_kernel(x_ref, w_ref, cm_ref, sm_ref, cr_ref, sr_ref, o_ref):
    j = pl.program_id(1)
    acc = jnp.dot(x_ref[...], w_ref[...], preferred_element_type=F32)

    moba_tiles = 2 * MOBA_WIDTH // PROJ_TN
    rope_tiles = moba_tiles + 2 * RET_QK_WIDTH // PROJ_TN

    partner = lambda xg: pltpu.roll(xg, LANES // 2, axis=1)

    @pl.when(j < moba_tiles)
    def _():
        _rotate_groups(acc, cm_ref[...], sm_ref[...], o_ref, partner)

    @pl.when((j >= moba_tiles) & (j < rope_tiles))
    def _():
        _rotate_groups(acc, cr_ref[...], sr_ref[...], o_ref, partner)

    @pl.when(j >= rope_tiles)
    def _():
        o_ref[...] = acc.astype(o_ref.dtype)


def _inproj(xb, w_main, tabs, S):
    T = xb.shape[0]
    tm = min(PROJ_TM, S)
    n_col = w_main.shape[1] // PROJ_TN
    pos_blocks = S // tm
    tab_spec = pl.BlockSpec((tm, LANES), lambda i, j: (i % pos_blocks, 0))
    return pl.pallas_call(
        _inproj_kernel,
        grid=(T // tm, n_col),
        in_specs=[pl.BlockSpec((tm, D_MODEL), lambda i, j: (i, 0)),
                  pl.BlockSpec((D_MODEL, PROJ_TN), lambda i, j: (0, j)),
                  tab_spec, tab_spec, tab_spec, tab_spec],
        out_specs=pl.BlockSpec((tm, PROJ_TN), lambda i, j: (i, j)),
        out_shape=jax.ShapeDtypeStruct((T, w_main.shape[1]), BF16),
        compiler_params=_params(("parallel", "arbitrary")),
        name="inproj",
    )(xb, w_main, *tabs)


MOBA_VROWS = MOBA_HEAD_DIM + 2 * SUBLANES


def _vt_kernel(x_ref, w_ref, o_ref):
    res = lax.dot_general(w_ref[...], x_ref[...], _NT, preferred_element_type=F32)
    hd = MOBA_HEAD_DIM
    pad = MOBA_VROWS - hd
    ones_row = jnp.where(lax.broadcasted_iota(jnp.int32, (pad, MOBA_BLOCK), 0) == 0, 1.0, 0.0).astype(o_ref.dtype)
    for c in range(o_ref.shape[0]):
        for h in range(MOBA_HEADS):
            o_ref[c, h, 0:hd, :] = res[h * hd:(h + 1) * hd, c * MOBA_BLOCK:(c + 1) * MOBA_BLOCK].astype(o_ref.dtype)
            o_ref[c, h, hd:MOBA_VROWS, :] = ones_row


def _moba_values_t(xb, w_vt, S):
    T = xb.shape[0]
    tm = min(PROJ_TM, S)
    per = tm // MOBA_BLOCK
    return pl.pallas_call(
        _vt_kernel,
        grid=(T // tm,),
        in_specs=[pl.BlockSpec((tm, D_MODEL), lambda i: (i, 0)),
                  pl.BlockSpec((MOBA_WIDTH, D_MODEL), lambda i: (0, 0))],
        out_specs=pl.BlockSpec((per, MOBA_HEADS, MOBA_VROWS, MOBA_BLOCK), lambda i: (i, 0, 0, 0)),
        out_shape=jax.ShapeDtypeStruct((T // MOBA_BLOCK, MOBA_HEADS, MOBA_VROWS, MOBA_BLOCK), BF16),
        compiler_params=_params(("parallel",)),
        name="moba_vt",
    )(xb, w_vt)


def _moba_kernel(q_ref, k_ref, vt_ref, o_ref, kmean_ref, bias_ref, qs_ref, acc_ref, m_ref, sa_ref, sb_ref, *, nb):
    i = pl.program_id(1)
    L = MOBA_BLOCK
    hd = MOBA_HEAD_DIM

    @pl.when(i == 0)
    def _():
        def body(j, c):
            kb = k_ref[pl.ds(pl.multiple_of(j * L, L), L), :].astype(F32)
            kmean_ref[pl.ds(j, 1), :] = jnp.sum(kb, axis=0, keepdims=True) * (1.0 / L)
            return c
        lax.fori_loop(0, nb, body, 0)

    per = LANES // hd
    groups = MOBA_HEADS // per
    W = MOBA_HEADS * L
    lane = lax.broadcasted_iota(jnp.int32, (L, LANES), 1)
    row0 = pl.multiple_of(i * L, L)

    gates = []
    for g in range(groups):
        cols = slice(g * LANES, (g + 1) * LANES)
        q_pair = q_ref[:, cols]
        km = kmean_ref[:, cols].astype(BF16)
        for hh in range(per):
            qm = jnp.where((lane // (hd // 2)) % per == hh, q_pair, jnp.zeros_like(q_pair))
            gates.append(lax.dot_general(km, qm, _NT, preferred_element_type=F32))
            qs_ref[g, hh * L:(hh + 1) * L, :] = (qm.astype(F32) * (hd ** -0.5 * math.log2(math.e))).astype(BF16)

    blk = lax.broadcasted_iota(jnp.int32, (nb, W), 0)
    blkf = blk.astype(F32)
    gm = jnp.where(blk < i, jnp.concatenate(gates, axis=1), -jnp.inf)
    keep = jnp.zeros((nb, W), F32)
    for _ in range(MOBA_TOPK):
        best = jnp.max(gm, axis=0, keepdims=True)
        first = jnp.min(jnp.where(gm == best, blkf, float(nb)), axis=0, keepdims=True)
        hit = blkf == first
        keep = jnp.where(hit, 1.0, keep)
        gm = jnp.where(hit, -jnp.inf, gm)
    bias_ref[...] = jnp.where(blk < i, jnp.where(keep > 0.0, 0.0, NEG), NEG)

    def scores(r):
        return jnp.concatenate(
            [lax.dot_general(k_ref[pl.ds(r, L), g * LANES:(g + 1) * LANES], qs_ref[g], _NT,
                             preferred_element_type=F32) for g in range(groups)], axis=1)

    def values(j, p):
        pb = p.astype(BF16)
        return [jnp.dot(vt_ref[j, h], pb[:, h * L:(h + 1) * L], preferred_element_type=F32)
                for h in range(MOBA_HEADS)]

    kpos = lax.broadcasted_iota(jnp.int32, (L, W), 0)
    qpos = lax.broadcasted_iota(jnp.int32, (L, W), 1) % L
    s = jnp.where(kpos <= qpos, scores(row0), NEG)
    m = jnp.max(s, axis=0, keepdims=True)
    m_ref[...] = jnp.broadcast_to(m, (SUBLANES, W))
    for h, pv in enumerate(values(i, jnp.exp2(s - m))):
        acc_ref[h] = pv

    def scores_into(dst_ref, r):
        first = None
        for g in range(groups):
            sg = lax.dot_general(k_ref[pl.ds(r, L), g * LANES:(g + 1) * LANES], qs_ref[g], _NT,
                                 preferred_element_type=F32)
            dst_ref[:, g * per * L:(g + 1) * per * L] = sg
            if first is None:
                first = pltpu.bitcast(sg[0:SUBLANES, 0:LANES], jnp.uint32)
        return ((first >> 16) >> 16)[0, 0].astype(jnp.int32)

    def softmax_update(j, cur_ref, heads, row_start):
        lanes = slice(heads[0] * L, (heads[-1] + 1) * L)
        s = cur_ref[pl.ds(row_start, L), lanes]
        bias = bias_ref[pl.ds(j, 1), lanes]
        m_old = m_ref[:, lanes]
        m_new = jnp.maximum(m_old, jnp.max(s, axis=0, keepdims=True) + bias)
        alpha = jnp.exp2(m_old - m_new)
        m_ref[:, lanes] = m_new
        pb = jnp.exp2(s - (m_new[0:1, :] - bias)).astype(BF16)
        for n, h in enumerate(heads):
            pv = jnp.dot(vt_ref[j, h], pb[:, n * L:(n + 1) * L], preferred_element_type=F32)
            acc_ref[h] = alpha[0:1, n * L:(n + 1) * L] * acc_ref[h] + pv

    half = MOBA_HEADS // 2

    def past(j, cur_ref, nxt_ref):
        start = 0
        if nxt_ref is not None:
            nxt = jnp.minimum(j + 1, i - 1)
            start = pl.multiple_of(scores_into(nxt_ref, pl.multiple_of(nxt * L, L)), L)
        softmax_update(j, cur_ref, tuple(range(half)), 0)
        softmax_update(j, cur_ref, tuple(range(half, MOBA_HEADS)), start)

    scores_into(sa_ref, 0)

    def four_past(jj, c):
        past(4 * jj, sa_ref, sb_ref)
        past(4 * jj + 1, sb_ref, sa_ref)
        past(4 * jj + 2, sa_ref, sb_ref)
        past(4 * jj + 3, sb_ref, sa_ref)
        return c

    lax.fori_loop(0, i // 4, four_past, 0)

    rest = i % 4
    base = i - rest

    @pl.when(rest >= 1)
    def _():
        past(base, sa_ref, sb_ref)

    @pl.when(rest >= 2)
    def _():
        past(base + 1, sb_ref, sa_ref)

    @pl.when(rest >= 3)
    def _():
        past(base + 2, sa_ref, None)

    for h in range(MOBA_HEADS):
        o_ref[0, h * hd:(h + 1) * hd, :] = acc_ref[h, 0:hd, :] / acc_ref[h, hd:hd + 1, :]


def _moba(proj, vt, B, S):
    nb = S // MOBA_BLOCK
    L = MOBA_BLOCK
    return pl.pallas_call(
        functools.partial(_moba_kernel, nb=nb),
        grid=(B, nb),
        in_specs=[pl.BlockSpec((L, MOBA_WIDTH), lambda b, i: (b * nb + i, 0)),
                  pl.BlockSpec((S, MOBA_WIDTH), lambda b, i: (b, 1)),
                  pl.BlockSpec((nb, MOBA_HEADS, MOBA_VROWS, L), lambda b, i: (b, 0, 0, 0))],
        out_specs=pl.BlockSpec((1, MOBA_WIDTH, L), lambda b, i: (b, 0, i)),
        out_shape=jax.ShapeDtypeStruct((B, MOBA_WIDTH, S), F32),
        scratch_shapes=[pltpu.VMEM((nb, MOBA_WIDTH), F32),
                        pltpu.VMEM((nb, MOBA_HEADS * L), F32),
                        pltpu.VMEM((MOBA_HEADS // (LANES // MOBA_HEAD_DIM), (LANES // MOBA_HEAD_DIM) * L, LANES),
                                   BF16),
                        pltpu.VMEM((MOBA_HEADS, MOBA_VROWS, L), F32),
                        pltpu.VMEM((SUBLANES, MOBA_HEADS * L), F32),
                        pltpu.VMEM((L, MOBA_HEADS * L), F32),
                        pltpu.VMEM((L, MOBA_HEADS * L), F32)],
        compiler_params=_params(("parallel", "arbitrary")),
        name="moba",
    )(proj, proj, vt)


def _ret_log_g():
    return jnp.log(1.0 - 2.0 ** (-5.0 - jnp.arange(RET_HEADS, dtype=F32)))


def _retention_kernel(q_ref, k_ref, v_ref, g_ref, dec_ref, qd_ref, kd_ref, cd_ref, o_ref, state_ref):
    n = pl.program_id(1)

    @pl.when(n == 0)
    def _():
        state_ref[...] = jnp.zeros_like(state_ref)

    for h in range(RET_HEADS):
        qk = slice(h * RET_QK_DIM, (h + 1) * RET_QK_DIM)
        vv = slice(h * RET_V_DIM, (h + 1) * RET_V_DIM)
        q = q_ref[:, qk]
        k = k_ref[:, qk]
        v = v_ref[:, vv]
        scores = lax.dot_general(q, k, _NT, preferred_element_type=F32) * dec_ref[h]
        y = jnp.dot(scores.astype(BF16), v, preferred_element_type=F32)
        state = state_ref[h]
        y = y + jnp.dot(q, state.astype(BF16), preferred_element_type=F32) * qd_ref[h]
        kt = (k.astype(F32) * kd_ref[h]).T.astype(BF16)
        kv = jnp.dot(kt, v, preferred_element_type=F32)
        state_ref[h] = state * cd_ref[h:h + 1, :] + kv
        mu = jnp.mean(y, axis=-1, keepdims=True)
        yc = y - mu
        var = jnp.mean(yc * yc, axis=-1, keepdims=True)
        yn = yc * lax.rsqrt(var + LN_EPS)
        gate = g_ref[:, vv].astype(F32)
        o_ref[:, vv] = (gate * jax.nn.sigmoid(gate) * yn).astype(o_ref.dtype)


def _retention(proj, B, S):
    C = RET_CHUNK
    nc = S // C
    T = B * S
    log_g = _ret_log_g()
    pos = jnp.arange(C, dtype=F32)
    diff = pos[:, None] - pos[None, :]
    scale = RET_QK_DIM ** -0.5
    decay = jnp.where(diff >= 0, jnp.exp(log_g[:, None, None] * jnp.maximum(diff, 0.0)), 0.0) * scale
    q_decay = jnp.broadcast_to(jnp.exp(log_g[:, None] * (pos + 1.0))[:, :, None], (RET_HEADS, C, RET_V_DIM))
    k_decay = jnp.broadcast_to((jnp.exp(log_g[:, None] * (C - 1.0 - pos)) * scale)[:, :, None],
                               (RET_HEADS, C, RET_QK_DIM))
    chunk_decay = jnp.broadcast_to(jnp.exp(log_g * C)[:, None], (RET_HEADS, RET_V_DIM))
    const = lambda shape: pl.BlockSpec(shape, lambda b, n: (0,) * len(shape))
    return pl.pallas_call(
        _retention_kernel,
        grid=(B, nc),
        in_specs=[pl.BlockSpec((C, RET_QK_WIDTH), lambda b, n: (b * nc + n, 2)),
                  pl.BlockSpec((C, RET_QK_WIDTH), lambda b, n: (b * nc + n, 3)),
                  pl.BlockSpec((C, RET_V_WIDTH), lambda b, n: (b * nc + n, 2)),
                  pl.BlockSpec((C, RET_V_WIDTH), lambda b, n: (b * nc + n, 3)),
                  const((RET_HEADS, C, C)), const((RET_HEADS, C, RET_V_DIM)),
                  const((RET_HEADS, C, RET_QK_DIM)), const((RET_HEADS, RET_V_DIM))],
        out_specs=pl.BlockSpec((C, RET_V_WIDTH), lambda b, n: (b * nc + n, 0)),
        out_shape=jax.ShapeDtypeStruct((T, RET_V_WIDTH), BF16),
        scratch_shapes=[pltpu.VMEM((RET_HEADS, RET_QK_DIM, RET_V_DIM), F32)],
        compiler_params=_params(("parallel", "arbitrary")),
        name="retention",
    )(proj, proj, proj, proj, decay, q_decay, k_decay, chunk_decay)


def _layer_norm(y, g, b):
    mu = jnp.mean(y, axis=-1, keepdims=True)
    yc = y - mu
    var = jnp.mean(yc * yc, axis=-1, keepdims=True)
    return yc * lax.rsqrt(var + LN_EPS) * g + b


def _merge_kernel(yat_ref, yr_ref, ga_ref, gr_ref, x_ref, wa_ref, wr_ref, wo_ref, g_ref, b_ref, o_ref, ob_ref):
    ya = yat_ref[0].T.astype(BF16)
    branch_a = jnp.dot(ya, wa_ref[...], preferred_element_type=F32)
    branch_r = jnp.dot(yr_ref[...], wr_ref[...], preferred_element_type=F32)
    merged = (jax.nn.sigmoid(ga_ref[...].astype(F32)) * branch_a
              + jax.nn.sigmoid(gr_ref[...].astype(F32)) * branch_r)
    mix = jnp.dot(merged.astype(BF16), wo_ref[...], preferred_element_type=F32)
    y = _layer_norm(DN_ALPHA * x_ref[...] + mix, g_ref[...], b_ref[...])
    o_ref[...] = y
    ob_ref[...] = y.astype(BF16)


def _merge(yat, yr, proj, x, wa, wr, wo, g, b, B, S):
    L = min(MERGE_TM, S)
    nb = S // L
    T = B * S
    tok = lambda c: pl.BlockSpec((L, D_MODEL), lambda bb, i: (bb * nb + i, c))
    const = lambda shape: pl.BlockSpec(shape, lambda bb, i: (0,) * len(shape))
    return pl.pallas_call(
        _merge_kernel,
        grid=(B, nb),
        in_specs=[pl.BlockSpec((1, MOBA_WIDTH, L), lambda bb, i: (bb, 0, i)),
                  tok(0), tok(4), tok(5), tok(0),
                  const((MOBA_WIDTH, D_MODEL)), const((RET_V_WIDTH, D_MODEL)), const((D_MODEL, D_MODEL)),
                  const((1, D_MODEL)), const((1, D_MODEL))],
        out_specs=[tok(0), tok(0)],
        out_shape=[jax.ShapeDtypeStruct((T, D_MODEL), F32), jax.ShapeDtypeStruct((T, D_MODEL), BF16)],
        compiler_params=_params(("parallel", "parallel")),
        name="merge_ln1",
    )(yat, yr, proj, proj, x, wa, wr, wo, g, b)


def _cmpx(xs, a, b):
    if xs[b] is None:
        return
    if xs[a] is None:
        xs[a], xs[b] = xs[b], None
        return
    hi = jnp.maximum(xs[a], xs[b])
    lo = jnp.minimum(xs[a], xs[b])
    xs[a], xs[b] = hi, lo


def _bitonic_merge_desc(xs):
    n = len(xs)
    j = n // 2
    while j >= 1:
        for a in range(n):
            b = a ^ j
            if b > a:
                _cmpx(xs, a, b)
        j //= 2
    return xs


def _sort_desc(xs):
    xs = list(xs)
    n = len(xs)
    k = 2
    while k <= n:
        j = k // 2
        while j >= 1:
            for a in range(n):
                b = a ^ j
                if b > a:
                    if (a & k) == 0:
                        _cmpx(xs, a, b)
                    else:
                        _cmpx(xs, b, a)
            j //= 2
        k *= 2
    return xs


def _max_or_none(x, y):
    if x is None:
        return y
    if y is None:
        return x
    return jnp.maximum(x, y)


def _merge_top(xs, ys):
    n = len(xs)
    return _bitonic_merge_desc([_max_or_none(xs[v], ys[n - 1 - v]) for v in range(n)])


def _top_desc(rows, k):
    lists = [_sort_desc(rows[g:g + k]) for g in range(0, len(rows), k)]
    while len(lists) > 1:
        lists = [_merge_top(lists[g], lists[g + 1]) for g in range(0, len(lists), 2)]
    return lists[0]


def _route_kernel(x_ref, wq_ref, sk_ref, r1_ref, e1_ref, n0_ref, c0_ref, sc_ref, tmp_ref):
    half_dim = PEER_KEY_DIM // 2
    for sub in range(x_ref.shape[0] // LANES):
        toks = slice(sub * LANES, (sub + 1) * LANES)
        q = jnp.dot(x_ref[toks, :], wq_ref[...], preferred_element_type=F32).astype(BF16)
        for p in range(2):
            for h in range(PEER_HEADS):
                qhp = q[:, (2 * h + p) * half_dim:(2 * h + p + 1) * half_dim]
                sc_ref[sub, p, pl.ds(h, PEER_N_KEYS, stride=PEER_HEADS), :] = lax.dot_general(
                    sk_ref[p], qhp, _NT, preferred_element_type=F32)
        _route_tokens(sc_ref.at[sub], tmp_ref.at[sub], toks, r1_ref, e1_ref, n0_ref, c0_ref)


def _route_tokens(sc_ref, tmp_ref, toks, r1_ref, e1_ref, n0_ref, c0_ref):
    K = PEER_TOPK
    nk = PEER_N_KEYS
    H = PEER_HEADS
    tm = LANES
    inf = jnp.inf
    s0 = sc_ref[0].reshape(nk, H, tm)
    s1 = sc_ref[1].reshape(nk, H, tm)
    a = _top_desc([s0[k] for k in range(nk)], K)
    b = _top_desc([s1[k] for k in range(nk)], K)

    sums = [[a[r] + b[c] for c in range(K // (r + 1))] for r in range(K)]
    pad = lambda xs: xs + [None] * (K - len(xs))
    z = _merge_top(sums[0], pad(sums[1]))
    mid = [x for r in range(2, 7) for x in sums[r]]
    z = _merge_top(z, _sort_desc(pad(mid)))
    low = [x for r in range(7, K) for x in sums[r]]
    z = _merge_top(z, _sort_desc(pad(low)))
    tau = z[K - 1]
    zsum = jnp.ones_like(tau)
    for v in range(1, K):
        zsum = zsum + jnp.exp(z[v] - z[0])
    inv_z = 1.0 / zsum

    alphas = []
    for c in range(K):
        alpha = None
        for r in range(K // (c + 1)):
            cand = jnp.where(sums[r][c] >= tau, a[r], inf)
            alpha = cand if alpha is None else jnp.minimum(alpha, cand)
        alphas.append(alpha)

    for k0 in range(0, nk, ROUTE_KEY_CHUNK):
        blk0 = s0[k0:k0 + ROUTE_KEY_CHUNK]
        blk1 = s1[k0:k0 + ROUTE_KEY_CHUNK]
        count = jnp.zeros(blk0.shape, F32)
        for c in range(K):
            count = jnp.where(blk0 >= alphas[c][None], float(c + 1), count)
        rank = jnp.full(blk1.shape, float(K), F32)
        for c in reversed(range(K)):
            rank = jnp.where(blk1 >= b[c][None], float(c), rank)
        e1 = jnp.exp(blk1 - b[0][None])
        c0 = jnp.exp(blk0 - a[0][None]) * inv_z[None]
        rows = slice(k0 * H, (k0 + ROUTE_KEY_CHUNK) * H)
        for n, val in enumerate((rank, e1, count, c0)):
            tmp_ref[n, rows, :] = val.reshape(ROUTE_KEY_CHUNK * H, tm)

    for n, dst in enumerate((r1_ref, e1_ref, n0_ref, c0_ref)):
        for h in range(H):
            dst[h, :, toks] = tmp_ref[n, pl.ds(h, nk, stride=H), :].astype(dst.dtype)


def _route(xb, wq, sk):
    T = xb.shape[0]
    tm = ROUTE_TM
    H = PEER_HEADS
    nk = PEER_N_KEYS
    subs = tm // LANES
    spec = pl.BlockSpec((H, nk, tm), lambda t: (0, 0, t))
    half = jax.ShapeDtypeStruct((H, nk, T), BF16)
    word = jax.ShapeDtypeStruct((H, nk, T), F32)
    return pl.pallas_call(
        _route_kernel,
        grid=(T // tm,),
        in_specs=[pl.BlockSpec((tm, D_MODEL), lambda t: (t, 0)),
                  pl.BlockSpec(wq.shape, lambda t: (0, 0)),
                  pl.BlockSpec(sk.shape, lambda t: (0, 0, 0))],
        out_specs=[spec, spec, spec, spec],
        out_shape=[half, half, word, word],
        scratch_shapes=[pltpu.VMEM((subs, 2, nk * H, LANES), F32),
                        pltpu.VMEM((subs, 4, nk * H, LANES), F32)],
        compiler_params=_params(("parallel",)),
        name="peer_route",
    )(xb, wq, sk)


def _gelu_tanh(x):
    k = -2.0 * math.sqrt(2.0 / math.pi) * math.log2(math.e)
    z = x * (k + (k * 0.044715) * (x * x))
    return x / (1.0 + jnp.exp2(z))


def _peer_kernel(xb_ref, u_ref, vt_ref, r1_ref, e1_ref, n0_ref, c0_ref, x_ref, g_ref, b_ref,
                 o_ref, ob_ref, w_ref, act_ref, acc_ref, *, rows):
    e = pl.program_id(1)
    nk = PEER_N_KEYS
    tm = xb_ref.shape[0]
    pk = 2 * SUBLANES

    @pl.when(e == 0)
    def _():
        acc_ref[...] = jnp.zeros_like(acc_ref)

    tc = min(PEER_LANE_CHUNK, tm)

    def routing_weights(r):
        i = e * rows + r
        first = None
        for ch in range(tm // tc):
            ls = slice(ch * tc, (ch + 1) * tc)
            w = [None] * (nk // pk)
            for h in range(PEER_HEADS):
                count = jnp.broadcast_to(n0_ref[h, pl.ds(i, 1), ls], (pk, tc)).astype(BF16)
                weight = jnp.broadcast_to(c0_ref[h, pl.ds(i, 1), ls], (pk, tc)).astype(BF16)
                for g in range(nk // pk):
                    rs = slice(g * pk, (g + 1) * pk)
                    sel = jnp.where(r1_ref[h, rs, ls] < count, e1_ref[h, rs, ls], jnp.zeros((pk, tc), BF16))
                    term = sel * weight
                    w[g] = term if w[g] is None else w[g] + term
            for g in range(nk // pk):
                w_ref[r * nk + g * pk:r * nk + (g + 1) * pk, ls] = w[g]
            if first is None:
                first = pltpu.bitcast(w[0][:, 0:LANES], jnp.uint32)
        return first

    cr = PEER_CHUNK_ROWS
    zero = 0
    for c in range(rows // cr):
        bits = routing_weights(c * cr)
        for r in range(c * cr + 1, (c + 1) * cr):
            routing_weights(r)
        lo = 0 if c == 0 else pl.multiple_of(c * cr * nk + zero, cr * nk)
        ht = lax.dot_general(u_ref[pl.ds(lo, cr * nk), :], xb_ref[...], _NT,
                             preferred_element_type=F32)
        rows_c = slice(c * cr * nk, (c + 1) * cr * nk)
        act_ref[rows_c, :] = _gelu_tanh(ht.astype(BF16)) * w_ref[rows_c, :]
        zero = ((bits >> 16) >> 16)[0, 0].astype(jnp.int32)
    acc_ref[...] += jnp.dot(vt_ref[...], act_ref[...], preferred_element_type=F32)

    @pl.when(e == pl.num_programs(1) - 1)
    def _():
        y = _layer_norm(DN_ALPHA * x_ref[...] + acc_ref[...].T, g_ref[...], b_ref[...])
        o_ref[...] = y
        ob_ref[...] = y.astype(BF16)


def _peer(xb, x, u, vt, routing, g, b):
    T = xb.shape[0]
    tm = min(PEER_TM, T)
    rows = PEER_ROWS
    nk = PEER_N_KEYS
    rspec = pl.BlockSpec((PEER_HEADS, nk, tm), lambda t, e: (0, 0, t))
    tok = pl.BlockSpec((tm, D_MODEL), lambda t, e: (t, 0))
    const = pl.BlockSpec((1, D_MODEL), lambda t, e: (0, 0))
    return pl.pallas_call(
        functools.partial(_peer_kernel, rows=rows),
        grid=(T // tm, nk // rows),
        in_specs=[tok,
                  pl.BlockSpec((rows * nk, D_MODEL), lambda t, e: (e, 0)),
                  pl.BlockSpec((D_MODEL, rows * nk), lambda t, e: (0, e)),
                  rspec, rspec, rspec, rspec, tok, const, const],
        out_specs=[tok, tok],
        out_shape=[jax.ShapeDtypeStruct((T, D_MODEL), F32), jax.ShapeDtypeStruct((T, D_MODEL), BF16)],
        scratch_shapes=[pltpu.VMEM((rows * nk, tm), BF16),
                        pltpu.VMEM((rows * nk, tm), BF16),
                        pltpu.VMEM((D_MODEL, tm), F32)],
        compiler_params=_params(("parallel", "arbitrary")),
        name="peer_dense",
    )(xb, u, vt, *routing, x, g, b)


def _rope_tables(S):
    pos = jnp.arange(S, dtype=F32)
    d = MOBA_HEAD_DIM
    inv = ROPE_THETA ** (-jnp.arange(0, d, 2, dtype=F32) / d)
    ang = pos[:, None] * inv[None, :]
    cos, sin = jnp.cos(ang), jnp.sin(ang)
    reps = LANES // d
    cos_m = jnp.tile(cos, (1, 2 * reps))
    sin_m = jnp.concatenate([jnp.tile(-sin, (1, reps)), jnp.tile(sin, (1, reps))], axis=1)
    d = RET_QK_DIM
    inv = 1.0 / (ROPE_THETA ** jnp.linspace(0.0, 1.0, d // 2, dtype=F32))
    ang = pos[:, None] * inv[None, :]
    cos, sin = jnp.cos(ang), jnp.sin(ang)
    cos_r = jnp.concatenate([cos, cos], axis=1)
    sin_r = jnp.concatenate([-sin, sin], axis=1)
    return cos_m, sin_m, cos_r, sin_r


def _moba_half_split(w):
    rows, cols = w.shape
    per = LANES // MOBA_HEAD_DIM
    return (w.reshape(rows, cols // LANES, per, 2, MOBA_HEAD_DIM // 2).transpose(0, 1, 3, 2, 4)
            .reshape(rows, cols))


def _ret_column_perm():
    within = np.concatenate([np.arange(0, RET_QK_DIM, 2), np.arange(1, RET_QK_DIM, 2)])
    return np.concatenate([h * RET_QK_DIM + within for h in range(RET_HEADS)])


def kernel(x, w_in, w_moba_out, w_ret_out, w_out, ln1_g, ln1_b, peer_w_query, peer_sub_keys,
           peer_u, peer_v, ln2_g, ln2_b):
    B, S, D = x.shape
    assert D == D_MODEL and S % MOBA_BLOCK == 0 and S % RET_CHUNK == 0
    T = B * S
    tabs = _rope_tables(S)
    perm = _ret_column_perm()
    o = IN_OFFSETS
    xf = x.reshape(T, D).astype(F32)
    xb = xf.astype(BF16)
    for l in range(DEPTH):
        w = w_in[l]
        w_main = jnp.concatenate(
            [_moba_half_split(w[:, o[0]:o[2]]), w[:, o[3]:o[4]][:, perm], w[:, o[4]:o[5]][:, perm], w[:, o[5]:o[9]]],
            axis=1).astype(BF16)
        w_vt = w[:, o[2]:o[3]].T.astype(BF16)
        proj = _inproj(xb, w_main, tabs, S)
        vt = _moba_values_t(xb, w_vt, S)
        yat = _moba(proj, vt, B, S)
        yr = _retention(proj, B, S)
        xf, xb = _merge(yat, yr, proj, xf, w_moba_out[l].astype(BF16), w_ret_out[l].astype(BF16),
                        w_out[l].astype(BF16), ln1_g[l].reshape(1, D), ln1_b[l].reshape(1, D), B, S)
        routing = _route(xb, peer_w_query[l].astype(BF16), peer_sub_keys[l].astype(BF16))
        xf, xb = _peer(xb, xf, peer_u[l].astype(BF16), peer_v[l].T.astype(BF16), routing,
                       ln2_g[l].reshape(1, D), ln2_b[l].reshape(1, D))
    return xf.reshape(B, S, D).astype(x.dtype)
```

```python
import functools
import math

import numpy as np
import jax
import jax.numpy as jnp
from jax import lax
from jax.experimental import pallas as pl
from jax.experimental.pallas import tpu as pltpu

F32 = jnp.float32
BF16 = jnp.bfloat16

D_MODEL = 1024
DEPTH = 2
MOBA_HEADS = 8
MOBA_HEAD_DIM = 64
MOBA_WIDTH = MOBA_HEADS * MOBA_HEAD_DIM
MOBA_BLOCK = 256
MOBA_TOPK = 3
ROPE_THETA = 10000.0
RET_HEADS = 4
RET_QK_DIM = 128
RET_V_DIM = 256
RET_QK_WIDTH = RET_HEADS * RET_QK_DIM
RET_V_WIDTH = RET_HEADS * RET_V_DIM
RET_CHUNK = 256
PEER_N_KEYS = 128
PEER_HEADS = 8
PEER_KEY_DIM = 256
PEER_TOPK = 16
DN_ALPHA = (2.0 * DEPTH) ** 0.25
LN_EPS = 1e-5
NEG = -1e30

IN_SIZES = (MOBA_WIDTH, MOBA_WIDTH, MOBA_WIDTH, RET_QK_WIDTH, RET_QK_WIDTH,
            RET_V_WIDTH, RET_V_WIDTH, D_MODEL, D_MODEL)
IN_OFFSETS = tuple(int(v) for v in np.concatenate([[0], np.cumsum(IN_SIZES)]))

LANES = 128
SUBLANES = 8
VMEM_LIMIT = 56 * 1024 * 1024

PROJ_TM = 1024
PROJ_TN = 1024
MERGE_TM = 512
ROUTE_TM = 512
ROUTE_KEY_CHUNK = 4
PEER_TM = 512
PEER_ROWS = 16
PEER_LANE_CHUNK = 256
PEER_CHUNK_ROWS = 2

_NT = (((1,), (1,)), ((), ()))


def _params(sem):
    return pltpu.CompilerParams(dimension_semantics=sem, vmem_limit_bytes=VMEM_LIMIT)


def _rotate_groups(acc, cos, sin, o_ref, partner_fn):
    for g in range(PROJ_TN // LANES):
        xg = acc[:, g * LANES:(g + 1) * LANES]
        o_ref[:, g * LANES:(g + 1) * LANES] = (xg * cos + partner_fn(xg) * sin).astype(o_ref.dtype)


def _inproj_kernel(x_ref, w_ref, cm_ref, sm_ref, cr_ref, sr_ref, o_ref):
    j = pl.program_id(1)
    acc = jnp.dot(x_ref[...], w_ref[...], preferred_element_type=F32)

    moba_tiles = 2 * MOBA_WIDTH // PROJ_TN
    rope_tiles = moba_tiles + 2 * RET_QK_WIDTH // PROJ_TN

    partner = lambda xg: pltpu.roll(xg, LANES // 2, axis=1)

    @pl.when(j < moba_tiles)
    def _():
        _rotate_groups(acc, cm_ref[...], sm_ref[...], o_ref, partner)

    @pl.when((j >= moba_tiles) & (j < rope_tiles))
    def _():
        _rotate_groups(acc, cr_ref[...], sr_ref[...], o_ref, partner)

    @pl.when(j >= rope_tiles)
    def _():
        o_ref[...] = acc.astype(o_ref.dtype)


def _inproj(xb, w_main, tabs, S):
    T = xb.shape[0]
    tm = min(PROJ_TM, S)
    n_col = w_main.shape[1] // PROJ_TN
    pos_blocks = S // tm
    tab_spec = pl.BlockSpec((tm, LANES), lambda i, j: (i % pos_blocks, 0))
    return pl.pallas_call(
        _inproj_kernel,
        grid=(T // tm, n_col),
        in_specs=[pl.BlockSpec((tm, D_MODEL), lambda i, j: (i, 0)),
                  pl.BlockSpec((D_MODEL, PROJ_TN), lambda i, j: (0, j)),
                  tab_spec, tab_spec, tab_spec, tab_spec],
        out_specs=pl.BlockSpec((tm, PROJ_TN), lambda i, j: (i, j)),
        out_shape=jax.ShapeDtypeStruct((T, w_main.shape[1]), BF16),
        compiler_params=_params(("parallel", "arbitrary")),
        name="inproj",
    )(xb, w_main, *tabs)


MOBA_VROWS = MOBA_HEAD_DIM + 2 * SUBLANES


def _vt_kernel(x_ref, w_ref, o_ref):
    res = lax.dot_general(w_ref[...], x_ref[...], _NT, preferred_element_type=F32)
    hd = MOBA_HEAD_DIM
    pad = MOBA_VROWS - hd
    ones_row = jnp.where(lax.broadcasted_iota(jnp.int32, (pad, MOBA_BLOCK), 0) == 0, 1.0, 0.0).astype(o_ref.dtype)
    for c in range(o_ref.shape[0]):
        for h in range(MOBA_HEADS):
            o_ref[c, h, 0:hd, :] = res[h * hd:(h + 1) * hd, c * MOBA_BLOCK:(c + 1) * MOBA_BLOCK].astype(o_ref.dtype)
            o_ref[c, h, hd:MOBA_VROWS, :] = ones_row


def _moba_values_t(xb, w_vt, S):
    T = xb.shape[0]
    tm = min(PROJ_TM, S)
    per = tm // MOBA_BLOCK
    return pl.pallas_call(
        _vt_kernel,
        grid=(T // tm,),
        in_specs=[pl.BlockSpec((tm, D_MODEL), lambda i: (i, 0)),
                  pl.BlockSpec((MOBA_WIDTH, D_MODEL), lambda i: (0, 0))],
        out_specs=pl.BlockSpec((per, MOBA_HEADS, MOBA_VROWS, MOBA_BLOCK), lambda i: (i, 0, 0, 0)),
        out_shape=jax.ShapeDtypeStruct((T // MOBA_BLOCK, MOBA_HEADS, MOBA_VROWS, MOBA_BLOCK), BF16),
        compiler_params=_params(("parallel",)),
        name="moba_vt",
    )(xb, w_vt)


def _moba_kernel(q_ref, k_ref, vt_ref, o_ref, kmean_ref, bias_ref, qs_ref, acc_ref, m_ref, sa_ref, sb_ref, *, nb):
    i = pl.program_id(1)
    L = MOBA_BLOCK
    hd = MOBA_HEAD_DIM

    @pl.when(i == 0)
    def _():
        def body(j, c):
            kb = k_ref[pl.ds(pl.multiple_of(j * L, L), L), :].astype(F32)
            kmean_ref[pl.ds(j, 1), :] = jnp.sum(kb, axis=0, keepdims=True) * (1.0 / L)
            return c
        lax.fori_loop(0, nb, body, 0)

    per = LANES // hd
    groups = MOBA_HEADS // per
    W = MOBA_HEADS * L
    lane = lax.broadcasted_iota(jnp.int32, (L, LANES), 1)
    row0 = pl.multiple_of(i * L, L)

    gates = []
    for g in range(groups):
        cols = slice(g * LANES, (g + 1) * LANES)
        q_pair = q_ref[:, cols]
        km = kmean_ref[:, cols].astype(BF16)
        for hh in range(per):
            qm = jnp.where((lane // (hd // 2)) % per == hh, q_pair, jnp.zeros_like(q_pair))
            gates.append(lax.dot_general(km, qm, _NT, preferred_element_type=F32))
            qs_ref[g, hh * L:(hh + 1) * L, :] = (qm.astype(F32) * (hd ** -0.5 * math.log2(math.e))).astype(BF16)

    blk = lax.broadcasted_iota(jnp.int32, (nb, W), 0)
    blkf = blk.astype(F32)
    gm = jnp.where(blk < i, jnp.concatenate(gates, axis=1), -jnp.inf)
    keep = jnp.zeros((nb, W), F32)
    for _ in range(MOBA_TOPK):
        best = jnp.max(gm, axis=0, keepdims=True)
        first = jnp.min(jnp.where(gm == best, blkf, float(nb)), axis=0, keepdims=True)
        hit = blkf == first
        keep = jnp.where(hit, 1.0, keep)
        gm = jnp.where(hit, -jnp.inf, gm)
    bias_ref[...] = jnp.where(blk < i, jnp.where(keep > 0.0, 0.0, NEG), NEG)

    def scores(r):
        return jnp.concatenate(
            [lax.dot_general(k_ref[pl.ds(r, L), g * LANES:(g + 1) * LANES], qs_ref[g], _NT,
                             preferred_element_type=F32) for g in range(groups)], axis=1)

    def values(j, p):
        pb = p.astype(BF16)
        return [jnp.dot(vt_ref[j, h], pb[:, h * L:(h + 1) * L], preferred_element_type=F32)
                for h in range(MOBA_HEADS)]

    kpos = lax.broadcasted_iota(jnp.int32, (L, W), 0)
    qpos = lax.broadcasted_iota(jnp.int32, (L, W), 1) % L
    s = jnp.where(kpos <= qpos, scores(row0), NEG)
    m = jnp.max(s, axis=0, keepdims=True)
    m_ref[...] = jnp.broadcast_to(m, (SUBLANES, W))
    for h, pv in enumerate(values(i, jnp.exp2(s - m))):
        acc_ref[h] = pv

    def scores_into(dst_ref, r):
        first = None
        for g in range(groups):
            sg = lax.dot_general(k_ref[pl.ds(r, L), g * LANES:(g + 1) * LANES], qs_ref[g], _NT,
                                 preferred_element_type=F32)
            dst_ref[:, g * per * L:(g + 1) * per * L] = sg
            if first is None:
                first = pltpu.bitcast(sg[0:SUBLANES, 0:LANES], jnp.uint32)
        return ((first >> 16) >> 16)[0, 0].astype(jnp.int32)

    def softmax_update(j, cur_ref, heads, row_start):
        lanes = slice(heads[0] * L, (heads[-1] + 1) * L)
        s = cur_ref[pl.ds(row_start, L), lanes]
        bias = bias_ref[pl.ds(j, 1), lanes]
        m_old = m_ref[:, lanes]
        m_new = jnp.maximum(m_old, jnp.max(s, axis=0, keepdims=True) + bias)
        alpha = jnp.exp2(m_old - m_new)
        m_ref[:, lanes] = m_new
        pb = jnp.exp2(s - (m_new[0:1, :] - bias)).astype(BF16)
        for n, h in enumerate(heads):
            pv = jnp.dot(vt_ref[j, h], pb[:, n * L:(n + 1) * L], preferred_element_type=F32)
            acc_ref[h] = alpha[0:1, n * L:(n + 1) * L] * acc_ref[h] + pv

    half = MOBA_HEADS // 2

    def past(j, cur_ref, nxt_ref):
        start = 0
        if nxt_ref is not None:
            nxt = jnp.minimum(j + 1, i - 1)
            start = pl.multiple_of(scores_into(nxt_ref, pl.multiple_of(nxt * L, L)), L)
        softmax_update(j, cur_ref, tuple(range(half)), 0)
        softmax_update(j, cur_ref, tuple(range(half, MOBA_HEADS)), start)

    scores_into(sa_ref, 0)

    def four_past(jj, c):
        past(4 * jj, sa_ref, sb_ref)
        past(4 * jj + 1, sb_ref, sa_ref)
        past(4 * jj + 2, sa_ref, sb_ref)
        past(4 * jj + 3, sb_ref, sa_ref)
        return c

    lax.fori_loop(0, i // 4, four_past, 0)

    rest = i % 4
    base = i - rest

    @pl.when(rest >= 1)
    def _():
        past(base, sa_ref, sb_ref)

    @pl.when(rest >= 2)
    def _():
        past(base + 1, sb_ref, sa_ref)

    @pl.when(rest >= 3)
    def _():
        past(base + 2, sa_ref, None)

    for h in range(MOBA_HEADS):
        o_ref[0, h * hd:(h + 1) * hd, :] = acc_ref[h, 0:hd, :] / acc_ref[h, hd:hd + 1, :]


def _moba(proj, vt, B, S):
    nb = S // MOBA_BLOCK
    L = MOBA_BLOCK
    return pl.pallas_call(
        functools.partial(_moba_kernel, nb=nb),
        grid=(B, nb),
        in_specs=[pl.BlockSpec((L, MOBA_WIDTH), lambda b, i: (b * nb + i, 0)),
                  pl.BlockSpec((S, MOBA_WIDTH), lambda b, i: (b, 1)),
                  pl.BlockSpec((nb, MOBA_HEADS, MOBA_VROWS, L), lambda b, i: (b, 0, 0, 0))],
        out_specs=pl.BlockSpec((1, MOBA_WIDTH, L), lambda b, i: (b, 0, i)),
        out_shape=jax.ShapeDtypeStruct((B, MOBA_WIDTH, S), F32),
        scratch_shapes=[pltpu.VMEM((nb, MOBA_WIDTH), F32),
                        pltpu.VMEM((nb, MOBA_HEADS * L), F32),
                        pltpu.VMEM((MOBA_HEADS // (LANES // MOBA_HEAD_DIM), (LANES // MOBA_HEAD_DIM) * L, LANES),
                                   BF16),
                        pltpu.VMEM((MOBA_HEADS, MOBA_VROWS, L), F32),
                        pltpu.VMEM((SUBLANES, MOBA_HEADS * L), F32),
                        pltpu.VMEM((L, MOBA_HEADS * L), F32),
                        pltpu.VMEM((L, MOBA_HEADS * L), F32)],
        compiler_params=_params(("parallel", "arbitrary")),
        name="moba",
    )(proj, proj, vt)


def _ret_log_g():
    return jnp.log(1.0 - 2.0 ** (-5.0 - jnp.arange(RET_HEADS, dtype=F32)))


def _retention_kernel(q_ref, k_ref, v_ref, g_ref, dec_ref, qd_ref, kd_ref, cd_ref, o_ref, state_ref):
    n = pl.program_id(1)

    @pl.when(n == 0)
    def _():
        state_ref[...] = jnp.zeros_like(state_ref)

    for h in range(RET_HEADS):
        qk = slice(h * RET_QK_DIM, (h + 1) * RET_QK_DIM)
        vv = slice(h * RET_V_DIM, (h + 1) * RET_V_DIM)
        q = q_ref[:, qk]
        k = k_ref[:, qk]
        v = v_ref[:, vv]
        scores = lax.dot_general(q, k, _NT, preferred_element_type=F32) * dec_ref[h]
        y = jnp.dot(scores.astype(BF16), v, preferred_element_type=F32)
        state = state_ref[h]
        y = y + jnp.dot(q, state.astype(BF16), preferred_element_type=F32) * qd_ref[h]
        kt = (k.astype(F32) * kd_ref[h]).T.astype(BF16)
        kv = jnp.dot(kt, v, preferred_element_type=F32)
        state_ref[h] = state * cd_ref[h:h + 1, :] + kv
        mu = jnp.mean(y, axis=-1, keepdims=True)
        yc = y - mu
        var = jnp.mean(yc * yc, axis=-1, keepdims=True)
        yn = yc * lax.rsqrt(var + LN_EPS)
        gate = g_ref[:, vv].astype(F32)
        o_ref[:, vv] = (gate * jax.nn.sigmoid(gate) * yn).astype(o_ref.dtype)


def _retention(proj, B, S):
    C = RET_CHUNK
    nc = S // C
    T = B * S
    log_g = _ret_log_g()
    pos = jnp.arange(C, dtype=F32)
    diff = pos[:, None] - pos[None, :]
    scale = RET_QK_DIM ** -0.5
    decay = jnp.where(diff >= 0, jnp.exp(log_g[:, None, None] * jnp.maximum(diff, 0.0)), 0.0) * scale
    q_decay = jnp.broadcast_to(jnp.exp(log_g[:, None] * (pos + 1.0))[:, :, None], (RET_HEADS, C, RET_V_DIM))
    k_decay = jnp.broadcast_to((jnp.exp(log_g[:, None] * (C - 1.0 - pos)) * scale)[:, :, None],
                               (RET_HEADS, C, RET_QK_DIM))
    chunk_decay = jnp.broadcast_to(jnp.exp(log_g * C)[:, None], (RET_HEADS, RET_V_DIM))
    const = lambda shape: pl.BlockSpec(shape, lambda b, n: (0,) * len(shape))
    return pl.pallas_call(
        _retention_kernel,
        grid=(B, nc),
        in_specs=[pl.BlockSpec((C, RET_QK_WIDTH), lambda b, n: (b * nc + n, 2)),
                  pl.BlockSpec((C, RET_QK_WIDTH), lambda b, n: (b * nc + n, 3)),
                  pl.BlockSpec((C, RET_V_WIDTH), lambda b, n: (b * nc + n, 2)),
                  pl.BlockSpec((C, RET_V_WIDTH), lambda b, n: (b * nc + n, 3)),
                  const((RET_HEADS, C, C)), const((RET_HEADS, C, RET_V_DIM)),
                  const((RET_HEADS, C, RET_QK_DIM)), const((RET_HEADS, RET_V_DIM))],
        out_specs=pl.BlockSpec((C, RET_V_WIDTH), lambda b, n: (b * nc + n, 0)),
        out_shape=jax.ShapeDtypeStruct((T, RET_V_WIDTH), BF16),
        scratch_shapes=[pltpu.VMEM((RET_HEADS, RET_QK_DIM, RET_V_DIM), F32)],
        compiler_params=_params(("parallel", "arbitrary")),
        name="retention",
    )(proj, proj, proj, proj, decay, q_decay, k_decay, chunk_decay)


def _layer_norm(y, g, b):
    mu = jnp.mean(y, axis=-1, keepdims=True)
    yc = y - mu
    var = jnp.mean(yc * yc, axis=-1, keepdims=True)
    return yc * lax.rsqrt(var + LN_EPS) * g + b


def _merge_kernel(yat_ref, yr_ref, ga_ref, gr_ref, x_ref, wa_ref, wr_ref, wo_ref, g_ref, b_ref, o_ref, ob_ref):
    ya = yat_ref[0].T.astype(BF16)
    branch_a = jnp.dot(ya, wa_ref[...], preferred_element_type=F32)
    branch_r = jnp.dot(yr_ref[...], wr_ref[...], preferred_element_type=F32)
    merged = (jax.nn.sigmoid(ga_ref[...].astype(F32)) * branch_a
              + jax.nn.sigmoid(gr_ref[...].astype(F32)) * branch_r)
    mix = jnp.dot(merged.astype(BF16), wo_ref[...], preferred_element_type=F32)
    y = _layer_norm(DN_ALPHA * x_ref[...] + mix, g_ref[...], b_ref[...])
    o_ref[...] = y
    ob_ref[...] = y.astype(BF16)


def _merge(yat, yr, proj, x, wa, wr, wo, g, b, B, S):
    L = min(MERGE_TM, S)
    nb = S // L
    T = B * S
    tok = lambda c: pl.BlockSpec((L, D_MODEL), lambda bb, i: (bb * nb + i, c))
    const = lambda shape: pl.BlockSpec(shape, lambda bb, i: (0,) * len(shape))
    return pl.pallas_call(
        _merge_kernel,
        grid=(B, nb),
        in_specs=[pl.BlockSpec((1, MOBA_WIDTH, L), lambda bb, i: (bb, 0, i)),
                  tok(0), tok(4), tok(5), tok(0),
                  const((MOBA_WIDTH, D_MODEL)), const((RET_V_WIDTH, D_MODEL)), const((D_MODEL, D_MODEL)),
                  const((1, D_MODEL)), const((1, D_MODEL))],
        out_specs=[tok(0), tok(0)],
        out_shape=[jax.ShapeDtypeStruct((T, D_MODEL), F32), jax.ShapeDtypeStruct((T, D_MODEL), BF16)],
        compiler_params=_params(("parallel", "parallel")),
        name="merge_ln1",
    )(yat, yr, proj, proj, x, wa, wr, wo, g, b)


def _cmpx(xs, a, b):
    if xs[b] is None:
        return
    if xs[a] is None:
        xs[a], xs[b] = xs[b], None
        return
    hi = jnp.maximum(xs[a], xs[b])
    lo = jnp.minimum(xs[a], xs[b])
    xs[a], xs[b] = hi, lo


def _bitonic_merge_desc(xs):
    n = len(xs)
    j = n // 2
    while j >= 1:
        for a in range(n):
            b = a ^ j
            if b > a:
                _cmpx(xs, a, b)
        j //= 2
    return xs


def _sort_desc(xs):
    xs = list(xs)
    n = len(xs)
    k = 2
    while k <= n:
        j = k // 2
        while j >= 1:
            for a in range(n):
                b = a ^ j
                if b > a:
                    if (a & k) == 0:
                        _cmpx(xs, a, b)
                    else:
                        _cmpx(xs, b, a)
            j //= 2
        k *= 2
    return xs


def _max_or_none(x, y):
    if x is None:
        return y
    if y is None:
        return x
    return jnp.maximum(x, y)


def _merge_top(xs, ys):
    n = len(xs)
    return _bitonic_merge_desc([_max_or_none(xs[v], ys[n - 1 - v]) for v in range(n)])


def _top_desc(rows, k):
    lists = [_sort_desc(rows[g:g + k]) for g in range(0, len(rows), k)]
    while len(lists) > 1:
        lists = [_merge_top(lists[g], lists[g + 1]) for g in range(0, len(lists), 2)]
    return lists[0]


def _route_kernel(x_ref, wq_ref, sk_ref, r1_ref, e1_ref, n0_ref, c0_ref, sc_ref, tmp_ref):
    half_dim = PEER_KEY_DIM // 2
    for sub in range(x_ref.shape[0] // LANES):
        toks = slice(sub * LANES, (sub + 1) * LANES)
        q = jnp.dot(x_ref[toks, :], wq_ref[...], preferred_element_type=F32).astype(BF16)
        for p in range(2):
            for h in range(PEER_HEADS):
                qhp = q[:, (2 * h + p) * half_dim:(2 * h + p + 1) * half_dim]
                sc_ref[sub, p, pl.ds(h, PEER_N_KEYS, stride=PEER_HEADS), :] = lax.dot_general(
                    sk_ref[p], qhp, _NT, preferred_element_type=F32)
        _route_tokens(sc_ref.at[sub], tmp_ref.at[sub], toks, r1_ref, e1_ref, n0_ref, c0_ref)


def _route_tokens(sc_ref, tmp_ref, toks, r1_ref, e1_ref, n0_ref, c0_ref):
    K = PEER_TOPK
    nk = PEER_N_KEYS
    H = PEER_HEADS
    tm = LANES
    inf = jnp.inf
    s0 = sc_ref[0].reshape(nk, H, tm)
    s1 = sc_ref[1].reshape(nk, H, tm)
    a = _top_desc([s0[k] for k in range(nk)], K)
    b = _top_desc([s1[k] for k in range(nk)], K)

    sums = [[a[r] + b[c] for c in range(K // (r + 1))] for r in range(K)]
    pad = lambda xs: xs + [None] * (K - len(xs))
    z = _merge_top(sums[0], pad(sums[1]))
    mid = [x for r in range(2, 7) for x in sums[r]]
    z = _merge_top(z, _sort_desc(pad(mid)))
    low = [x for r in range(7, K) for x in sums[r]]
    z = _merge_top(z, _sort_desc(pad(low)))
    tau = z[K - 1]
    zsum = jnp.ones_like(tau)
    for v in range(1, K):
        zsum = zsum + jnp.exp(z[v] - z[0])
    inv_z = 1.0 / zsum

    alphas = []
    for c in range(K):
        alpha = None
        for r in range(K // (c + 1)):
            cand = jnp.where(sums[r][c] >= tau, a[r], inf)
            alpha = cand if alpha is None else jnp.minimum(alpha, cand)
        alphas.append(alpha)

    for k0 in range(0, nk, ROUTE_KEY_CHUNK):
        blk0 = s0[k0:k0 + ROUTE_KEY_CHUNK]
        blk1 = s1[k0:k0 + ROUTE_KEY_CHUNK]
        count = jnp.zeros(blk0.shape, F32)
        for c in range(K):
            count = jnp.where(blk0 >= alphas[c][None], float(c + 1), count)
        rank = jnp.full(blk1.shape, float(K), F32)
        for c in reversed(range(K)):
            rank = jnp.where(blk1 >= b[c][None], float(c), rank)
        e1 = jnp.exp(blk1 - b[0][None])
        c0 = jnp.exp(blk0 - a[0][None]) * inv_z[None]
        rows = slice(k0 * H, (k0 + ROUTE_KEY_CHUNK) * H)
        for n, val in enumerate((rank, e1, count, c0)):
            tmp_ref[n, rows, :] = val.reshape(ROUTE_KEY_CHUNK * H, tm)

    for n, dst in enumerate((r1_ref, e1_ref, n0_ref, c0_ref)):
        for h in range(H):
            dst[h, :, toks] = tmp_ref[n, pl.ds(h, nk, stride=H), :].astype(dst.dtype)


def _route(xb, wq, sk):
    T = xb.shape[0]
    tm = ROUTE_TM
    H = PEER_HEADS
    nk = PEER_N_KEYS
    subs = tm // LANES
    spec = pl.BlockSpec((H, nk, tm), lambda t: (0, 0, t))
    half = jax.ShapeDtypeStruct((H, nk, T), BF16)
    word = jax.ShapeDtypeStruct((H, nk, T), F32)
    return pl.pallas_call(
        _route_kernel,
        grid=(T // tm,),
        in_specs=[pl.BlockSpec((tm, D_MODEL), lambda t: (t, 0)),
                  pl.BlockSpec(wq.shape, lambda t: (0, 0)),
                  pl.BlockSpec(sk.shape, lambda t: (0, 0, 0))],
        out_specs=[spec, spec, spec, spec],
        out_shape=[half, half, word, word],
        scratch_shapes=[pltpu.VMEM((subs, 2, nk * H, LANES), F32),
                        pltpu.VMEM((subs, 4, nk * H, LANES), F32)],
        compiler_params=_params(("parallel",)),
        name="peer_route",
    )(xb, wq, sk)


FP8 = jnp.float8_e4m3fn
PEER_U_SCALE = 32.0


def _gelu_tanh_scaled(y):
    k = -2.0 * math.sqrt(2.0 / math.pi) * math.log2(math.e) / PEER_U_SCALE
    z = y * (k + (k * 0.044715 / PEER_U_SCALE ** 2) * (y * y))
    return y / (1.0 + jnp.exp2(z))


def _peer_kernel(xb_ref, u_ref, vt_ref, r1_ref, e1_ref, n0_ref, c0_ref, x_ref, g_ref, b_ref,
                 o_ref, ob_ref, w_ref, act_ref, acc_ref, x8_ref, *, rows):
    e = pl.program_id(1)
    nk = PEER_N_KEYS
    tm = xb_ref.shape[0]
    pk = 2 * SUBLANES

    @pl.when(e == 0)
    def _():
        acc_ref[...] = jnp.zeros_like(acc_ref)
        x8_ref[...] = xb_ref[...].astype(FP8)

    tc = min(PEER_LANE_CHUNK, tm)

    def routing_weights(r):
        i = e * rows + r
        first = None
        for ch in range(tm // tc):
            ls = slice(ch * tc, (ch + 1) * tc)
            w = [None] * (nk // pk)
            for h in range(PEER_HEADS):
                count = jnp.broadcast_to(n0_ref[h, pl.ds(i, 1), ls], (pk, tc)).astype(BF16)
                weight = jnp.broadcast_to(c0_ref[h, pl.ds(i, 1), ls], (pk, tc)).astype(BF16)
                for g in range(nk // pk):
                    rs = slice(g * pk, (g + 1) * pk)
                    sel = jnp.where(r1_ref[h, rs, ls] < count, e1_ref[h, rs, ls], jnp.zeros((pk, tc), BF16))
                    term = sel * weight
                    w[g] = term if w[g] is None else w[g] + term
            for g in range(nk // pk):
                w_ref[r * nk + g * pk:r * nk + (g + 1) * pk, ls] = w[g]
            if first is None:
                first = pltpu.bitcast(w[0][:, 0:LANES], jnp.uint32)
        return first

    cr = PEER_CHUNK_ROWS
    zero = 0
    for c in range(rows // cr):
        bits = routing_weights(c * cr)
        for r in range(c * cr + 1, (c + 1) * cr):
            routing_weights(r)
        lo = 0 if c == 0 else pl.multiple_of(c * cr * nk + zero, cr * nk)
        ht = lax.dot_general(u_ref[pl.ds(lo, cr * nk), :], x8_ref[...], _NT,
                             preferred_element_type=F32)
        rows_c = slice(c * cr * nk, (c + 1) * cr * nk)
        act_ref[rows_c, :] = (_gelu_tanh_scaled(ht.astype(BF16)) * w_ref[rows_c, :]).astype(FP8)
        zero = ((bits >> 16) >> 16)[0, 0].astype(jnp.int32)
    acc_ref[...] += jnp.dot(vt_ref[...], act_ref[...], preferred_element_type=F32)

    @pl.when(e == pl.num_programs(1) - 1)
    def _():
        ffn = acc_ref[...].T * (1.0 / PEER_U_SCALE)
        y = _layer_norm(DN_ALPHA * x_ref[...] + ffn, g_ref[...], b_ref[...])
        o_ref[...] = y
        ob_ref[...] = y.astype(BF16)


def _peer(xb, x, u, vt, routing, g, b):
    T = xb.shape[0]
    tm = min(PEER_TM, T)
    rows = PEER_ROWS
    nk = PEER_N_KEYS
    rspec = pl.BlockSpec((PEER_HEADS, nk, tm), lambda t, e: (0, 0, t))
    tok = pl.BlockSpec((tm, D_MODEL), lambda t, e: (t, 0))
    const = pl.BlockSpec((1, D_MODEL), lambda t, e: (0, 0))
    return pl.pallas_call(
        functools.partial(_peer_kernel, rows=rows),
        grid=(T // tm, nk // rows),
        in_specs=[tok,
                  pl.BlockSpec((rows * nk, D_MODEL), lambda t, e: (e, 0)),
                  pl.BlockSpec((D_MODEL, rows * nk), lambda t, e: (0, e)),
                  rspec, rspec, rspec, rspec, tok, const, const],
        out_specs=[tok, tok],
        out_shape=[jax.ShapeDtypeStruct((T, D_MODEL), F32), jax.ShapeDtypeStruct((T, D_MODEL), BF16)],
        scratch_shapes=[pltpu.VMEM((rows * nk, tm), BF16),
                        pltpu.VMEM((rows * nk, tm), FP8),
                        pltpu.VMEM((D_MODEL, tm), F32),
                        pltpu.VMEM((tm, D_MODEL), FP8)],
        compiler_params=_params(("parallel", "arbitrary")),
        name="peer_dense",
    )(xb, u, vt, *routing, x, g, b)


def _rope_tables(S):
    pos = jnp.arange(S, dtype=F32)
    d = MOBA_HEAD_DIM
    inv = ROPE_THETA ** (-jnp.arange(0, d, 2, dtype=F32) / d)
    ang = pos[:, None] * inv[None, :]
    cos, sin = jnp.cos(ang), jnp.sin(ang)
    reps = LANES // d
    cos_m = jnp.tile(cos, (1, 2 * reps))
    sin_m = jnp.concatenate([jnp.tile(-sin, (1, reps)), jnp.tile(sin, (1, reps))], axis=1)
    d = RET_QK_DIM
    inv = 1.0 / (ROPE_THETA ** jnp.linspace(0.0, 1.0, d // 2, dtype=F32))
    ang = pos[:, None] * inv[None, :]
    cos, sin = jnp.cos(ang), jnp.sin(ang)
    cos_r = jnp.concatenate([cos, cos], axis=1)
    sin_r = jnp.concatenate([-sin, sin], axis=1)
    return cos_m, sin_m, cos_r, sin_r


def _moba_half_split(w):
    rows, cols = w.shape
    per = LANES // MOBA_HEAD_DIM
    return (w.reshape(rows, cols // LANES, per, 2, MOBA_HEAD_DIM // 2).transpose(0, 1, 3, 2, 4)
            .reshape(rows, cols))


def _ret_column_perm():
    within = np.concatenate([np.arange(0, RET_QK_DIM, 2), np.arange(1, RET_QK_DIM, 2)])
    return np.concatenate([h * RET_QK_DIM + within for h in range(RET_HEADS)])


def kernel(x, w_in, w_moba_out, w_ret_out, w_out, ln1_g, ln1_b, peer_w_query, peer_sub_keys,
           peer_u, peer_v, ln2_g, ln2_b):
    B, S, D = x.shape
    assert D == D_MODEL and S % MOBA_BLOCK == 0 and S % RET_CHUNK == 0
    T = B * S
    tabs = _rope_tables(S)
    perm = _ret_column_perm()
    o = IN_OFFSETS
    xf = x.reshape(T, D).astype(F32)
    xb = xf.astype(BF16)
    for l in range(DEPTH):
        w = w_in[l]
        w_main = jnp.concatenate(
            [_moba_half_split(w[:, o[0]:o[2]]), w[:, o[3]:o[4]][:, perm], w[:, o[4]:o[5]][:, perm], w[:, o[5]:o[9]]],
            axis=1).astype(BF16)
        w_vt = w[:, o[2]:o[3]].T.astype(BF16)
        proj = _inproj(xb, w_main, tabs, S)
        vt = _moba_values_t(xb, w_vt, S)
        yat = _moba(proj, vt, B, S)
        yr = _retention(proj, B, S)
        xf, xb = _merge(yat, yr, proj, xf, w_moba_out[l].astype(BF16), w_ret_out[l].astype(BF16),
                        w_out[l].astype(BF16), ln1_g[l].reshape(1, D), ln1_b[l].reshape(1, D), B, S)
        routing = _route(xb, peer_w_query[l].astype(BF16), peer_sub_keys[l].astype(BF16))
        xf, xb = _peer(xb, xf, (peer_u[l] * PEER_U_SCALE).astype(FP8), peer_v[l].T.astype(FP8), routing,
                       ln2_g[l].reshape(1, D), ln2_b[l].reshape(1, D))
    return xf.reshape(B, S, D).astype(x.dtype)
```

```python
import functools
import math

import numpy as np
import jax
import jax.numpy as jnp
from jax import lax
from jax.experimental import pallas as pl
from jax.experimental.pallas import tpu as pltpu

F32 = jnp.float32
BF16 = jnp.bfloat16

D_MODEL = 1024
DEPTH = 2
MOBA_HEADS = 8
MOBA_HEAD_DIM = 64
MOBA_WIDTH = MOBA_HEADS * MOBA_HEAD_DIM
MOBA_BLOCK = 256
MOBA_TOPK = 3
ROPE_THETA = 10000.0
RET_HEADS = 4
RET_QK_DIM = 128
RET_V_DIM = 256
RET_QK_WIDTH = RET_HEADS * RET_QK_DIM
RET_V_WIDTH = RET_HEADS * RET_V_DIM
RET_CHUNK = 256
PEER_N_KEYS = 128
PEER_HEADS = 8
PEER_KEY_DIM = 256
PEER_TOPK = 16
DN_ALPHA = (2.0 * DEPTH) ** 0.25
LN_EPS = 1e-5
NEG = -1e30

IN_SIZES = (MOBA_WIDTH, MOBA_WIDTH, MOBA_WIDTH, RET_QK_WIDTH, RET_QK_WIDTH,
            RET_V_WIDTH, RET_V_WIDTH, D_MODEL, D_MODEL)
IN_OFFSETS = tuple(int(v) for v in np.concatenate([[0], np.cumsum(IN_SIZES)]))

LANES = 128
SUBLANES = 8
VMEM_LIMIT = 56 * 1024 * 1024

PROJ_TM = 1024
PROJ_TN = 1024
MERGE_TM = 512
ROUTE_TM = 512
ROUTE_KEY_CHUNK = 4
PEER_TM = 512
PEER_ROWS = 32
PEER_LANE_CHUNK = 256
PEER_CHUNK_ROWS = 2

_NT = (((1,), (1,)), ((), ()))


def _params(sem):
    return pltpu.CompilerParams(dimension_semantics=sem, vmem_limit_bytes=VMEM_LIMIT)


def _rotate_groups(acc, cos, sin, o_ref, partner_fn):
    for g in range(PROJ_TN // LANES):
        xg = acc[:, g * LANES:(g + 1) * LANES]
        o_ref[:, g * LANES:(g + 1) * LANES] = (xg * cos + partner_fn(xg) * sin).astype(o_ref.dtype)


def _inproj_kernel(x_ref, w_ref, cm_ref, sm_ref, cr_ref, sr_ref, o_ref):
    j = pl.program_id(1)
    acc = jnp.dot(x_ref[...], w_ref[...], preferred_element_type=F32)

    moba_tiles = 2 * MOBA_WIDTH // PROJ_TN
    rope_tiles = moba_tiles + 2 * RET_QK_WIDTH // PROJ_TN

    partner = lambda xg: pltpu.roll(xg, LANES // 2, axis=1)

    @pl.when(j < moba_tiles)
    def _():
        _rotate_groups(acc, cm_ref[...], sm_ref[...], o_ref, partner)

    @pl.when((j >= moba_tiles) & (j < rope_tiles))
    def _():
        _rotate_groups(acc, cr_ref[...], sr_ref[...], o_ref, partner)

    @pl.when(j >= rope_tiles)
    def _():
        o_ref[...] = acc.astype(o_ref.dtype)


def _inproj(xb, w_main, tabs, S):
    T = xb.shape[0]
    tm = min(PROJ_TM, S)
    n_col = w_main.shape[1] // PROJ_TN
    pos_blocks = S // tm
    tab_spec = pl.BlockSpec((tm, LANES), lambda i, j: (i % pos_blocks, 0))
    return pl.pallas_call(
        _inproj_kernel,
        grid=(T // tm, n_col),
        in_specs=[pl.BlockSpec((tm, D_MODEL), lambda i, j: (i, 0)),
                  pl.BlockSpec((D_MODEL, PROJ_TN), lambda i, j: (0, j)),
                  tab_spec, tab_spec, tab_spec, tab_spec],
        out_specs=pl.BlockSpec((tm, PROJ_TN), lambda i, j: (i, j)),
        out_shape=jax.ShapeDtypeStruct((T, w_main.shape[1]), BF16),
        compiler_params=_params(("parallel", "arbitrary")),
        name="inproj",
    )(xb, w_main, *tabs)


MOBA_VROWS = MOBA_HEAD_DIM + 2 * SUBLANES


def _vt_kernel(x_ref, w_ref, o_ref):
    res = lax.dot_general(w_ref[...], x_ref[...], _NT, preferred_element_type=F32)
    hd = MOBA_HEAD_DIM
    pad = MOBA_VROWS - hd
    ones_row = jnp.where(lax.broadcasted_iota(jnp.int32, (pad, MOBA_BLOCK), 0) == 0, 1.0, 0.0).astype(o_ref.dtype)
    for c in range(o_ref.shape[0]):
        for h in range(MOBA_HEADS):
            o_ref[c, h, 0:hd, :] = res[h * hd:(h + 1) * hd, c * MOBA_BLOCK:(c + 1) * MOBA_BLOCK].astype(o_ref.dtype)
            o_ref[c, h, hd:MOBA_VROWS, :] = ones_row


def _moba_values_t(xb, w_vt, S):
    T = xb.shape[0]
    tm = min(PROJ_TM, S)
    per = tm // MOBA_BLOCK
    return pl.pallas_call(
        _vt_kernel,
        grid=(T // tm,),
        in_specs=[pl.BlockSpec((tm, D_MODEL), lambda i: (i, 0)),
                  pl.BlockSpec((MOBA_WIDTH, D_MODEL), lambda i: (0, 0))],
        out_specs=pl.BlockSpec((per, MOBA_HEADS, MOBA_VROWS, MOBA_BLOCK), lambda i: (i, 0, 0, 0)),
        out_shape=jax.ShapeDtypeStruct((T // MOBA_BLOCK, MOBA_HEADS, MOBA_VROWS, MOBA_BLOCK), BF16),
        compiler_params=_params(("parallel",)),
        name="moba_vt",
    )(xb, w_vt)


def _moba_kernel(q_ref, k_ref, vt_ref, o_ref, kmean_ref, bias_ref, qs_ref, acc_ref, m_ref, sa_ref, sb_ref, *, nb):
    i = pl.program_id(1)
    L = MOBA_BLOCK
    hd = MOBA_HEAD_DIM

    @pl.when(i == 0)
    def _():
        def body(j, c):
            kb = k_ref[pl.ds(pl.multiple_of(j * L, L), L), :].astype(F32)
            kmean_ref[pl.ds(j, 1), :] = jnp.sum(kb, axis=0, keepdims=True) * (1.0 / L)
            return c
        lax.fori_loop(0, nb, body, 0)

    per = LANES // hd
    groups = MOBA_HEADS // per
    W = MOBA_HEADS * L
    lane = lax.broadcasted_iota(jnp.int32, (L, LANES), 1)
    row0 = pl.multiple_of(i * L, L)

    gates = []
    for g in range(groups):
        cols = slice(g * LANES, (g + 1) * LANES)
        q_pair = q_ref[:, cols]
        km = kmean_ref[:, cols].astype(BF16)
        for hh in range(per):
            qm = jnp.where((lane // (hd // 2)) % per == hh, q_pair, jnp.zeros_like(q_pair))
            gates.append(lax.dot_general(km, qm, _NT, preferred_element_type=F32))
            qs_ref[g, hh * L:(hh + 1) * L, :] = (qm.astype(F32) * (hd ** -0.5 * math.log2(math.e))).astype(BF16)

    blk = lax.broadcasted_iota(jnp.int32, (nb, W), 0)
    blkf = blk.astype(F32)
    gm = jnp.where(blk < i, jnp.concatenate(gates, axis=1), -jnp.inf)
    keep = jnp.zeros((nb, W), F32)
    for _ in range(MOBA_TOPK):
        best = jnp.max(gm, axis=0, keepdims=True)
        first = jnp.min(jnp.where(gm == best, blkf, float(nb)), axis=0, keepdims=True)
        hit = blkf == first
        keep = jnp.where(hit, 1.0, keep)
        gm = jnp.where(hit, -jnp.inf, gm)
    bias_ref[...] = jnp.where(blk < i, jnp.where(keep > 0.0, 0.0, NEG), NEG)

    def scores(r):
        return jnp.concatenate(
            [lax.dot_general(k_ref[pl.ds(r, L), g * LANES:(g + 1) * LANES], qs_ref[g], _NT,
                             preferred_element_type=F32) for g in range(groups)], axis=1)

    def values(j, p):
        pb = p.astype(BF16)
        return [jnp.dot(vt_ref[j, h], pb[:, h * L:(h + 1) * L], preferred_element_type=F32)
                for h in range(MOBA_HEADS)]

    kpos = lax.broadcasted_iota(jnp.int32, (L, W), 0)
    qpos = lax.broadcasted_iota(jnp.int32, (L, W), 1) % L
    s = jnp.where(kpos <= qpos, scores(row0), NEG)
    m = jnp.max(s, axis=0, keepdims=True)
    m_ref[...] = jnp.broadcast_to(m, (SUBLANES, W))
    for h, pv in enumerate(values(i, jnp.exp2(s - m))):
        acc_ref[h] = pv

    def scores_into(dst_ref, r):
        first = None
        for g in range(groups):
            sg = lax.dot_general(k_ref[pl.ds(r, L), g * LANES:(g + 1) * LANES], qs_ref[g], _NT,
                                 preferred_element_type=F32)
            dst_ref[:, g * per * L:(g + 1) * per * L] = sg
            if first is None:
                first = pltpu.bitcast(sg[0:SUBLANES, 0:LANES], jnp.uint32)
        return ((first >> 16) >> 16)[0, 0].astype(jnp.int32)

    def softmax_update(j, cur_ref, heads, row_start):
        lanes = slice(heads[0] * L, (heads[-1] + 1) * L)
        s = cur_ref[pl.ds(row_start, L), lanes]
        bias = bias_ref[pl.ds(j, 1), lanes]
        m_old = m_ref[:, lanes]
        m_new = jnp.maximum(m_old, jnp.max(s, axis=0, keepdims=True) + bias)
        alpha = jnp.exp2(m_old - m_new)
        m_ref[:, lanes] = m_new
        pb = jnp.exp2(s - (m_new[0:1, :] - bias)).astype(BF16)
        for n, h in enumerate(heads):
            pv = jnp.dot(vt_ref[j, h], pb[:, n * L:(n + 1) * L], preferred_element_type=F32)
            acc_ref[h] = alpha[0:1, n * L:(n + 1) * L] * acc_ref[h] + pv

    half = MOBA_HEADS // 2

    def past(j, cur_ref, nxt_ref):
        start = 0
        if nxt_ref is not None:
            nxt = jnp.minimum(j + 1, i - 1)
            start = pl.multiple_of(scores_into(nxt_ref, pl.multiple_of(nxt * L, L)), L)
        softmax_update(j, cur_ref, tuple(range(half)), 0)
        softmax_update(j, cur_ref, tuple(range(half, MOBA_HEADS)), start)

    scores_into(sa_ref, 0)

    def four_past(jj, c):
        past(4 * jj, sa_ref, sb_ref)
        past(4 * jj + 1, sb_ref, sa_ref)
        past(4 * jj + 2, sa_ref, sb_ref)
        past(4 * jj + 3, sb_ref, sa_ref)
        return c

    lax.fori_loop(0, i // 4, four_past, 0)

    rest = i % 4
    base = i - rest

    @pl.when(rest >= 1)
    def _():
        past(base, sa_ref, sb_ref)

    @pl.when(rest >= 2)
    def _():
        past(base + 1, sb_ref, sa_ref)

    @pl.when(rest >= 3)
    def _():
        past(base + 2, sa_ref, None)

    for h in range(MOBA_HEADS):
        o_ref[0, h * hd:(h + 1) * hd, :] = acc_ref[h, 0:hd, :] / acc_ref[h, hd:hd + 1, :]


def _moba(proj, vt, B, S):
    nb = S // MOBA_BLOCK
    L = MOBA_BLOCK
    return pl.pallas_call(
        functools.partial(_moba_kernel, nb=nb),
        grid=(B, nb),
        in_specs=[pl.BlockSpec((L, MOBA_WIDTH), lambda b, i: (b * nb + i, 0)),
                  pl.BlockSpec((S, MOBA_WIDTH), lambda b, i: (b, 1)),
                  pl.BlockSpec((nb, MOBA_HEADS, MOBA_VROWS, L), lambda b, i: (b, 0, 0, 0))],
        out_specs=pl.BlockSpec((1, MOBA_WIDTH, L), lambda b, i: (b, 0, i)),
        out_shape=jax.ShapeDtypeStruct((B, MOBA_WIDTH, S), F32),
        scratch_shapes=[pltpu.VMEM((nb, MOBA_WIDTH), F32),
                        pltpu.VMEM((nb, MOBA_HEADS * L), F32),
                        pltpu.VMEM((MOBA_HEADS // (LANES // MOBA_HEAD_DIM), (LANES // MOBA_HEAD_DIM) * L, LANES),
                                   BF16),
                        pltpu.VMEM((MOBA_HEADS, MOBA_VROWS, L), F32),
                        pltpu.VMEM((SUBLANES, MOBA_HEADS * L), F32),
                        pltpu.VMEM((L, MOBA_HEADS * L), F32),
                        pltpu.VMEM((L, MOBA_HEADS * L), F32)],
        compiler_params=_params(("parallel", "arbitrary")),
        name="moba",
    )(proj, proj, vt)


def _ret_log_g():
    return jnp.log(1.0 - 2.0 ** (-5.0 - jnp.arange(RET_HEADS, dtype=F32)))


def _retention_kernel(q_ref, k_ref, v_ref, g_ref, dec_ref, qd_ref, kd_ref, cd_ref, o_ref, state_ref):
    n = pl.program_id(1)

    @pl.when(n == 0)
    def _():
        state_ref[...] = jnp.zeros_like(state_ref)

    for h in range(RET_HEADS):
        qk = slice(h * RET_QK_DIM, (h + 1) * RET_QK_DIM)
        vv = slice(h * RET_V_DIM, (h + 1) * RET_V_DIM)
        q = q_ref[:, qk]
        k = k_ref[:, qk]
        v = v_ref[:, vv]
        scores = lax.dot_general(q, k, _NT, preferred_element_type=F32) * dec_ref[h]
        y = jnp.dot(scores.astype(BF16), v, preferred_element_type=F32)
        state = state_ref[h]
        y = y + jnp.dot(q, state.astype(BF16), preferred_element_type=F32) * qd_ref[h]
        kt = (k.astype(F32) * kd_ref[h]).T.astype(BF16)
        kv = jnp.dot(kt, v, preferred_element_type=F32)
        state_ref[h] = state * cd_ref[h:h + 1, :] + kv
        mu = jnp.mean(y, axis=-1, keepdims=True)
        yc = y - mu
        var = jnp.mean(yc * yc, axis=-1, keepdims=True)
        yn = yc * lax.rsqrt(var + LN_EPS)
        gate = g_ref[:, vv].astype(F32)
        o_ref[:, vv] = (gate * jax.nn.sigmoid(gate) * yn).astype(o_ref.dtype)


def _retention(proj, B, S):
    C = RET_CHUNK
    nc = S // C
    T = B * S
    log_g = _ret_log_g()
    pos = jnp.arange(C, dtype=F32)
    diff = pos[:, None] - pos[None, :]
    scale = RET_QK_DIM ** -0.5
    decay = jnp.where(diff >= 0, jnp.exp(log_g[:, None, None] * jnp.maximum(diff, 0.0)), 0.0) * scale
    q_decay = jnp.broadcast_to(jnp.exp(log_g[:, None] * (pos + 1.0))[:, :, None], (RET_HEADS, C, RET_V_DIM))
    k_decay = jnp.broadcast_to((jnp.exp(log_g[:, None] * (C - 1.0 - pos)) * scale)[:, :, None],
                               (RET_HEADS, C, RET_QK_DIM))
    chunk_decay = jnp.broadcast_to(jnp.exp(log_g * C)[:, None], (RET_HEADS, RET_V_DIM))
    const = lambda shape: pl.BlockSpec(shape, lambda b, n: (0,) * len(shape))
    return pl.pallas_call(
        _retention_kernel,
        grid=(B, nc),
        in_specs=[pl.BlockSpec((C, RET_QK_WIDTH), lambda b, n: (b * nc + n, 2)),
                  pl.BlockSpec((C, RET_QK_WIDTH), lambda b, n: (b * nc + n, 3)),
                  pl.BlockSpec((C, RET_V_WIDTH), lambda b, n: (b * nc + n, 2)),
                  pl.BlockSpec((C, RET_V_WIDTH), lambda b, n: (b * nc + n, 3)),
                  const((RET_HEADS, C, C)), const((RET_HEADS, C, RET_V_DIM)),
                  const((RET_HEADS, C, RET_QK_DIM)), const((RET_HEADS, RET_V_DIM))],
        out_specs=pl.BlockSpec((C, RET_V_WIDTH), lambda b, n: (b * nc + n, 0)),
        out_shape=jax.ShapeDtypeStruct((T, RET_V_WIDTH), BF16),
        scratch_shapes=[pltpu.VMEM((RET_HEADS, RET_QK_DIM, RET_V_DIM), F32)],
        compiler_params=_params(("parallel", "arbitrary")),
        name="retention",
    )(proj, proj, proj, proj, decay, q_decay, k_decay, chunk_decay)


def _layer_norm(y, g, b):
    mu = jnp.mean(y, axis=-1, keepdims=True)
    yc = y - mu
    var = jnp.mean(yc * yc, axis=-1, keepdims=True)
    return yc * lax.rsqrt(var + LN_EPS) * g + b


def _merge_kernel(yat_ref, yr_ref, ga_ref, gr_ref, x_ref, wa_ref, wr_ref, wo_ref, g_ref, b_ref, o_ref, ob_ref):
    ya = yat_ref[0].T.astype(BF16)
    branch_a = jnp.dot(ya, wa_ref[...], preferred_element_type=F32)
    branch_r = jnp.dot(yr_ref[...], wr_ref[...], preferred_element_type=F32)
    merged = (jax.nn.sigmoid(ga_ref[...].astype(F32)) * branch_a
              + jax.nn.sigmoid(gr_ref[...].astype(F32)) * branch_r)
    mix = jnp.dot(merged.astype(BF16), wo_ref[...], preferred_element_type=F32)
    y = _layer_norm(DN_ALPHA * x_ref[...] + mix, g_ref[...], b_ref[...])
    o_ref[...] = y
    ob_ref[...] = y.astype(BF16)


def _merge(yat, yr, proj, x, wa, wr, wo, g, b, B, S):
    L = min(MERGE_TM, S)
    nb = S // L
    T = B * S
    tok = lambda c: pl.BlockSpec((L, D_MODEL), lambda bb, i: (bb * nb + i, c))
    const = lambda shape: pl.BlockSpec(shape, lambda bb, i: (0,) * len(shape))
    return pl.pallas_call(
        _merge_kernel,
        grid=(B, nb),
        in_specs=[pl.BlockSpec((1, MOBA_WIDTH, L), lambda bb, i: (bb, 0, i)),
                  tok(0), tok(4), tok(5), tok(0),
                  const((MOBA_WIDTH, D_MODEL)), const((RET_V_WIDTH, D_MODEL)), const((D_MODEL, D_MODEL)),
                  const((1, D_MODEL)), const((1, D_MODEL))],
        out_specs=[tok(0), tok(0)],
        out_shape=[jax.ShapeDtypeStruct((T, D_MODEL), F32), jax.ShapeDtypeStruct((T, D_MODEL), BF16)],
        compiler_params=_params(("parallel", "parallel")),
        name="merge_ln1",
    )(yat, yr, proj, proj, x, wa, wr, wo, g, b)


def _cmpx(xs, a, b):
    if xs[b] is None:
        return
    if xs[a] is None:
        xs[a], xs[b] = xs[b], None
        return
    hi = jnp.maximum(xs[a], xs[b])
    lo = jnp.minimum(xs[a], xs[b])
    xs[a], xs[b] = hi, lo


def _bitonic_merge_desc(xs):
    n = len(xs)
    j = n // 2
    while j >= 1:
        for a in range(n):
            b = a ^ j
            if b > a:
                _cmpx(xs, a, b)
        j //= 2
    return xs


def _sort_desc(xs):
    xs = list(xs)
    n = len(xs)
    k = 2
    while k <= n:
        j = k // 2
        while j >= 1:
            for a in range(n):
                b = a ^ j
                if b > a:
                    if (a & k) == 0:
                        _cmpx(xs, a, b)
                    else:
                        _cmpx(xs, b, a)
            j //= 2
        k *= 2
    return xs


def _max_or_none(x, y):
    if x is None:
        return y
    if y is None:
        return x
    return jnp.maximum(x, y)


def _merge_top(xs, ys):
    n = len(xs)
    return _bitonic_merge_desc([_max_or_none(xs[v], ys[n - 1 - v]) for v in range(n)])


def _top_desc(rows, k):
    lists = [_sort_desc(rows[g:g + k]) for g in range(0, len(rows), k)]
    while len(lists) > 1:
        lists = [_merge_top(lists[g], lists[g + 1]) for g in range(0, len(lists), 2)]
    return lists[0]


def _route_kernel(x_ref, wq_ref, sk_ref, r1_ref, e1_ref, n0_ref, c0_ref, sc_ref, tmp_ref):
    half_dim = PEER_KEY_DIM // 2
    for sub in range(x_ref.shape[0] // LANES):
        toks = slice(sub * LANES, (sub + 1) * LANES)
        q = jnp.dot(x_ref[toks, :], wq_ref[...], preferred_element_type=F32).astype(BF16)
        for p in range(2):
            for h in range(PEER_HEADS):
                qhp = q[:, (2 * h + p) * half_dim:(2 * h + p + 1) * half_dim]
                sc_ref[sub, p, pl.ds(h, PEER_N_KEYS, stride=PEER_HEADS), :] = lax.dot_general(
                    sk_ref[p], qhp, _NT, preferred_element_type=F32)
        _route_tokens(sc_ref.at[sub], tmp_ref.at[sub], toks, r1_ref, e1_ref, n0_ref, c0_ref)


def _route_tokens(sc_ref, tmp_ref, toks, r1_ref, e1_ref, n0_ref, c0_ref):
    K = PEER_TOPK
    nk = PEER_N_KEYS
    H = PEER_HEADS
    tm = LANES
    inf = jnp.inf
    s0 = sc_ref[0].reshape(nk, H, tm)
    s1 = sc_ref[1].reshape(nk, H, tm)
    a = _top_desc([s0[k] for k in range(nk)], K)
    b = _top_desc([s1[k] for k in range(nk)], K)

    sums = [[a[r] + b[c] for c in range(K // (r + 1))] for r in range(K)]
    pad = lambda xs: xs + [None] * (K - len(xs))
    z = _merge_top(sums[0], pad(sums[1]))
    mid = [x for r in range(2, 7) for x in sums[r]]
    z = _merge_top(z, _sort_desc(pad(mid)))
    low = [x for r in range(7, K) for x in sums[r]]
    z = _merge_top(z, _sort_desc(pad(low)))
    tau = z[K - 1]
    zsum = jnp.ones_like(tau)
    for v in range(1, K):
        zsum = zsum + jnp.exp(z[v] - z[0])
    inv_z = 1.0 / zsum

    alphas = []
    for c in range(K):
        alpha = None
        for r in range(K // (c + 1)):
            cand = jnp.where(sums[r][c] >= tau, a[r], inf)
            alpha = cand if alpha is None else jnp.minimum(alpha, cand)
        alphas.append(alpha)

    for k0 in range(0, nk, ROUTE_KEY_CHUNK):
        blk0 = s0[k0:k0 + ROUTE_KEY_CHUNK]
        blk1 = s1[k0:k0 + ROUTE_KEY_CHUNK]
        count = jnp.zeros(blk0.shape, F32)
        for c in range(K):
            count = jnp.where(blk0 >= alphas[c][None], float(c + 1), count)
        rank = jnp.full(blk1.shape, float(K), F32)
        for c in reversed(range(K)):
            rank = jnp.where(blk1 >= b[c][None], float(c), rank)
        e1 = jnp.exp(blk1 - b[0][None])
        c0 = jnp.exp(blk0 - a[0][None]) * inv_z[None]
        rows = slice(k0 * H, (k0 + ROUTE_KEY_CHUNK) * H)
        for n, val in enumerate((rank, e1, count, c0)):
            tmp_ref[n, rows, :] = val.reshape(ROUTE_KEY_CHUNK * H, tm)

    for n, dst in enumerate((r1_ref, e1_ref, n0_ref, c0_ref)):
        for h in range(H):
            dst[h, :, toks] = tmp_ref[n, pl.ds(h, nk, stride=H), :].astype(dst.dtype)


def _route(xb, wq, sk):
    T = xb.shape[0]
    tm = ROUTE_TM
    H = PEER_HEADS
    nk = PEER_N_KEYS
    subs = tm // LANES
    spec = pl.BlockSpec((H, nk, tm), lambda t: (0, 0, t))
    half = jax.ShapeDtypeStruct((H, nk, T), BF16)
    word = jax.ShapeDtypeStruct((H, nk, T), F32)
    return pl.pallas_call(
        _route_kernel,
        grid=(T // tm,),
        in_specs=[pl.BlockSpec((tm, D_MODEL), lambda t: (t, 0)),
                  pl.BlockSpec(wq.shape, lambda t: (0, 0)),
                  pl.BlockSpec(sk.shape, lambda t: (0, 0, 0))],
        out_specs=[spec, spec, spec, spec],
        out_shape=[half, half, word, word],
        scratch_shapes=[pltpu.VMEM((subs, 2, nk * H, LANES), F32),
                        pltpu.VMEM((subs, 4, nk * H, LANES), F32)],
        compiler_params=_params(("parallel",)),
        name="peer_route",
    )(xb, wq, sk)


FP8 = jnp.float8_e4m3fn
PEER_U_SCALE = 32.0


def _gelu_tanh_scaled(y):
    k = -2.0 * math.sqrt(2.0 / math.pi) * math.log2(math.e) / PEER_U_SCALE
    z = y * (k + (k * 0.044715 / PEER_U_SCALE ** 2) * (y * y))
    return y / (1.0 + jnp.exp2(z))


def _peer_kernel(xb_ref, u_ref, vt_ref, r1_ref, e1_ref, n0_ref, c0_ref, x_ref, g_ref, b_ref,
                 o_ref, ob_ref, w_ref, act_ref, acc_ref, x8_ref, *, rows):
    e = pl.program_id(1)
    nk = PEER_N_KEYS
    tm = xb_ref.shape[0]
    pk = 2 * SUBLANES

    @pl.when(e == 0)
    def _():
        acc_ref[...] = jnp.zeros_like(acc_ref)
        x8_ref[...] = xb_ref[...].astype(FP8)

    tc = min(PEER_LANE_CHUNK, tm)

    def routing_weights(r):
        i = e * rows + r
        first = None
        for ch in range(tm // tc):
            ls = slice(ch * tc, (ch + 1) * tc)
            w = [None] * (nk // pk)
            for h in range(PEER_HEADS):
                count = jnp.broadcast_to(n0_ref[h, pl.ds(i, 1), ls], (pk, tc)).astype(BF16)
                weight = jnp.broadcast_to(c0_ref[h, pl.ds(i, 1), ls], (pk, tc)).astype(BF16)
                for g in range(nk // pk):
                    rs = slice(g * pk, (g + 1) * pk)
                    sel = jnp.where(r1_ref[h, rs, ls] < count, e1_ref[h, rs, ls], jnp.zeros((pk, tc), BF16))
                    term = sel * weight
                    w[g] = term if w[g] is None else w[g] + term
            for g in range(nk // pk):
                w_ref[r * nk + g * pk:r * nk + (g + 1) * pk, ls] = w[g]
            if first is None:
                first = pltpu.bitcast(w[0][:, 0:LANES], jnp.uint32)
        return first

    cr = PEER_CHUNK_ROWS
    zero = 0
    for c in range(rows // cr):
        bits = routing_weights(c * cr)
        for r in range(c * cr + 1, (c + 1) * cr):
            routing_weights(r)
        lo = 0 if c == 0 else pl.multiple_of(c * cr * nk + zero, cr * nk)
        ht = lax.dot_general(u_ref[pl.ds(lo, cr * nk), :], x8_ref[...], _NT,
                             preferred_element_type=F32)
        rows_c = slice(c * cr * nk, (c + 1) * cr * nk)
        act_ref[rows_c, :] = (_gelu_tanh_scaled(ht.astype(BF16)) * w_ref[rows_c, :]).astype(FP8)
        zero = ((bits >> 16) >> 16)[0, 0].astype(jnp.int32)
    acc_ref[...] += jnp.dot(vt_ref[...], act_ref[...], preferred_element_type=F32)

    @pl.when(e == pl.num_programs(1) - 1)
    def _():
        ffn = acc_ref[...].T * (1.0 / PEER_U_SCALE)
        y = _layer_norm(DN_ALPHA * x_ref[...] + ffn, g_ref[...], b_ref[...])
        o_ref[...] = y
        ob_ref[...] = y.astype(BF16)


def _peer(xb, x, u, vt, routing, g, b):
    T = xb.shape[0]
    tm = min(PEER_TM, T)
    rows = PEER_ROWS
    nk = PEER_N_KEYS
    rspec = pl.BlockSpec((PEER_HEADS, nk, tm), lambda t, e: (0, 0, t))
    tok = pl.BlockSpec((tm, D_MODEL), lambda t, e: (t, 0))
    const = pl.BlockSpec((1, D_MODEL), lambda t, e: (0, 0))
    return pl.pallas_call(
        functools.partial(_peer_kernel, rows=rows),
        grid=(T // tm, nk // rows),
        in_specs=[tok,
                  pl.BlockSpec((rows * nk, D_MODEL), lambda t, e: (e, 0)),
                  pl.BlockSpec((D_MODEL, rows * nk), lambda t, e: (0, e)),
                  rspec, rspec, rspec, rspec, tok, const, const],
        out_specs=[tok, tok],
        out_shape=[jax.ShapeDtypeStruct((T, D_MODEL), F32), jax.ShapeDtypeStruct((T, D_MODEL), BF16)],
        scratch_shapes=[pltpu.VMEM((rows * nk, tm), BF16),
                        pltpu.VMEM((rows * nk, tm), FP8),
                        pltpu.VMEM((D_MODEL, tm), F32),
                        pltpu.VMEM((tm, D_MODEL), FP8)],
        compiler_params=_params(("parallel", "arbitrary")),
        name="peer_dense",
    )(xb, u, vt, *routing, x, g, b)


def _rope_tables(S):
    pos = jnp.arange(S, dtype=F32)
    d = MOBA_HEAD_DIM
    inv = ROPE_THETA ** (-jnp.arange(0, d, 2, dtype=F32) / d)
    ang = pos[:, None] * inv[None, :]
    cos, sin = jnp.cos(ang), jnp.sin(ang)
    reps = LANES // d
    cos_m = jnp.tile(cos, (1, 2 * reps))
    sin_m = jnp.concatenate([jnp.tile(-sin, (1, reps)), jnp.tile(sin, (1, reps))], axis=1)
    d = RET_QK_DIM
    inv = 1.0 / (ROPE_THETA ** jnp.linspace(0.0, 1.0, d // 2, dtype=F32))
    ang = pos[:, None] * inv[None, :]
    cos, sin = jnp.cos(ang), jnp.sin(ang)
    cos_r = jnp.concatenate([cos, cos], axis=1)
    sin_r = jnp.concatenate([-sin, sin], axis=1)
    return cos_m, sin_m, cos_r, sin_r


def _moba_half_split(w):
    rows, cols = w.shape
    per = LANES // MOBA_HEAD_DIM
    return (w.reshape(rows, cols // LANES, per, 2, MOBA_HEAD_DIM // 2).transpose(0, 1, 3, 2, 4)
            .reshape(rows, cols))


def _ret_column_perm():
    within = np.concatenate([np.arange(0, RET_QK_DIM, 2), np.arange(1, RET_QK_DIM, 2)])
    return np.concatenate([h * RET_QK_DIM + within for h in range(RET_HEADS)])


def kernel(x, w_in, w_moba_out, w_ret_out, w_out, ln1_g, ln1_b, peer_w_query, peer_sub_keys,
           peer_u, peer_v, ln2_g, ln2_b):
    B, S, D = x.shape
    assert D == D_MODEL and S % MOBA_BLOCK == 0 and S % RET_CHUNK == 0
    T = B * S
    tabs = _rope_tables(S)
    perm = _ret_column_perm()
    o = IN_OFFSETS
    xf = x.reshape(T, D).astype(F32)
    xb = xf.astype(BF16)
    for l in range(DEPTH):
        w = w_in[l]
        w_main = jnp.concatenate(
            [_moba_half_split(w[:, o[0]:o[2]]), w[:, o[3]:o[4]][:, perm], w[:, o[4]:o[5]][:, perm], w[:, o[5]:o[9]]],
            axis=1).astype(BF16)
        w_vt = w[:, o[2]:o[3]].T.astype(BF16)
        proj = _inproj(xb, w_main, tabs, S)
        vt = _moba_values_t(xb, w_vt, S)
        yat = _moba(proj, vt, B, S)
        yr = _retention(proj, B, S)
        xf, xb = _merge(yat, yr, proj, xf, w_moba_out[l].astype(BF16), w_ret_out[l].astype(BF16),
                        w_out[l].astype(BF16), ln1_g[l].reshape(1, D), ln1_b[l].reshape(1, D), B, S)
        routing = _route(xb, peer_w_query[l].astype(BF16), peer_sub_keys[l].astype(BF16))
        xf, xb = _peer(xb, xf, (peer_u[l] * PEER_U_SCALE).astype(FP8), peer_v[l].T.astype(FP8), routing,
                       ln2_g[l].reshape(1, D), ln2_b[l].reshape(1, D))
    return xf.reshape(B, S, D).astype(x.dtype)
```
